```python
import math
import jax, jax.numpy as jnp
from jax import lax
import numpy as np

D_MODEL = 1024
BATCH = 16
SEQ = 256
DEPTH = 2
DEC_BATCH = 2
DEC_SEQ = 4096
PAST_LEN = 256

GRID_W = 64
N_MIXERS = 2
N_HEADS = 16
N_KV_HEADS = 4
HEAD_DIM = 64
GROUP = N_HEADS // N_KV_HEADS
QKV_DIM = (N_HEADS + 2 * N_KV_HEADS) * HEAD_DIM
WINDOW = 128
BLOCK = 128
ROPE_BASE = 10000.0
N_FREQ = HEAD_DIM // 4
POOL_WINDOWS = (2, 4, 8, 16)
N_POOL_GROUPS = 4
POOL_GROUP_DIM = D_MODEL // N_POOL_GROUPS
D_FF = 2816
N_ATTN_LAYERS = (DEPTH + 1) // 2
N_POOL_LAYERS = DEPTH // 2
N_MOD = 9
LN_EPS = 1e-5
DEEPNORM_ALPHA = (2.0 * DEPTH) ** 0.25
DEEPNORM_BETA = (8.0 * DEPTH) ** -0.25
ATTN_SCALE = HEAD_DIM ** -0.5
NEG_INF = -1e30

kernel_name = "hybrid_diffusion_window_gqa_pool_macaron_step"


def layer_norm(x, g, b):
    xf = x.astype(jnp.float32)
    mu = xf.mean(-1, keepdims=True)
    var = jnp.square(xf - mu).mean(-1, keepdims=True)
    y = (xf - mu) * lax.rsqrt(var + LN_EPS)
    return (y * g.astype(jnp.float32) + b.astype(jnp.float32)).astype(x.dtype)


def residual_post_norm(x, out, gate, g, b):
    return layer_norm(DEEPNORM_ALPHA * x + gate * out, g, b)


def adaln_params(cond, w_mod, b_mod):
    m = (jax.nn.silu(cond) @ w_mod + b_mod)[..., None, :]
    return jnp.split(m, N_MOD, axis=-1)


def modulate(x, shift, scale):
    return x * (1.0 + scale) + shift


def swiglu(x, w_gate, w_up, w_down):
    return (jax.nn.silu(x @ w_gate) * (x @ w_up)) @ w_down


def axial_rope_tables(n_rows, dtype):
    rows = jnp.repeat(jnp.arange(n_rows, dtype=jnp.float32), GRID_W)
    cols = jnp.tile(jnp.arange(GRID_W, dtype=jnp.float32), n_rows)
    inv = jnp.power(ROPE_BASE, -jnp.arange(N_FREQ, dtype=jnp.float32) / N_FREQ)
    ang_r = rows[:, None] * inv
    ang_c = cols[:, None] * inv
    ang = jnp.concatenate([ang_r, ang_r, ang_c, ang_c], axis=-1)
    return jnp.cos(ang).astype(dtype), jnp.sin(ang).astype(dtype)


def apply_axial_rope(x, cos, sin):
    r1, r2, c1, c2 = jnp.split(x, 4, axis=-1)
    rot = jnp.concatenate([-r2, r1, -c2, c1], axis=-1)
    return x * cos[None, :, None, :] + rot * sin[None, :, None, :]


def split_qkv(h, w_qkv):
    B, S, _ = h.shape
    qkv = h @ w_qkv
    q, k, v = jnp.split(qkv, [N_HEADS * HEAD_DIM, (N_HEADS + N_KV_HEADS) * HEAD_DIM], axis=-1)
    q = q.reshape(B, S, N_HEADS, HEAD_DIM)
    k = k.reshape(B, S, N_KV_HEADS, HEAD_DIM)
    v = v.reshape(B, S, N_KV_HEADS, HEAD_DIM)
    return q, k, v


def to_query_blocks(q):
    B, S = q.shape[:2]
    qb = q.reshape(B, S // BLOCK, BLOCK, N_KV_HEADS, GROUP, HEAD_DIM)
    return qb.transpose(1, 0, 2, 3, 4, 5)


def from_query_blocks(o):
    nb, B = o.shape[:2]
    return o.transpose(1, 0, 2, 3, 4, 5).reshape(B, nb * BLOCK, N_HEADS * HEAD_DIM)


def sink_column(sink, score_shape):
    s = sink.astype(jnp.float32).reshape(N_KV_HEADS, GROUP)[None, :, :, None, None]
    return jnp.broadcast_to(s, score_shape[:-1] + (1,))


def context_attention(q, k, v, sink):
    L = k.shape[1]

    def one_block(q_blk):
        s = jnp.einsum('bqkgd,bjkd->bkgqj', q_blk, k).astype(jnp.float32) * ATTN_SCALE
        logits = jnp.concatenate([s, sink_column(sink, s.shape)], axis=-1)
        p = jax.nn.softmax(logits, axis=-1)[..., :L].astype(v.dtype)
        return jnp.einsum('bkgqj,bjkd->bqkgd', p, v)

    return from_query_blocks(lax.map(one_block, to_query_blocks(q)))


def latent_attention(q, k, v, k_ctx, v_ctx, sink):
    S = q.shape[1]
    L = k_ctx.shape[1]
    nb = S // BLOCK
    pad = ((0, 0), (BLOCK, BLOCK), (0, 0), (0, 0))
    k_pad = jnp.pad(k, pad)
    v_pad = jnp.pad(v, pad)
    offs_q = jnp.arange(BLOCK)
    offs_k = jnp.arange(3 * BLOCK) - BLOCK

    def one_block(args):
        q_blk, b = args
        start = b * BLOCK
        kb = lax.dynamic_slice_in_dim(k_pad, start, 3 * BLOCK, axis=1)
        vb = lax.dynamic_slice_in_dim(v_pad, start, 3 * BLOCK, axis=1)
        qpos = start + offs_q
        kpos = start + offs_k
        valid = ((jnp.abs(qpos[:, None] - kpos[None, :]) <= WINDOW)
                 & (kpos >= 0)[None, :] & (kpos < S)[None, :])
        s_loc = jnp.einsum('bqkgd,bjkd->bkgqj', q_blk, kb).astype(jnp.float32) * ATTN_SCALE
        s_loc = jnp.where(valid, s_loc, NEG_INF)
        s_ctx = jnp.einsum('bqkgd,bjkd->bkgqj', q_blk, k_ctx).astype(jnp.float32) * ATTN_SCALE
        logits = jnp.concatenate([s_loc, s_ctx, sink_column(sink, s_loc.shape)], axis=-1)
        p = jax.nn.softmax(logits, axis=-1).astype(v.dtype)
        p_loc = p[..., :3 * BLOCK]
        p_ctx = p[..., 3 * BLOCK:3 * BLOCK + L]
        return (jnp.einsum('bkgqj,bjkd->bqkgd', p_loc, vb)
                + jnp.einsum('bkgqj,bjkd->bqkgd', p_ctx, v_ctx))

    return from_query_blocks(lax.map(one_block, (to_query_blocks(q), jnp.arange(nb))))


def attn_context(h, w_qkv, w_o, sink):
    q, k, v = split_qkv(h, w_qkv)
    o = context_attention(q, k, v, sink)
    return o @ w_o, k, v


def attn_latent(h, k_ctx, v_ctx, cos, sin, w_qkv, w_o, sink):
    q, k, v = split_qkv(h, w_qkv)
    q = apply_axial_rope(q, cos, sin)
    k = apply_axial_rope(k, cos, sin)
    o = latent_attention(q, k, v, k_ctx, v_ctx, sink)
    return o @ w_o


def multiscale_pool(h, w_pool, scale):
    B, S, D = h.shape
    hf = h.astype(jnp.float32)
    cs = jnp.concatenate([jnp.zeros((B, 1, D), jnp.float32), jnp.cumsum(hf, axis=1)], axis=1)
    t = jnp.arange(S)
    outs = []
    for gi, w in enumerate(POOL_WINDOWS):
        lo = jnp.clip(t - w // 2, 0, S)
        hi = jnp.clip(t + w // 2, 0, S)
        sl = slice(gi * POOL_GROUP_DIM, (gi + 1) * POOL_GROUP_DIM)
        cg = cs[..., sl]
        total = jnp.take(cg, hi, axis=1) - jnp.take(cg, lo, axis=1)
        cnt = (hi - lo).astype(jnp.float32)[None, :, None]
        pooled = (total / cnt - hf[..., sl]).astype(h.dtype)
        outs.append(pooled @ w_pool[gi])
    return jnp.concatenate(outs, axis=-1) * scale


def setup_inputs(seed: int = 0) -> dict:
    key = jax.random.key(seed)
    ks = jax.random.split(key, 20)
    f32 = jnp.float32
    nrm = lambda k, shape: jax.random.normal(k, shape, f32)
    d_inner = N_HEADS * HEAD_DIM
    return {
        'x_prompt': nrm(ks[0], (BATCH, SEQ, D_MODEL)),
        'x_sample': nrm(ks[1], (DEC_BATCH, DEC_SEQ, D_MODEL)),
        'cache_k': nrm(ks[2], (DEC_BATCH, N_ATTN_LAYERS, PAST_LEN, N_KV_HEADS, HEAD_DIM)),
        'cache_v': nrm(ks[3], (DEC_BATCH, N_ATTN_LAYERS, PAST_LEN, N_KV_HEADS, HEAD_DIM)),
        'c': nrm(ks[4], (DEC_BATCH, D_MODEL)),
        'c_ctx': nrm(ks[5], (D_MODEL,)),
        'w_mod': nrm(ks[6], (DEPTH, D_MODEL, N_MOD * D_MODEL)) * (0.5 * D_MODEL ** -0.5),
        'b_mod': nrm(ks[7], (DEPTH, N_MOD * D_MODEL)) * 0.01,
        'ln_g': 1.0 + 0.05 * nrm(ks[8], (DEPTH, 3, D_MODEL)),
        'ln_b': 0.02 * nrm(ks[9], (DEPTH, 3, D_MODEL)),
        'ffn_w_gate': nrm(ks[10], (DEPTH, 2, D_MODEL, D_FF)) * D_MODEL ** -0.5,
        'ffn_w_up': nrm(ks[11], (DEPTH, 2, D_MODEL, D_FF)) * D_MODEL ** -0.5,
        'ffn_w_down': nrm(ks[12], (DEPTH, 2, D_FF, D_MODEL)) * (DEEPNORM_BETA * D_FF ** -0.5),
        'attn_w_qkv': nrm(ks[13], (N_ATTN_LAYERS, D_MODEL, QKV_DIM)) * D_MODEL ** -0.5,
        'attn_w_o': nrm(ks[14], (N_ATTN_LAYERS, d_inner, D_MODEL)) * (DEEPNORM_BETA * d_inner ** -0.5),
        'attn_sink': 0.5 * nrm(ks[15], (N_ATTN_LAYERS, N_HEADS)),
        'pool_w': nrm(ks[16], (N_POOL_LAYERS, N_POOL_GROUPS, POOL_GROUP_DIM, POOL_GROUP_DIM)) * (DEEPNORM_BETA * POOL_GROUP_DIM ** -0.5),
        'pool_scale': 1.0 + 0.1 * nrm(ks[17], (N_POOL_LAYERS, D_MODEL)),
    }


def reference(x_prompt, x_sample, cache_k, cache_v, c, c_ctx, w_mod, b_mod, ln_g, ln_b,
              ffn_w_gate, ffn_w_up, ffn_w_down, attn_w_qkv, attn_w_o, attn_sink,
              pool_w, pool_scale):
    n_rows = x_sample.shape[1] // GRID_W
    cos, sin = axial_rope_tables(n_rows, x_sample.dtype)
    xp, xs = x_prompt, x_sample
    new_k, new_v = [], []
    for i in range(DEPTH):
        mix = i % N_MIXERS
        j = i // N_MIXERS
        mp = adaln_params(c_ctx, w_mod[i], b_mod[i])
        ms = adaln_params(c, w_mod[i], b_mod[i])

        fp = swiglu(modulate(xp, mp[0], mp[1]), ffn_w_gate[i, 0], ffn_w_up[i, 0], ffn_w_down[i, 0])
        fs = swiglu(modulate(xs, ms[0], ms[1]), ffn_w_gate[i, 0], ffn_w_up[i, 0], ffn_w_down[i, 0])
        xp = residual_post_norm(xp, 0.5 * fp, mp[2], ln_g[i, 0], ln_b[i, 0])
        xs = residual_post_norm(xs, 0.5 * fs, ms[2], ln_g[i, 0], ln_b[i, 0])

        hp = modulate(xp, mp[3], mp[4])
        hs = modulate(xs, ms[3], ms[4])
        if mix == 0:
            op, k_p, v_p = attn_context(hp, attn_w_qkv[j], attn_w_o[j], attn_sink[j])
            new_k.append(k_p)
            new_v.append(v_p)
            os_ = attn_latent(hs, cache_k[:, j], cache_v[:, j], cos, sin,
                              attn_w_qkv[j], attn_w_o[j], attn_sink[j])
        else:
            op = multiscale_pool(hp, pool_w[j], pool_scale[j])
            os_ = multiscale_pool(hs, pool_w[j], pool_scale[j])
        xp = residual_post_norm(xp, op, mp[5], ln_g[i, 1], ln_b[i, 1])
        xs = residual_post_norm(xs, os_, ms[5], ln_g[i, 1], ln_b[i, 1])

        fp = swiglu(modulate(xp, mp[6], mp[7]), ffn_w_gate[i, 1], ffn_w_up[i, 1], ffn_w_down[i, 1])
        fs = swiglu(modulate(xs, ms[6], ms[7]), ffn_w_gate[i, 1], ffn_w_up[i, 1], ffn_w_down[i, 1])
        xp = residual_post_norm(xp, 0.5 * fp, mp[8], ln_g[i, 2], ln_b[i, 2])
        xs = residual_post_norm(xs, 0.5 * fs, ms[8], ln_g[i, 2], ln_b[i, 2])

    state_k = jnp.stack(new_k, axis=1)
    state_v = jnp.stack(new_v, axis=1)
    return (xp, xs, state_k, state_v)
```

```python
import functools

import jax
import jax.numpy as jnp
from jax import lax
from jax.experimental import pallas as pl
from jax.experimental.pallas import tpu as pltpu

D_MODEL = 1024
BATCH = 16
SEQ = 256
DEPTH = 2
DEC_BATCH = 2
DEC_SEQ = 4096
PAST_LEN = 256
GRID_W = 64
N_HEADS = 16
N_KV_HEADS = 4
HEAD_DIM = 64
WINDOW = 128
BLOCK = 128
ROPE_BASE = 10000.0
N_FREQ = HEAD_DIM // 4
POOL_WINDOWS = (2, 4, 8, 16)
POOL_GROUP_DIM = D_MODEL // 4
D_FF = 2816
N_MOD = 9
LN_EPS = 1e-5
DEEPNORM_ALPHA = (2.0 * DEPTH) ** 0.25
ATTN_SCALE = HEAD_DIM ** -0.5
NEG_INF = -1e30

N_PROMPT_TOK = BATCH * SEQ
N_TOK = N_PROMPT_TOK + DEC_BATCH * DEC_SEQ
MOD_GROUP_ROWS = 4096
N_COND = 1 + DEC_BATCH
COND_ROWS = 8

LANES = 128
HALF_LANES = LANES // 2
TM = 512
FF_CHUNK = 256
POOL_TM = 256
POOL_HALO = 8
QKV_COLS = N_HEADS * HEAD_DIM + 2 * N_KV_HEADS * LANES
KV_DUP = N_KV_HEADS * LANES
VMEM_LIMIT = 56 * 1024 * 1024

F32 = jnp.float32
BF16 = jnp.bfloat16


def _layer_norm(y, g, b):
    mu = jnp.mean(y, axis=-1, keepdims=True)
    yc = y - mu
    var = jnp.mean(yc * yc, axis=-1, keepdims=True)
    return yc * lax.rsqrt(var + LN_EPS) * g + b


def _modulate(x, mod_ref, row0):
    shift = mod_ref[0, row0:row0 + 1, :]
    scale = mod_ref[0, row0 + 1:row0 + 2, :]
    return x * (1.0 + scale) + shift


def _params(n_axes=1):
    return pltpu.CompilerParams(
        dimension_semantics=("arbitrary",) * n_axes,
        vmem_limit_bytes=VMEM_LIMIT)


def _resident(shape):
    nd = len(shape)
    return pl.BlockSpec(shape, lambda *_: (0,) * nd, pipeline_mode=pl.Buffered(1))


def _mod_spec(tile_rows):
    tiles_per_group = MOD_GROUP_ROWS // tile_rows
    return pl.BlockSpec((1, N_MOD, D_MODEL), lambda i: (i // tiles_per_group, 0, 0))


ADALN_NC = 1024


def _adaln_kernel(cond_ref, w_ref, b_ref, o_ref):
    c = cond_ref[...]
    s = (c * (1.0 / (1.0 + jnp.exp(-c)))).astype(BF16)
    o_ref[0] = jnp.dot(s, w_ref[0].astype(BF16), preferred_element_type=F32) + b_ref[0]


def _adaln(cond, w_mod, b_mod):
    n_out = N_MOD * D_MODEL
    out = pl.pallas_call(
        _adaln_kernel,
        grid=(DEPTH, n_out // ADALN_NC),
        in_specs=[
            pl.BlockSpec((COND_ROWS, D_MODEL), lambda l, n: (0, 0)),
            pl.BlockSpec((1, D_MODEL, ADALN_NC), lambda l, n: (l, 0, n)),
            pl.BlockSpec((1, 1, ADALN_NC), lambda l, n: (l, 0, n)),
        ],
        out_specs=pl.BlockSpec((1, COND_ROWS, ADALN_NC), lambda l, n: (l, 0, n)),
        out_shape=jax.ShapeDtypeStruct((DEPTH, COND_ROWS, n_out), F32),
        compiler_params=_params(2),
        name="adaln",
    )(cond, w_mod, b_mod.reshape(DEPTH, 1, n_out))
    return out[:, :N_COND].reshape(DEPTH, N_COND, N_MOD, D_MODEL)


def _ffn_kernel(x_ref, mod_ref, lng_ref, lnb_ref, wg_ref, wu_ref, wd_ref, o_ref, a_ref, *, row0):
    x = x_ref[...]
    h = _modulate(x, mod_ref, row0).astype(BF16)
    for c in range(D_FF // FF_CHUNK):
        sl = slice(c * FF_CHUNK, (c + 1) * FF_CHUNK)
        g = jnp.dot(h, wg_ref[:, sl], preferred_element_type=F32)
        u = jnp.dot(h, wu_ref[:, sl], preferred_element_type=F32)
        a_ref[:, sl] = (g * (1.0 / (1.0 + jnp.exp(-g))) * u).astype(BF16)
    f = jnp.dot(a_ref[...], wd_ref[...], preferred_element_type=F32)
    gate = mod_ref[0, row0 + 2:row0 + 3, :]
    y = DEEPNORM_ALPHA * x + (0.5 * gate) * f
    o_ref[...] = _layer_norm(y, lng_ref[...], lnb_ref[...])


def _ffn(x, mods, row0, ln_g, ln_b, wg, wu, wd):
    return pl.pallas_call(
        functools.partial(_ffn_kernel, row0=row0),
        grid=(N_TOK // TM,),
        in_specs=[
            pl.BlockSpec((TM, D_MODEL), lambda i: (i, 0)),
            _mod_spec(TM),
            _resident((1, D_MODEL)),
            _resident((1, D_MODEL)),
            _resident((D_MODEL, D_FF)),
            _resident((D_MODEL, D_FF)),
            _resident((D_FF, D_MODEL)),
        ],
        out_specs=pl.BlockSpec((TM, D_MODEL), lambda i: (i, 0)),
        out_shape=jax.ShapeDtypeStruct((N_TOK, D_MODEL), F32),
        scratch_shapes=[pltpu.VMEM((TM, D_FF), BF16)],
        compiler_params=_params(),
        name="ffn",
    )(x, mods, ln_g.reshape(1, D_MODEL), ln_b.reshape(1, D_MODEL), wg, wu, wd)


ROPE_TILES = DEC_SEQ // TM
N_PROMPT_TILES = N_PROMPT_TOK // TM


def _qkv_kernel(x_ref, mod_ref, w_ref, cos_ref, sup_ref, sdn_ref, q_ref, k_ref, v_ref):
    h = _modulate(x_ref[...], mod_ref, 3).astype(BF16)
    qkv = jnp.dot(h, w_ref[...], preferred_element_type=F32)
    cos = cos_ref[...]
    s_up = sup_ref[...]
    s_dn = sdn_ref[...]
    n_q_blocks = N_HEADS * HEAD_DIM // LANES
    for j in range(n_q_blocks + N_KV_HEADS):
        blk = qkv[:, j * LANES:(j + 1) * LANES]
        up = pltpu.roll(blk, LANES - 16, 1)
        dn = pltpu.roll(blk, 16, 1)
        roped = blk * cos + up * s_up + dn * s_dn
        if j < n_q_blocks:
            q_ref[:, j * LANES:(j + 1) * LANES] = (roped * ATTN_SCALE).astype(BF16)
        else:
            jk = j - n_q_blocks
            k_ref[:, jk * LANES:(jk + 1) * LANES] = roped
    v_ref[...] = qkv[:, N_HEADS * HEAD_DIM + KV_DUP:]


def _rope_index(i):
    return (jnp.where(i < N_PROMPT_TILES, ROPE_TILES, (i - N_PROMPT_TILES) % ROPE_TILES), 0)


def _qkv(x, mods, w_dup, cos_t, sup_t, sdn_t):
    return pl.pallas_call(
        _qkv_kernel,
        grid=(N_TOK // TM,),
        in_specs=[
            pl.BlockSpec((TM, D_MODEL), lambda i: (i, 0)),
            _mod_spec(TM),
            _resident((D_MODEL, QKV_COLS)),
            pl.BlockSpec((TM, LANES), _rope_index),
            pl.BlockSpec((TM, LANES), _rope_index),
            pl.BlockSpec((TM, LANES), _rope_index),
        ],
        out_specs=[
            pl.BlockSpec((TM, N_HEADS * HEAD_DIM), lambda i: (i, 0)),
            pl.BlockSpec((TM, KV_DUP), lambda i: (i, 0)),
            pl.BlockSpec((TM, KV_DUP), lambda i: (i, 0)),
        ],
        out_shape=[
            jax.ShapeDtypeStruct((N_TOK, N_HEADS * HEAD_DIM), BF16),
            jax.ShapeDtypeStruct((N_TOK, KV_DUP), F32),
            jax.ShapeDtypeStruct((N_TOK, KV_DUP), F32),
        ],
        compiler_params=_params(),
        name="qkv",
    )(x, mods, w_dup, cos_t, sup_t, sdn_t)


def _split_halves(x):
    lo = lax.broadcasted_iota(jnp.int32, x.shape, 1) < HALF_LANES
    return jnp.where(lo, x, 0.0).astype(BF16), jnp.where(lo, 0.0, x).astype(BF16)


def _softmax_with_sink(s, sink):
    m = jnp.maximum(jnp.max(s, axis=-1, keepdims=True), sink)
    e = jnp.exp(s - m)
    den = jnp.sum(e, axis=-1, keepdims=True) + jnp.exp(sink - m)
    return (e * (1.0 / den)).astype(BF16)


def _attend(q_ref, o_ref, sink_ref, k_blocks, v_blocks, mask):
    nt = (((1,), (1,)), ((), ()))
    for kh in range(N_KV_HEADS):
        halves_k = [_split_halves(kb) for kb in k_blocks(kh)]
        halves_v = [_split_halves(vb) for vb in v_blocks(kh)]
        k_a = jnp.concatenate([hk[0] for hk in halves_k], axis=0)
        k_b = jnp.concatenate([hk[1] for hk in halves_k], axis=0)
        v_ab = jnp.concatenate([hv[0] for hv in halves_v] + [hv[1] for hv in halves_v], axis=0)
        for jj in range(2):
            j = 2 * kh + jj
            qb = q_ref[:, j * LANES:(j + 1) * LANES]
            s_a = lax.dot_general(qb, k_a, nt, preferred_element_type=F32)
            s_b = lax.dot_general(qb, k_b, nt, preferred_element_type=F32)
            if mask is not None:
                s_a = jnp.where(mask, s_a, NEG_INF)
                s_b = jnp.where(mask, s_b, NEG_INF)
            p_a = _softmax_with_sink(s_a, sink_ref[2 * j])
            p_b = _softmax_with_sink(s_b, sink_ref[2 * j + 1])
            p = jnp.concatenate([p_a, p_b], axis=1)
            o_ref[:, j * LANES:(j + 1) * LANES] = jnp.dot(
                p, v_ab, preferred_element_type=F32).astype(BF16)


def _ctx_attn_kernel(sink_ref, q_ref, k_ref, v_ref, o_ref):
    _attend(q_ref, o_ref, sink_ref,
            lambda kh: [k_ref[:, kh * LANES:(kh + 1) * LANES]],
            lambda kh: [v_ref[:, kh * LANES:(kh + 1) * LANES]],
            None)


def _ctx_attn(sink, q, k, v):
    return pl.pallas_call(
        _ctx_attn_kernel,
        grid=(BATCH,),
        in_specs=[
            pl.BlockSpec(memory_space=pltpu.SMEM),
            pl.BlockSpec((SEQ, N_HEADS * HEAD_DIM), lambda b: (b, 0)),
            pl.BlockSpec((SEQ, KV_DUP), lambda b: (b, 0)),
            pl.BlockSpec((SEQ, KV_DUP), lambda b: (b, 0)),
        ],
        out_specs=pl.BlockSpec((SEQ, N_HEADS * HEAD_DIM), lambda b: (b, 0)),
        out_shape=jax.ShapeDtypeStruct((N_PROMPT_TOK, N_HEADS * HEAD_DIM), BF16),
        compiler_params=_params(),
        name="ctx_attn",
    )(sink, q, k, v)


N_QBLK = DEC_SEQ // BLOCK
N_LOCAL = 3 * BLOCK


def _lat_attn_kernel(sink_ref, q_ref, kp_ref, kc_ref, kn_ref, vp_ref, vc_ref, vn_ref,
                     kx_ref, vx_ref, o_ref):
    qb = pl.program_id(1)
    k_min = jnp.where(qb > 0, -BLOCK, 0)
    k_max = jnp.where(qb < N_QBLK - 1, 2 * BLOCK, BLOCK)
    n_keys = N_LOCAL + PAST_LEN
    qpos = lax.broadcasted_iota(jnp.int32, (BLOCK, n_keys), 0)
    col = lax.broadcasted_iota(jnp.int32, (BLOCK, n_keys), 1)
    kpos = col - BLOCK
    local_ok = (jnp.abs(qpos - kpos) <= WINDOW) & (kpos >= k_min) & (kpos < k_max)
    mask = local_ok | (col >= N_LOCAL)

    def k_blocks(kh):
        sl = slice(kh * LANES, (kh + 1) * LANES)
        return [kp_ref[:, sl], kc_ref[:, sl], kn_ref[:, sl], kx_ref[0, :, sl]]

    def v_blocks(kh):
        sl = slice(kh * LANES, (kh + 1) * LANES)
        return [vp_ref[:, sl], vc_ref[:, sl], vn_ref[:, sl], vx_ref[0, :, sl]]

    _attend(q_ref, o_ref, sink_ref, k_blocks, v_blocks, mask)


def _lat_attn(sink, q, k, v, k_ctx, v_ctx):
    first = N_PROMPT_TOK // BLOCK

    def cur(b, i):
        return (first + b * N_QBLK + i, 0)

    def prev(b, i):
        return (first + b * N_QBLK + jnp.maximum(i - 1, 0), 0)

    def nxt(b, i):
        return (first + b * N_QBLK + jnp.minimum(i + 1, N_QBLK - 1), 0)

    kv_spec = lambda f: pl.BlockSpec((BLOCK, KV_DUP), f)
    ctx_spec = pl.BlockSpec((1, PAST_LEN, KV_DUP), lambda b, i: (b, 0, 0))
    return pl.pallas_call(
        _lat_attn_kernel,
        grid=(DEC_BATCH, N_QBLK),
        in_specs=[
            pl.BlockSpec(memory_space=pltpu.SMEM),
            pl.BlockSpec((BLOCK, N_HEADS * HEAD_DIM), cur),
            kv_spec(prev), kv_spec(cur), kv_spec(nxt),
            kv_spec(prev), kv_spec(cur), kv_spec(nxt),
            ctx_spec, ctx_spec,
        ],
        out_specs=pl.BlockSpec((BLOCK, N_HEADS * HEAD_DIM), lambda b, i: (b * N_QBLK + i, 0)),
        out_shape=jax.ShapeDtypeStruct((DEC_BATCH * DEC_SEQ, N_HEADS * HEAD_DIM), BF16),
        compiler_params=_params(2),
        name="lat_attn",
    )(sink, q, k, k, k, v, v, v, k_ctx, v_ctx)


def _proj_kernel(x_ref, a_ref, mod_ref, lng_ref, lnb_ref, wo_ref, o_ref):
    f = jnp.dot(a_ref[...], wo_ref[...], preferred_element_type=F32)
    gate = mod_ref[0, 5:6, :]
    y = DEEPNORM_ALPHA * x_ref[...] + gate * f
    o_ref[...] = _layer_norm(y, lng_ref[...], lnb_ref[...])


def _proj(x, attn, mods, ln_g, ln_b, w_o):
    return pl.pallas_call(
        _proj_kernel,
        grid=(N_TOK // TM,),
        in_specs=[
            pl.BlockSpec((TM, D_MODEL), lambda i: (i, 0)),
            pl.BlockSpec((TM, N_HEADS * HEAD_DIM), lambda i: (i, 0)),
            _mod_spec(TM),
            _resident((1, D_MODEL)),
            _resident((1, D_MODEL)),
            _resident((N_HEADS * HEAD_DIM, D_MODEL)),
        ],
        out_specs=pl.BlockSpec((TM, D_MODEL), lambda i: (i, 0)),
        out_shape=jax.ShapeDtypeStruct((N_TOK, D_MODEL), F32),
        compiler_params=_params(),
        name="attn_proj",
    )(x, attn, mods, ln_g.reshape(1, D_MODEL), ln_b.reshape(1, D_MODEL), w_o)


PROMPT_POOL_TILES = N_PROMPT_TOK // POOL_TM
LATENT_POOL_TILES = DEC_SEQ // POOL_TM


def _pool_kernel(x_ref, xp_ref, xn_ref, mod_ref, lng_ref, lnb_ref, w_ref, sc_ref, o_ref, h_ref):
    i = pl.program_id(0)
    in_seq = (i - PROMPT_POOL_TILES) % LATENT_POOL_TILES
    is_start = (i < PROMPT_POOL_TILES) | (in_seq == 0)
    is_end = (i < PROMPT_POOL_TILES) | (in_seq == LATENT_POOL_TILES - 1)

    x = x_ref[...]
    h = _modulate(x, mod_ref, 3)
    h_ref[POOL_HALO:POOL_HALO + POOL_TM, :] = h
    h_ref[0:POOL_HALO, :] = jnp.where(is_start, 0.0, _modulate(xp_ref[...], mod_ref, 3))
    h_ref[POOL_HALO + POOL_TM:, :] = jnp.where(is_end, 0.0, _modulate(xn_ref[...], mod_ref, 3))

    r = lax.broadcasted_iota(jnp.int32, (POOL_TM, POOL_GROUP_DIM), 0)
    gate = mod_ref[0, 5:6, :]
    outs = []
    for gi, w in enumerate(POOL_WINDOWS):
        half = w // 2
        cols = slice(gi * POOL_GROUP_DIM, (gi + 1) * POOL_GROUP_DIM)
        total = h_ref[POOL_HALO - half:POOL_HALO - half + POOL_TM, cols]
        for d in range(-half + 1, half):
            total = total + h_ref[POOL_HALO + d:POOL_HALO + d + POOL_TM, cols]
        clipped_lo = jnp.where(is_start, jnp.maximum(half - r, 0), 0)
        clipped_hi = jnp.where(is_end, jnp.maximum(r + half - POOL_TM, 0), 0)
        cnt = (w - clipped_lo - clipped_hi).astype(F32)
        pooled = (total / cnt - h[:, cols]).astype(BF16)
        outs.append(jnp.dot(pooled, w_ref[gi], preferred_element_type=F32))
    mixed = jnp.concatenate(outs, axis=-1) * sc_ref[...]
    y = DEEPNORM_ALPHA * x + gate * mixed
    o_ref[...] = _layer_norm(y, lng_ref[...], lnb_ref[...])


def _pool(x, mods, ln_g, ln_b, w_pool, scale):
    halo_per_tile = POOL_TM // POOL_HALO
    last_halo = N_TOK // POOL_HALO - 1
    return pl.pallas_call(
        _pool_kernel,
        grid=(N_TOK // POOL_TM,),
        in_specs=[
            pl.BlockSpec((POOL_TM, D_MODEL), lambda i: (i, 0)),
            pl.BlockSpec((POOL_HALO, D_MODEL), lambda i: (jnp.maximum(i * halo_per_tile - 1, 0), 0)),
            pl.BlockSpec((POOL_HALO, D_MODEL),
                         lambda i: (jnp.minimum((i + 1) * halo_per_tile, last_halo), 0)),
            _mod_spec(POOL_TM),
            _resident((1, D_MODEL)),
            _resident((1, D_MODEL)),
            _resident((len(POOL_WINDOWS), POOL_GROUP_DIM, POOL_GROUP_DIM)),
            _resident((1, D_MODEL)),
        ],
        out_specs=pl.BlockSpec((POOL_TM, D_MODEL), lambda i: (i, 0)),
        out_shape=jax.ShapeDtypeStruct((N_TOK, D_MODEL), F32),
        scratch_shapes=[pltpu.VMEM((POOL_TM + 2 * POOL_HALO, D_MODEL), F32)],
        compiler_params=_params(),
        name="pool",
    )(x, x, x, mods, ln_g.reshape(1, D_MODEL), ln_b.reshape(1, D_MODEL), w_pool,
      scale.reshape(1, D_MODEL))


def _dup_heads(a):
    lead = a.shape[:-1]
    a = a.reshape(lead + (N_KV_HEADS, 1, HEAD_DIM))
    a = jnp.broadcast_to(a, lead + (N_KV_HEADS, 2, HEAD_DIM))
    return a.reshape(lead + (KV_DUP,))


def _undup_heads(a):
    return a.reshape(a.shape[0], N_KV_HEADS, 2, HEAD_DIM)[:, :, 0, :]


def _rope_tables():
    n_rows = DEC_SEQ // GRID_W
    rows = jnp.repeat(jnp.arange(n_rows, dtype=F32), GRID_W)
    cols = jnp.tile(jnp.arange(GRID_W, dtype=F32), n_rows)
    inv = jnp.power(ROPE_BASE, -jnp.arange(N_FREQ, dtype=F32) / N_FREQ)
    ang_r = rows[:, None] * inv
    ang_c = cols[:, None] * inv
    ang = jnp.concatenate([ang_r, ang_r, ang_c, ang_c], axis=-1)
    cos = jnp.tile(jnp.cos(ang), (1, LANES // HEAD_DIM))
    sin = jnp.tile(jnp.sin(ang), (1, LANES // HEAD_DIM))
    first_half = (jnp.arange(LANES) % 32) < 16
    s_up = jnp.where(first_half, -sin, 0.0)
    s_dn = jnp.where(first_half, 0.0, sin)
    ident = jnp.ones((TM, LANES), F32)
    zeros = jnp.zeros((TM, LANES), F32)
    return (jnp.concatenate([cos, ident], axis=0),
            jnp.concatenate([s_up, zeros], axis=0),
            jnp.concatenate([s_dn, zeros], axis=0))


def kernel(x_prompt, x_sample, cache_k, cache_v, c, c_ctx, w_mod, b_mod, ln_g, ln_b,
           ffn_w_gate, ffn_w_up, ffn_w_down, attn_w_qkv, attn_w_o, attn_sink,
           pool_w, pool_scale):
    cond = jnp.concatenate(
        [c_ctx[None, :], c, jnp.zeros((COND_ROWS - N_COND, D_MODEL), F32)], axis=0)
    mods = _adaln(cond, w_mod, b_mod)

    x = jnp.concatenate([x_prompt.reshape(N_PROMPT_TOK, D_MODEL),
                         x_sample.reshape(DEC_BATCH * DEC_SEQ, D_MODEL)], axis=0)
    wg = ffn_w_gate.astype(BF16)
    wu = ffn_w_up.astype(BF16)
    wd = ffn_w_down.astype(BF16)

    x = _ffn(x, mods[0], 0, ln_g[0, 0], ln_b[0, 0], wg[0, 0], wu[0, 0], wd[0, 0])
    n_q = N_HEADS * HEAD_DIM
    n_kv = N_KV_HEADS * HEAD_DIM
    w_qkv = attn_w_qkv[0]
    w_dup = jnp.concatenate(
        [w_qkv[:, :n_q], _dup_heads(w_qkv[:, n_q:n_q + n_kv]), _dup_heads(w_qkv[:, n_q + n_kv:])],
        axis=1).astype(BF16)
    cos_t, sup_t, sdn_t = _rope_tables()
    q, k, v = _qkv(x, mods[0], w_dup, cos_t, sup_t, sdn_t)
    sink = attn_sink[0]
    o_ctx = _ctx_attn(sink, q, k, v)
    k_ctx = _dup_heads(cache_k[:, 0].reshape(DEC_BATCH, PAST_LEN, n_kv))
    v_ctx = _dup_heads(cache_v[:, 0].reshape(DEC_BATCH, PAST_LEN, n_kv))
    o_lat = _lat_attn(sink, q, k, v, k_ctx, v_ctx)
    attn = jnp.concatenate([o_ctx, o_lat], axis=0)
    x = _proj(x, attn, mods[0], ln_g[0, 1], ln_b[0, 1], attn_w_o[0].astype(BF16))
    x = _ffn(x, mods[0], 6, ln_g[0, 2], ln_b[0, 2], wg[0, 1], wu[0, 1], wd[0, 1])

    x = _ffn(x, mods[1], 0, ln_g[1, 0], ln_b[1, 0], wg[1, 0], wu[1, 0], wd[1, 0])
    x = _pool(x, mods[1], ln_g[1, 1], ln_b[1, 1], pool_w[0].astype(BF16), pool_scale[0])
    x = _ffn(x, mods[1], 6, ln_g[1, 2], ln_b[1, 2], wg[1, 1], wu[1, 1], wd[1, 1])

    y_prompt = x[:N_PROMPT_TOK].reshape(BATCH, SEQ, D_MODEL)
    y_sample = x[N_PROMPT_TOK:].reshape(DEC_BATCH, DEC_SEQ, D_MODEL)
    state_k = _undup_heads(k[:N_PROMPT_TOK]).reshape(BATCH, 1, SEQ, N_KV_HEADS, HEAD_DIM)
    state_v = _undup_heads(v[:N_PROMPT_TOK]).reshape(BATCH, 1, SEQ, N_KV_HEADS, HEAD_DIM)
    return (y_prompt, y_sample, state_k, state_v)
```

```python
import functools

import jax
import jax.numpy as jnp
from jax import lax
from jax.experimental import pallas as pl
from jax.experimental.pallas import tpu as pltpu

D_MODEL = 1024
BATCH = 16
SEQ = 256
DEPTH = 2
DEC_BATCH = 2
DEC_SEQ = 4096
PAST_LEN = 256
GRID_W = 64
N_HEADS = 16
N_KV_HEADS = 4
HEAD_DIM = 64
WINDOW = 128
BLOCK = 128
ROPE_BASE = 10000.0
N_FREQ = HEAD_DIM // 4
POOL_WINDOWS = (2, 4, 8, 16)
POOL_GROUP_DIM = D_MODEL // 4
D_FF = 2816
N_MOD = 9
LN_EPS = 1e-5
DEEPNORM_ALPHA = (2.0 * DEPTH) ** 0.25
ATTN_SCALE = HEAD_DIM ** -0.5
NEG_INF = -1e30

N_PROMPT_TOK = BATCH * SEQ
N_LATENT_TOK = DEC_BATCH * DEC_SEQ
N_TOK = N_PROMPT_TOK + N_LATENT_TOK
MOD_GROUP_ROWS = 4096
N_COND = 1 + DEC_BATCH
COND_ROWS = 8

LANES = 128
HALF_LANES = LANES // 2
TM = 512
N_PROMPT_TILES = N_PROMPT_TOK // TM
FF_CHUNK = 256
POOL_TM = 256
POOL_HALO = 8
D_Q = N_HEADS * HEAD_DIM
D_KV = N_KV_HEADS * HEAD_DIM
D_KV_AB = N_KV_HEADS * 2 * LANES
VMEM_LIMIT = 56 * 1024 * 1024

F32 = jnp.float32
BF16 = jnp.bfloat16


def _layer_norm(y, g, b):
    mu = jnp.mean(y, axis=-1, keepdims=True)
    yc = y - mu
    var = jnp.mean(yc * yc, axis=-1, keepdims=True)
    return yc * lax.rsqrt(var + LN_EPS) * g + b


def _modulate(x, mod_ref, row0):
    shift = mod_ref[0, row0:row0 + 1, :]
    scale = mod_ref[0, row0 + 1:row0 + 2, :]
    return x * (1.0 + scale) + shift


def _params(n_axes=1):
    return pltpu.CompilerParams(
        dimension_semantics=("arbitrary",) * n_axes,
        vmem_limit_bytes=VMEM_LIMIT)


def _resident(shape):
    nd = len(shape)
    return pl.BlockSpec(shape, lambda *_: (0,) * nd, pipeline_mode=pl.Buffered(1))


def _mod_spec(tile_rows):
    tiles_per_group = MOD_GROUP_ROWS // tile_rows
    return pl.BlockSpec((1, N_MOD, D_MODEL), lambda i: (i // tiles_per_group, 0, 0))


def _slab_spec(cols):
    return pl.BlockSpec((TM, cols), lambda i: (i, 0))


def _prompt_spec(cols):
    return pl.BlockSpec((TM, cols), lambda i: (jnp.minimum(i, N_PROMPT_TILES - 1), 0))


def _latent_spec(cols):
    return pl.BlockSpec((TM, cols), lambda i: (jnp.maximum(i - N_PROMPT_TILES, 0), 0))


def _read_split(p_ref, l_ref):
    return jnp.where(pl.program_id(0) < N_PROMPT_TILES, p_ref[...], l_ref[...])


ADALN_NC = 1024


def _adaln_kernel(cond_ref, w_ref, b_ref, o_ref):
    c = cond_ref[...]
    s = (c * (1.0 / (1.0 + jnp.exp(-c)))).astype(BF16)
    o_ref[0] = jnp.dot(s, w_ref[0].astype(BF16), preferred_element_type=F32) + b_ref[0]


def _adaln(cond, w_mod, b_mod):
    n_out = N_MOD * D_MODEL
    out = pl.pallas_call(
        _adaln_kernel,
        grid=(DEPTH, n_out // ADALN_NC),
        in_specs=[
            pl.BlockSpec((COND_ROWS, D_MODEL), lambda l, n: (0, 0)),
            pl.BlockSpec((1, D_MODEL, ADALN_NC), lambda l, n: (l, 0, n)),
            pl.BlockSpec((1, 1, ADALN_NC), lambda l, n: (l, 0, n)),
        ],
        out_specs=pl.BlockSpec((1, COND_ROWS, ADALN_NC), lambda l, n: (l, 0, n)),
        out_shape=jax.ShapeDtypeStruct((DEPTH, COND_ROWS, n_out), F32),
        compiler_params=_params(2),
        name="adaln",
    )(cond, w_mod, b_mod.reshape(DEPTH, 1, n_out))
    return out[:, :N_COND].reshape(DEPTH, N_COND, N_MOD, D_MODEL)


def _ffn_body(x, mod_ref, lng_ref, lnb_ref, wg_ref, wu_ref, wd_ref, a_ref, row0):
    h = _modulate(x, mod_ref, row0).astype(BF16)
    for c in range(D_FF // FF_CHUNK):
        sl = slice(c * FF_CHUNK, (c + 1) * FF_CHUNK)
        g = jnp.dot(h, wg_ref[:, sl], preferred_element_type=F32)
        u = jnp.dot(h, wu_ref[:, sl], preferred_element_type=F32)
        a_ref[:, sl] = (g * (1.0 / (1.0 + jnp.exp(-g))) * u).astype(BF16)
    f = jnp.dot(a_ref[...], wd_ref[...], preferred_element_type=F32)
    gate = mod_ref[0, row0 + 2:row0 + 3, :]
    y = DEEPNORM_ALPHA * x + (0.5 * gate) * f
    return _layer_norm(y, lng_ref[...], lnb_ref[...])


def _ffn_kernel(*refs, row0, split_in, split_out):
    n_x = 2 if split_in else 1
    x_refs, (mod_ref, lng_ref, lnb_ref, wg_ref, wu_ref, wd_ref) = refs[:n_x], refs[n_x:n_x + 6]
    o_refs, a_ref = refs[n_x + 6:-1], refs[-1]
    x = _read_split(*x_refs) if split_in else x_refs[0][...]
    out = _ffn_body(x, mod_ref, lng_ref, lnb_ref, wg_ref, wu_ref, wd_ref, a_ref, row0)
    if split_out:
        is_prompt = pl.program_id(0) < N_PROMPT_TILES

        @pl.when(is_prompt)
        def _():
            o_refs[0][...] = out

        @pl.when(jnp.logical_not(is_prompt))
        def _():
            o_refs[1][...] = out
    else:
        o_refs[0][...] = out


def _ffn(xs, mods, row0, ln_g, ln_b, wg, wu, wd, split_out=False):
    split_in = len(xs) == 2
    x_specs = [_prompt_spec(D_MODEL), _latent_spec(D_MODEL)] if split_in else [_slab_spec(D_MODEL)]
    if split_out:
        out_specs = [_prompt_spec(D_MODEL), _latent_spec(D_MODEL)]
        out_shape = [jax.ShapeDtypeStruct((N_PROMPT_TOK, D_MODEL), F32),
                     jax.ShapeDtypeStruct((N_LATENT_TOK, D_MODEL), F32)]
    else:
        out_specs = [_slab_spec(D_MODEL)]
        out_shape = [jax.ShapeDtypeStruct((N_TOK, D_MODEL), F32)]
    outs = pl.pallas_call(
        functools.partial(_ffn_kernel, row0=row0, split_in=split_in, split_out=split_out),
        grid=(N_TOK // TM,),
        in_specs=x_specs + [
            _mod_spec(TM),
            _resident((1, D_MODEL)),
            _resident((1, D_MODEL)),
            _resident((D_MODEL, D_FF)),
            _resident((D_MODEL, D_FF)),
            _resident((D_FF, D_MODEL)),
        ],
        out_specs=out_specs,
        out_shape=out_shape,
        scratch_shapes=[pltpu.VMEM((TM, D_FF), BF16)],
        compiler_params=_params(),
        name="ffn",
    )(*xs, mods, ln_g.reshape(1, D_MODEL), ln_b.reshape(1, D_MODEL), wg, wu, wd)
    return outs if split_out else outs[0]


ROPE_TILES = DEC_SEQ // TM


def _write_ab(dst_ref, pair, src):
    lo = lax.broadcasted_iota(jnp.int32, src.shape, 1) < HALF_LANES
    swapped = pltpu.roll(src, HALF_LANES, 1)
    blocks = (jnp.where(lo, src, 0.0), jnp.where(lo, 0.0, swapped),
              jnp.where(lo, swapped, 0.0), jnp.where(lo, 0.0, src))
    for n, blk in enumerate(blocks):
        c0 = (4 * pair + n) * LANES
        dst_ref[:, c0:c0 + LANES] = blk.astype(BF16)


def _qkv_kernel(x_ref, mod_ref, w_ref, cos_ref, sup_ref, sdn_ref,
                q_ref, kab_ref, vab_ref, ks_ref, vs_ref):
    h = _modulate(x_ref[...], mod_ref, 3).astype(BF16)
    qkv = jnp.dot(h, w_ref[...], preferred_element_type=F32)
    cos = cos_ref[...]
    s_up = sup_ref[...]
    s_dn = sdn_ref[...]

    def rope(blk):
        up = pltpu.roll(blk, LANES - 16, 1)
        dn = pltpu.roll(blk, 16, 1)
        return blk * cos + up * s_up + dn * s_dn

    for j in range(D_Q // LANES):
        cols = slice(j * LANES, (j + 1) * LANES)
        q_ref[:, cols] = (rope(qkv[:, cols]) * ATTN_SCALE).astype(BF16)
    for pair in range(D_KV // LANES):
        k_cols = slice(D_Q + pair * LANES, D_Q + (pair + 1) * LANES)
        v_cols = slice(D_Q + D_KV + pair * LANES, D_Q + D_KV + (pair + 1) * LANES)
        _write_ab(kab_ref, pair, rope(qkv[:, k_cols]))
        _write_ab(vab_ref, pair, qkv[:, v_cols])

    @pl.when(pl.program_id(0) < N_PROMPT_TILES)
    def _():
        ks_ref[...] = qkv[:, D_Q:D_Q + D_KV]
        vs_ref[...] = qkv[:, D_Q + D_KV:]


def _rope_index(i):
    return (jnp.where(i < N_PROMPT_TILES, ROPE_TILES, (i - N_PROMPT_TILES) % ROPE_TILES), 0)


def _qkv(x, mods, w, cos_t, sup_t, sdn_t):
    return pl.pallas_call(
        _qkv_kernel,
        grid=(N_TOK // TM,),
        in_specs=[
            _slab_spec(D_MODEL),
            _mod_spec(TM),
            _resident((D_MODEL, D_Q + 2 * D_KV)),
            pl.BlockSpec((TM, LANES), _rope_index),
            pl.BlockSpec((TM, LANES), _rope_index),
            pl.BlockSpec((TM, LANES), _rope_index),
        ],
        out_specs=[
            _slab_spec(D_Q), _slab_spec(D_KV_AB), _slab_spec(D_KV_AB),
            _prompt_spec(D_KV), _prompt_spec(D_KV),
        ],
        out_shape=[
            jax.ShapeDtypeStruct((N_TOK, D_Q), BF16),
            jax.ShapeDtypeStruct((N_TOK, D_KV_AB), BF16),
            jax.ShapeDtypeStruct((N_TOK, D_KV_AB), BF16),
            jax.ShapeDtypeStruct((N_PROMPT_TOK, D_KV), F32),
            jax.ShapeDtypeStruct((N_PROMPT_TOK, D_KV), F32),
        ],
        compiler_params=_params(),
        name="qkv",
    )(x, mods, w, cos_t, sup_t, sdn_t)


def _attend(q_ref, o_ref, sink_ref, k_slabs, v_slabs, masks, tq):
    nt = (((1,), (1,)), ((), ()))
    row_hi = lax.broadcasted_iota(jnp.int32, (2 * tq, 1), 0) >= tq
    lane_lo = lax.broadcasted_iota(jnp.int32, (2 * tq, LANES), 1) < HALF_LANES
    for kh in range(N_KV_HEADS):
        ks = k_slabs(kh)
        vs = v_slabs(kh)
        k_cat = jnp.concatenate([s[:, :LANES] for s in ks] + [s[:, LANES:] for s in ks], axis=0)
        v_cat = jnp.concatenate([s[:, :LANES] for s in vs] + [s[:, LANES:] for s in vs], axis=0)
        n_keys = k_cat.shape[0] // 2
        j0 = 2 * kh
        q2 = jnp.concatenate([q_ref[:, j0 * LANES:(j0 + 1) * LANES],
                              q_ref[:, (j0 + 1) * LANES:(j0 + 2) * LANES]], axis=0)
        s = lax.dot_general(q2, k_cat, nt, preferred_element_type=F32)
        es, inv_dens = [], []
        for half in range(2):
            segs, off = [], half * n_keys
            for slab, m in zip(ks, masks):
                seg = s[:, off:off + slab.shape[0]]
                segs.append(seg if m is None else jnp.where(m, seg, NEG_INF))
                off += slab.shape[0]
            sink = jnp.where(row_hi, sink_ref[4 * kh + 2 + half], sink_ref[4 * kh + half])
            m_row = sink
            for seg in segs:
                m_row = jnp.maximum(m_row, jnp.max(seg, axis=-1, keepdims=True))
            den = jnp.exp(sink - m_row)
            for seg in segs:
                e = jnp.exp(seg - m_row)
                den = den + jnp.sum(e, axis=-1, keepdims=True)
                es.append(e.astype(BF16))
            inv_dens.append(1.0 / den)
        p = jnp.concatenate(es, axis=1)
        o2 = jnp.dot(p, v_cat, preferred_element_type=F32)
        o2 = o2 * jnp.where(lane_lo, inv_dens[0], inv_dens[1])
        o_ref[:, j0 * LANES:(j0 + 1) * LANES] = o2[:tq].astype(BF16)
        o_ref[:, (j0 + 1) * LANES:(j0 + 2) * LANES] = o2[tq:].astype(BF16)


def _kv_cols(kh):
    return slice(kh * 2 * LANES, (kh + 1) * 2 * LANES)


def _ctx_attn_kernel(sink_ref, q_ref, k_ref, v_ref, o_ref):
    _attend(q_ref, o_ref, sink_ref,
            lambda kh: [k_ref[:, _kv_cols(kh)]],
            lambda kh: [v_ref[:, _kv_cols(kh)]],
            [None], SEQ)


def _ctx_attn(sink, q, kab, vab):
    return pl.pallas_call(
        _ctx_attn_kernel,
        grid=(BATCH,),
        in_specs=[
            pl.BlockSpec(memory_space=pltpu.SMEM),
            pl.BlockSpec((SEQ, D_Q), lambda b: (b, 0)),
            pl.BlockSpec((SEQ, D_KV_AB), lambda b: (b, 0)),
            pl.BlockSpec((SEQ, D_KV_AB), lambda b: (b, 0)),
        ],
        out_specs=pl.BlockSpec((SEQ, D_Q), lambda b: (b, 0)),
        out_shape=jax.ShapeDtypeStruct((N_PROMPT_TOK, D_Q), BF16),
        compiler_params=_params(),
        name="ctx_attn",
    )(sink, q, kab, vab)


N_QBLK = DEC_SEQ // BLOCK


def _lat_attn_kernel(sink_ref, q_ref, kp_ref, kc_ref, kn_ref, vp_ref, vc_ref, vn_ref,
                     kx_ref, vx_ref, o_ref):
    qb = pl.program_id(1)
    r = lax.broadcasted_iota(jnp.int32, (2 * BLOCK, BLOCK), 0) & (BLOCK - 1)
    c = lax.broadcasted_iota(jnp.int32, (2 * BLOCK, BLOCK), 1)
    m_prev = c >= r + jnp.where(qb > 0, 0, BLOCK)
    m_next = c <= r - jnp.where(qb < N_QBLK - 1, 0, BLOCK)

    def slabs(p_ref, c_ref, n_ref, x_ref):
        return lambda kh: [p_ref[:, _kv_cols(kh)], c_ref[:, _kv_cols(kh)],
                           n_ref[:, _kv_cols(kh)], x_ref[0, :, _kv_cols(kh)]]

    _attend(q_ref, o_ref, sink_ref,
            slabs(kp_ref, kc_ref, kn_ref, kx_ref), slabs(vp_ref, vc_ref, vn_ref, vx_ref),
            [m_prev, None, m_next, None], BLOCK)


def _lat_attn(sink, q, kab, vab, kab_ctx, vab_ctx):
    first = N_PROMPT_TOK // BLOCK

    def cur(b, i):
        return (first + b * N_QBLK + i, 0)

    def prev(b, i):
        return (first + b * N_QBLK + jnp.maximum(i - 1, 0), 0)

    def nxt(b, i):
        return (first + b * N_QBLK + jnp.minimum(i + 1, N_QBLK - 1), 0)

    kv_spec = lambda f: pl.BlockSpec((BLOCK, D_KV_AB), f)
    ctx_spec = pl.BlockSpec((1, PAST_LEN, D_KV_AB), lambda b, i: (b, 0, 0))
    return pl.pallas_call(
        _lat_attn_kernel,
        grid=(DEC_BATCH, N_QBLK),
        in_specs=[
            pl.BlockSpec(memory_space=pltpu.SMEM),
            pl.BlockSpec((BLOCK, D_Q), cur),
            kv_spec(prev), kv_spec(cur), kv_spec(nxt),
            kv_spec(prev), kv_spec(cur), kv_spec(nxt),
            ctx_spec, ctx_spec,
        ],
        out_specs=pl.BlockSpec((BLOCK, D_Q), lambda b, i: (b * N_QBLK + i, 0)),
        out_shape=jax.ShapeDtypeStruct((N_LATENT_TOK, D_Q), BF16),
        compiler_params=_params(2),
        name="lat_attn",
    )(sink, q, kab, kab, kab, vab, vab, vab, kab_ctx, vab_ctx)


def _proj_kernel(x_ref, ap_ref, al_ref, mod_ref, lng_ref, lnb_ref, wo_ref, o_ref):
    f = jnp.dot(_read_split(ap_ref, al_ref), wo_ref[...], preferred_element_type=F32)
    gate = mod_ref[0, 5:6, :]
    y = DEEPNORM_ALPHA * x_ref[...] + gate * f
    o_ref[...] = _layer_norm(y, lng_ref[...], lnb_ref[...])


def _proj(x, attn_prompt, attn_latent, mods, ln_g, ln_b, w_o):
    return pl.pallas_call(
        _proj_kernel,
        grid=(N_TOK // TM,),
        in_specs=[
            _slab_spec(D_MODEL),
            _prompt_spec(D_Q),
            _latent_spec(D_Q),
            _mod_spec(TM),
            _resident((1, D_MODEL)),
            _resident((1, D_MODEL)),
            _resident((D_Q, D_MODEL)),
        ],
        out_specs=_slab_spec(D_MODEL),
        out_shape=jax.ShapeDtypeStruct((N_TOK, D_MODEL), F32),
        compiler_params=_params(),
        name="attn_proj",
    )(x, attn_prompt, attn_latent, mods, ln_g.reshape(1, D_MODEL), ln_b.reshape(1, D_MODEL), w_o)


PROMPT_POOL_TILES = N_PROMPT_TOK // POOL_TM
LATENT_POOL_TILES = DEC_SEQ // POOL_TM


def _pool_kernel(x_ref, xp_ref, xn_ref, mod_ref, lng_ref, lnb_ref, w_ref, sc_ref, o_ref, h_ref):
    i = pl.program_id(0)
    in_seq = (i - PROMPT_POOL_TILES) % LATENT_POOL_TILES
    is_start = (i < PROMPT_POOL_TILES) | (in_seq == 0)
    is_end = (i < PROMPT_POOL_TILES) | (in_seq == LATENT_POOL_TILES - 1)

    x = x_ref[...]
    h = _modulate(x, mod_ref, 3)
    h_ref[POOL_HALO:POOL_HALO + POOL_TM, :] = h
    h_ref[0:POOL_HALO, :] = jnp.where(is_start, 0.0, _modulate(xp_ref[...], mod_ref, 3))
    h_ref[POOL_HALO + POOL_TM:, :] = jnp.where(is_end, 0.0, _modulate(xn_ref[...], mod_ref, 3))

    r = lax.broadcasted_iota(jnp.int32, (POOL_TM, POOL_GROUP_DIM), 0)
    gate = mod_ref[0, 5:6, :]
    outs = []
    for gi, w in enumerate(POOL_WINDOWS):
        half = w // 2
        cols = slice(gi * POOL_GROUP_DIM, (gi + 1) * POOL_GROUP_DIM)
        total = h_ref[POOL_HALO - half:POOL_HALO - half + POOL_TM, cols]
        for d in range(-half + 1, half):
            total = total + h_ref[POOL_HALO + d:POOL_HALO + d + POOL_TM, cols]
        clipped_lo = jnp.where(is_start, jnp.maximum(half - r, 0), 0)
        clipped_hi = jnp.where(is_end, jnp.maximum(r + half - POOL_TM, 0), 0)
        cnt = (w - clipped_lo - clipped_hi).astype(F32)
        pooled = (total / cnt - h[:, cols]).astype(BF16)
        outs.append(jnp.dot(pooled, w_ref[gi], preferred_element_type=F32))
    mixed = jnp.concatenate(outs, axis=-1) * sc_ref[...]
    y = DEEPNORM_ALPHA * x + gate * mixed
    o_ref[...] = _layer_norm(y, lng_ref[...], lnb_ref[...])


def _pool(x, mods, ln_g, ln_b, w_pool, scale):
    halo_per_tile = POOL_TM // POOL_HALO
    last_halo = N_TOK // POOL_HALO - 1
    return pl.pallas_call(
        _pool_kernel,
        grid=(N_TOK // POOL_TM,),
        in_specs=[
            pl.BlockSpec((POOL_TM, D_MODEL), lambda i: (i, 0)),
            pl.BlockSpec((POOL_HALO, D_MODEL), lambda i: (jnp.maximum(i * halo_per_tile - 1, 0), 0)),
            pl.BlockSpec((POOL_HALO, D_MODEL),
                         lambda i: (jnp.minimum((i + 1) * halo_per_tile, last_halo), 0)),
            _mod_spec(POOL_TM),
            _resident((1, D_MODEL)),
            _resident((1, D_MODEL)),
            _resident((len(POOL_WINDOWS), POOL_GROUP_DIM, POOL_GROUP_DIM)),
            _resident((1, D_MODEL)),
        ],
        out_specs=pl.BlockSpec((POOL_TM, D_MODEL), lambda i: (i, 0)),
        out_shape=jax.ShapeDtypeStruct((N_TOK, D_MODEL), F32),
        scratch_shapes=[pltpu.VMEM((POOL_TM + 2 * POOL_HALO, D_MODEL), F32)],
        compiler_params=_params(),
        name="pool",
    )(x, x, x, mods, ln_g.reshape(1, D_MODEL), ln_b.reshape(1, D_MODEL), w_pool,
      scale.reshape(1, D_MODEL))


def _ab_layout(a):
    lead = a.shape[:-1]
    a = a.reshape(lead + (N_KV_HEADS, 1, HEAD_DIM)).astype(BF16)
    z = jnp.zeros_like(a)
    return jnp.concatenate([a, z, z, a], axis=-2).reshape(lead + (D_KV_AB,))


def _rope_tables():
    n_rows = DEC_SEQ // GRID_W
    rows = jnp.repeat(jnp.arange(n_rows, dtype=F32), GRID_W)
    cols = jnp.tile(jnp.arange(GRID_W, dtype=F32), n_rows)
    inv = jnp.power(ROPE_BASE, -jnp.arange(N_FREQ, dtype=F32) / N_FREQ)
    ang_r = rows[:, None] * inv
    ang_c = cols[:, None] * inv
    ang = jnp.concatenate([ang_r, ang_r, ang_c, ang_c], axis=-1)
    cos = jnp.tile(jnp.cos(ang), (1, LANES // HEAD_DIM))
    sin = jnp.tile(jnp.sin(ang), (1, LANES // HEAD_DIM))
    first_half = (jnp.arange(LANES) % 32) < 16
    s_up = jnp.where(first_half, -sin, 0.0)
    s_dn = jnp.where(first_half, 0.0, sin)
    ident = jnp.ones((TM, LANES), F32)
    zeros = jnp.zeros((TM, LANES), F32)
    return (jnp.concatenate([cos, ident], axis=0),
            jnp.concatenate([s_up, zeros], axis=0),
            jnp.concatenate([s_dn, zeros], axis=0))


def kernel(x_prompt, x_sample, cache_k, cache_v, c, c_ctx, w_mod, b_mod, ln_g, ln_b,
           ffn_w_gate, ffn_w_up, ffn_w_down, attn_w_qkv, attn_w_o, attn_sink,
           pool_w, pool_scale):
    cond = jnp.concatenate(
        [c_ctx[None, :], c, jnp.zeros((COND_ROWS - N_COND, D_MODEL), F32)], axis=0)
    mods = _adaln(cond, w_mod, b_mod)

    wg = ffn_w_gate.astype(BF16)
    wu = ffn_w_up.astype(BF16)
    wd = ffn_w_down.astype(BF16)

    x = _ffn((x_prompt.reshape(N_PROMPT_TOK, D_MODEL), x_sample.reshape(N_LATENT_TOK, D_MODEL)),
             mods[0], 0, ln_g[0, 0], ln_b[0, 0], wg[0, 0], wu[0, 0], wd[0, 0])
    cos_t, sup_t, sdn_t = _rope_tables()
    q, kab, vab, k_state, v_state = _qkv(x, mods[0], attn_w_qkv[0].astype(BF16), cos_t, sup_t, sdn_t)
    sink = attn_sink[0]
    o_ctx = _ctx_attn(sink, q, kab, vab)
    kab_ctx = _ab_layout(cache_k[:, 0].reshape(DEC_BATCH, PAST_LEN, D_KV))
    vab_ctx = _ab_layout(cache_v[:, 0].reshape(DEC_BATCH, PAST_LEN, D_KV))
    o_lat = _lat_attn(sink, q, kab, vab, kab_ctx, vab_ctx)
    x = _proj(x, o_ctx, o_lat, mods[0], ln_g[0, 1], ln_b[0, 1], attn_w_o[0].astype(BF16))
    x = _ffn((x,), mods[0], 6, ln_g[0, 2], ln_b[0, 2], wg[0, 1], wu[0, 1], wd[0, 1])

    x = _ffn((x,), mods[1], 0, ln_g[1, 0], ln_b[1, 0], wg[1, 0], wu[1, 0], wd[1, 0])
    x = _pool(x, mods[1], ln_g[1, 1], ln_b[1, 1], pool_w[0].astype(BF16), pool_scale[0])
    y_prompt, y_sample = _ffn((x,), mods[1], 6, ln_g[1, 2], ln_b[1, 2],
                              wg[1, 1], wu[1, 1], wd[1, 1], split_out=True)

    return (y_prompt.reshape(BATCH, SEQ, D_MODEL),
            y_sample.reshape(DEC_BATCH, DEC_SEQ, D_MODEL),
            k_state.reshape(BATCH, 1, SEQ, N_KV_HEADS, HEAD_DIM),
            v_state.reshape(BATCH, 1, SEQ, N_KV_HEADS, HEAD_DIM))
```

```python
import functools

import jax
import jax.numpy as jnp
from jax import lax
from jax.experimental import pallas as pl
from jax.experimental.pallas import tpu as pltpu

D_MODEL = 1024
BATCH = 16
SEQ = 256
DEPTH = 2
DEC_BATCH = 2
DEC_SEQ = 4096
PAST_LEN = 256
GRID_W = 64
N_HEADS = 16
N_KV_HEADS = 4
HEAD_DIM = 64
WINDOW = 128
BLOCK = 128
ROPE_BASE = 10000.0
N_FREQ = HEAD_DIM // 4
POOL_WINDOWS = (2, 4, 8, 16)
POOL_GROUP_DIM = D_MODEL // 4
D_FF = 2816
N_MOD = 9
LN_EPS = 1e-5
DEEPNORM_ALPHA = (2.0 * DEPTH) ** 0.25
ATTN_SCALE = HEAD_DIM ** -0.5
NEG_INF = -1e30

N_PROMPT_TOK = BATCH * SEQ
N_LATENT_TOK = DEC_BATCH * DEC_SEQ
N_TOK = N_PROMPT_TOK + N_LATENT_TOK
MOD_GROUP_ROWS = 4096
N_COND = 1 + DEC_BATCH
COND_ROWS = 8

LANES = 128
HALF_LANES = LANES // 2
TM = 512
N_PROMPT_TILES = N_PROMPT_TOK // TM
FF_CHUNK = 256
POOL_TM = 256
POOL_HALO = 8
D_Q = N_HEADS * HEAD_DIM
D_KV = N_KV_HEADS * HEAD_DIM
D_KV_AB = N_KV_HEADS * 2 * LANES
VMEM_LIMIT = 56 * 1024 * 1024

F32 = jnp.float32
BF16 = jnp.bfloat16


def _layer_norm(y, g, b):
    mu = jnp.mean(y, axis=-1, keepdims=True)
    yc = y - mu
    var = jnp.mean(yc * yc, axis=-1, keepdims=True)
    return yc * lax.rsqrt(var + LN_EPS) * g + b


def _modulate(x, mod_ref, row0):
    shift = mod_ref[row0:row0 + 1, :]
    scale = mod_ref[row0 + 1:row0 + 2, :]
    return x * (1.0 + scale) + shift


def _params(n_axes=1):
    return pltpu.CompilerParams(
        dimension_semantics=("arbitrary",) * n_axes,
        vmem_limit_bytes=VMEM_LIMIT)


def _resident(shape):
    nd = len(shape)
    return pl.BlockSpec(shape, lambda *_: (0,) * nd, pipeline_mode=pl.Buffered(1))


def _resident_at(index, tail):
    return pl.BlockSpec((None,) * len(index) + tuple(tail),
                        lambda *_: tuple(index) + (0,) * len(tail), pipeline_mode=pl.Buffered(1))


def _mod_spec(tile_rows, layer):
    tiles_per_group = MOD_GROUP_ROWS // tile_rows
    return pl.BlockSpec((None, None, N_MOD, D_MODEL),
                        lambda i: (layer, i // tiles_per_group, 0, 0))


def _ln_specs(layer, which):
    return [_resident_at((layer * 3 + which,), (1, D_MODEL))] * 2


def _slab_spec(cols):
    return pl.BlockSpec((TM, cols), lambda i: (i, 0))


def _prompt_spec(cols):
    return pl.BlockSpec((TM, cols), lambda i: (jnp.minimum(i, N_PROMPT_TILES - 1), 0))


def _latent_spec(cols):
    return pl.BlockSpec((TM, cols), lambda i: (jnp.maximum(i - N_PROMPT_TILES, 0), 0))


def _read_split(p_ref, l_ref):
    return jnp.where(pl.program_id(0) < N_PROMPT_TILES, p_ref[...], l_ref[...])


ADALN_NC = 1024


def _adaln_kernel(cond_ref, w_ref, b_ref, o_ref):
    c = cond_ref[...]
    s = (c * (1.0 / (1.0 + jnp.exp(-c)))).astype(BF16)
    o_ref[0] = jnp.dot(s, w_ref[0].astype(BF16), preferred_element_type=F32) + b_ref[0]


def _adaln(cond, w_mod, b_mod):
    n_out = N_MOD * D_MODEL
    out = pl.pallas_call(
        _adaln_kernel,
        grid=(DEPTH, n_out // ADALN_NC),
        in_specs=[
            pl.BlockSpec((COND_ROWS, D_MODEL), lambda l, n: (0, 0)),
            pl.BlockSpec((1, D_MODEL, ADALN_NC), lambda l, n: (l, 0, n)),
            pl.BlockSpec((1, 1, ADALN_NC), lambda l, n: (l, 0, n)),
        ],
        out_specs=pl.BlockSpec((1, COND_ROWS, ADALN_NC), lambda l, n: (l, 0, n)),
        out_shape=jax.ShapeDtypeStruct((DEPTH, COND_ROWS, n_out), F32),
        compiler_params=_params(2),
        name="adaln",
    )(cond, w_mod, b_mod.reshape(DEPTH, 1, n_out))
    return out[:, :N_COND].reshape(DEPTH, N_COND, N_MOD, D_MODEL)


def _ffn_body(x, mod_ref, lng_ref, lnb_ref, wg_ref, wu_ref, wd_ref, a_ref, row0):
    h = _modulate(x, mod_ref, row0).astype(BF16)
    for c in range(D_FF // FF_CHUNK):
        sl = slice(c * FF_CHUNK, (c + 1) * FF_CHUNK)
        g = jnp.dot(h, wg_ref[:, sl], preferred_element_type=F32)
        u = jnp.dot(h, wu_ref[:, sl], preferred_element_type=F32)
        a_ref[:, sl] = (g * (1.0 / (1.0 + jnp.exp(-g))) * u).astype(BF16)
    f = jnp.dot(a_ref[...], wd_ref[...], preferred_element_type=F32)
    gate = mod_ref[row0 + 2:row0 + 3, :]
    y = DEEPNORM_ALPHA * x + (0.5 * gate) * f
    return _layer_norm(y, lng_ref[...], lnb_ref[...])


def _ffn_kernel(*refs, row0, split_in, split_out):
    n_x = 2 if split_in else 1
    x_refs, (mod_ref, lng_ref, lnb_ref, wg_ref, wu_ref, wd_ref) = refs[:n_x], refs[n_x:n_x + 6]
    o_refs, a_ref = refs[n_x + 6:-1], refs[-1]
    x = _read_split(*x_refs) if split_in else x_refs[0][...]
    out = _ffn_body(x, mod_ref, lng_ref, lnb_ref, wg_ref, wu_ref, wd_ref, a_ref, row0)
    if split_out:
        is_prompt = pl.program_id(0) < N_PROMPT_TILES

        @pl.when(is_prompt)
        def _():
            o_refs[0][...] = out

        @pl.when(jnp.logical_not(is_prompt))
        def _():
            o_refs[1][...] = out
    else:
        o_refs[0][...] = out


def _ffn(xs, mods, ln_g, ln_b, wg, wu, wd, layer, which, split_out=False):
    split_in = len(xs) == 2
    row0 = 6 * which
    ln_row = 2 * which
    x_specs = [_prompt_spec(D_MODEL), _latent_spec(D_MODEL)] if split_in else [_slab_spec(D_MODEL)]
    if split_out:
        out_specs = [_prompt_spec(D_MODEL), _latent_spec(D_MODEL)]
        out_shape = [jax.ShapeDtypeStruct((N_PROMPT_TOK, D_MODEL), F32),
                     jax.ShapeDtypeStruct((N_LATENT_TOK, D_MODEL), F32)]
    else:
        out_specs = [_slab_spec(D_MODEL)]
        out_shape = [jax.ShapeDtypeStruct((N_TOK, D_MODEL), F32)]
    outs = pl.pallas_call(
        functools.partial(_ffn_kernel, row0=row0, split_in=split_in, split_out=split_out),
        grid=(N_TOK // TM,),
        in_specs=x_specs + [_mod_spec(TM, layer)] + _ln_specs(layer, ln_row) + [
            _resident_at((layer, which), (D_MODEL, D_FF)),
            _resident_at((layer, which), (D_MODEL, D_FF)),
            _resident_at((layer, which), (D_FF, D_MODEL)),
        ],
        out_specs=out_specs,
        out_shape=out_shape,
        scratch_shapes=[pltpu.VMEM((TM, D_FF), BF16)],
        compiler_params=_params(),
        name="ffn",
    )(*xs, mods, ln_g, ln_b, wg, wu, wd)
    return outs if split_out else outs[0]


ROPE_TILES = DEC_SEQ // TM


def _write_ab(dst_ref, pair, src):
    lo = lax.broadcasted_iota(jnp.int32, src.shape, 1) < HALF_LANES
    swapped = pltpu.roll(src, HALF_LANES, 1)
    blocks = (jnp.where(lo, src, 0.0), jnp.where(lo, 0.0, swapped),
              jnp.where(lo, swapped, 0.0), jnp.where(lo, 0.0, src))
    for n, blk in enumerate(blocks):
        c0 = (4 * pair + n) * LANES
        dst_ref[:, c0:c0 + LANES] = blk.astype(BF16)


def _qkv_kernel(x_ref, mod_ref, w_ref, cos_ref, sup_ref, sdn_ref,
                q_ref, kab_ref, vab_ref, ks_ref, vs_ref):
    h = _modulate(x_ref[...], mod_ref, 3).astype(BF16)
    qkv = jnp.dot(h, w_ref[...], preferred_element_type=F32)
    cos = cos_ref[...]
    s_up = sup_ref[...]
    s_dn = sdn_ref[...]

    def rope(blk):
        up = pltpu.roll(blk, LANES - 16, 1)
        dn = pltpu.roll(blk, 16, 1)
        return blk * cos + up * s_up + dn * s_dn

    for j in range(D_Q // LANES):
        cols = slice(j * LANES, (j + 1) * LANES)
        q_ref[:, cols] = (rope(qkv[:, cols]) * ATTN_SCALE).astype(BF16)
    for pair in range(D_KV // LANES):
        k_cols = slice(D_Q + pair * LANES, D_Q + (pair + 1) * LANES)
        v_cols = slice(D_Q + D_KV + pair * LANES, D_Q + D_KV + (pair + 1) * LANES)
        _write_ab(kab_ref, pair, rope(qkv[:, k_cols]))
        _write_ab(vab_ref, pair, qkv[:, v_cols])

    @pl.when(pl.program_id(0) < N_PROMPT_TILES)
    def _():
        k_t = qkv[:, D_Q:D_Q + D_KV].T
        v_t = qkv[:, D_Q + D_KV:].T
        for s in range(TM // SEQ):
            ks_ref[s] = k_t[:, s * SEQ:(s + 1) * SEQ]
            vs_ref[s] = v_t[:, s * SEQ:(s + 1) * SEQ]


def _rope_index(i):
    return (jnp.where(i < N_PROMPT_TILES, ROPE_TILES, (i - N_PROMPT_TILES) % ROPE_TILES), 0)


def _qkv(x, mods, w, cos_t, sup_t, sdn_t):
    seqs_per_tile = TM // SEQ
    state_spec = pl.BlockSpec((seqs_per_tile, D_KV, SEQ),
                              lambda i: (jnp.minimum(i, N_PROMPT_TILES - 1), 0, 0))
    return pl.pallas_call(
        _qkv_kernel,
        grid=(N_TOK // TM,),
        in_specs=[
            _slab_spec(D_MODEL),
            _mod_spec(TM, 0),
            _resident((D_MODEL, D_Q + 2 * D_KV)),
            pl.BlockSpec((TM, LANES), _rope_index),
            pl.BlockSpec((TM, LANES), _rope_index),
            pl.BlockSpec((TM, LANES), _rope_index),
        ],
        out_specs=[
            _slab_spec(D_Q), _slab_spec(D_KV_AB), _slab_spec(D_KV_AB), state_spec, state_spec,
        ],
        out_shape=[
            jax.ShapeDtypeStruct((N_TOK, D_Q), BF16),
            jax.ShapeDtypeStruct((N_TOK, D_KV_AB), BF16),
            jax.ShapeDtypeStruct((N_TOK, D_KV_AB), BF16),
            jax.ShapeDtypeStruct((BATCH, D_KV, SEQ), F32),
            jax.ShapeDtypeStruct((BATCH, D_KV, SEQ), F32),
        ],
        compiler_params=_params(),
        name="qkv",
    )(x, mods, w, cos_t, sup_t, sdn_t)


def _attend(q_ref, o_ref, sink_ref, k_slabs, v_slabs, masks, tq):
    nt = (((1,), (1,)), ((), ()))
    row_hi = lax.broadcasted_iota(jnp.int32, (2 * tq, 1), 0) >= tq
    lane_lo = lax.broadcasted_iota(jnp.int32, (2 * tq, LANES), 1) < HALF_LANES
    for kh in range(N_KV_HEADS):
        ks = k_slabs(kh)
        vs = v_slabs(kh)
        k_cat = jnp.concatenate([s[:, :LANES] for s in ks] + [s[:, LANES:] for s in ks], axis=0)
        v_cat = jnp.concatenate([s[:, :LANES] for s in vs] + [s[:, LANES:] for s in vs], axis=0)
        n_keys = k_cat.shape[0] // 2
        j0 = 2 * kh
        q2 = jnp.concatenate([q_ref[:, j0 * LANES:(j0 + 1) * LANES],
                              q_ref[:, (j0 + 1) * LANES:(j0 + 2) * LANES]], axis=0)
        s = lax.dot_general(q2, k_cat, nt, preferred_element_type=F32)
        es, inv_dens = [], []
        for half in range(2):
            segs, off = [], half * n_keys
            for slab, m in zip(ks, masks):
                seg = s[:, off:off + slab.shape[0]]
                segs.append(seg if m is None else jnp.where(m, seg, NEG_INF))
                off += slab.shape[0]
            logits = jnp.concatenate(segs, axis=1)
            sink = jnp.where(row_hi, sink_ref[4 * kh + 2 + half], sink_ref[4 * kh + half])
            m_row = jnp.maximum(jnp.max(logits, axis=-1, keepdims=True), sink)
            e = jnp.exp(logits - m_row)
            den = jnp.sum(e, axis=-1, keepdims=True) + jnp.exp(sink - m_row)
            es.append(e.astype(BF16))
            inv_dens.append(1.0 / den)
        p = jnp.concatenate(es, axis=1)
        o2 = jnp.dot(p, v_cat, preferred_element_type=F32)
        o2 = o2 * jnp.where(lane_lo, inv_dens[0], inv_dens[1])
        o_ref[:, j0 * LANES:(j0 + 1) * LANES] = o2[:tq].astype(BF16)
        o_ref[:, (j0 + 1) * LANES:(j0 + 2) * LANES] = o2[tq:].astype(BF16)


def _kv_cols(kh):
    return slice(kh * 2 * LANES, (kh + 1) * 2 * LANES)


def _ctx_attn_kernel(sink_ref, q_ref, k_ref, v_ref, o_ref):
    _attend(q_ref, o_ref, sink_ref,
            lambda kh: [k_ref[:, _kv_cols(kh)]],
            lambda kh: [v_ref[:, _kv_cols(kh)]],
            [None], SEQ)


def _ctx_attn(sink, q, kab, vab):
    return pl.pallas_call(
        _ctx_attn_kernel,
        grid=(BATCH,),
        in_specs=[
            pl.BlockSpec(memory_space=pltpu.SMEM),
            pl.BlockSpec((SEQ, D_Q), lambda b: (b, 0)),
            pl.BlockSpec((SEQ, D_KV_AB), lambda b: (b, 0)),
            pl.BlockSpec((SEQ, D_KV_AB), lambda b: (b, 0)),
        ],
        out_specs=pl.BlockSpec((SEQ, D_Q), lambda b: (b, 0)),
        out_shape=jax.ShapeDtypeStruct((N_PROMPT_TOK, D_Q), BF16),
        compiler_params=_params(),
        name="ctx_attn",
    )(sink, q, kab, vab)


N_QBLK = DEC_SEQ // BLOCK


def _lat_attn_kernel(sink_ref, q_ref, kp_ref, kc_ref, kn_ref, vp_ref, vc_ref, vn_ref,
                     kx_ref, vx_ref, o_ref):
    qb = pl.program_id(1)
    r = lax.broadcasted_iota(jnp.int32, (2 * BLOCK, BLOCK), 0) & (BLOCK - 1)
    c = lax.broadcasted_iota(jnp.int32, (2 * BLOCK, BLOCK), 1)
    m_prev = c >= r + jnp.where(qb > 0, 0, BLOCK)
    m_next = c <= r - jnp.where(qb < N_QBLK - 1, 0, BLOCK)

    def slabs(p_ref, c_ref, n_ref, x_ref):
        return lambda kh: [p_ref[:, _kv_cols(kh)], c_ref[:, _kv_cols(kh)],
                           n_ref[:, _kv_cols(kh)], x_ref[0, :, _kv_cols(kh)]]

    _attend(q_ref, o_ref, sink_ref,
            slabs(kp_ref, kc_ref, kn_ref, kx_ref), slabs(vp_ref, vc_ref, vn_ref, vx_ref),
            [m_prev, None, m_next, None], BLOCK)


def _lat_attn(sink, q, kab, vab, kab_ctx, vab_ctx):
    first = N_PROMPT_TOK // BLOCK

    def cur(b, i):
        return (first + b * N_QBLK + i, 0)

    def prev(b, i):
        return (first + b * N_QBLK + jnp.maximum(i - 1, 0), 0)

    def nxt(b, i):
        return (first + b * N_QBLK + jnp.minimum(i + 1, N_QBLK - 1), 0)

    kv_spec = lambda f: pl.BlockSpec((BLOCK, D_KV_AB), f)
    ctx_spec = pl.BlockSpec((1, PAST_LEN, D_KV_AB), lambda b, i: (b, 0, 0))
    return pl.pallas_call(
        _lat_attn_kernel,
        grid=(DEC_BATCH, N_QBLK),
        in_specs=[
            pl.BlockSpec(memory_space=pltpu.SMEM),
            pl.BlockSpec((BLOCK, D_Q), cur),
            kv_spec(prev), kv_spec(cur), kv_spec(nxt),
            kv_spec(prev), kv_spec(cur), kv_spec(nxt),
            ctx_spec, ctx_spec,
        ],
        out_specs=pl.BlockSpec((BLOCK, D_Q), lambda b, i: (b * N_QBLK + i, 0)),
        out_shape=jax.ShapeDtypeStruct((N_LATENT_TOK, D_Q), BF16),
        compiler_params=_params(2),
        name="lat_attn",
    )(sink, q, kab, kab, kab, vab, vab, vab, kab_ctx, vab_ctx)


def _proj_kernel(x_ref, ap_ref, al_ref, mod_ref, lng_ref, lnb_ref, wo_ref, o_ref):
    f = jnp.dot(_read_split(ap_ref, al_ref), wo_ref[...], preferred_element_type=F32)
    gate = mod_ref[5:6, :]
    y = DEEPNORM_ALPHA * x_ref[...] + gate * f
    o_ref[...] = _layer_norm(y, lng_ref[...], lnb_ref[...])


def _proj(x, attn_prompt, attn_latent, mods, ln_g, ln_b, w_o):
    return pl.pallas_call(
        _proj_kernel,
        grid=(N_TOK // TM,),
        in_specs=[
            _slab_spec(D_MODEL),
            _prompt_spec(D_Q),
            _latent_spec(D_Q),
            _mod_spec(TM, 0)] + _ln_specs(0, 1) + [
            _resident((D_Q, D_MODEL)),
        ],
        out_specs=_slab_spec(D_MODEL),
        out_shape=jax.ShapeDtypeStruct((N_TOK, D_MODEL), F32),
        compiler_params=_params(),
        name="attn_proj",
    )(x, attn_prompt, attn_latent, mods, ln_g, ln_b, w_o)


PROMPT_POOL_TILES = N_PROMPT_TOK // POOL_TM
LATENT_POOL_TILES = DEC_SEQ // POOL_TM


def _pool_kernel(x_ref, xp_ref, xn_ref, mod_ref, lng_ref, lnb_ref, w_ref, sc_ref, o_ref, h_ref):
    i = pl.program_id(0)
    in_seq = (i - PROMPT_POOL_TILES) % LATENT_POOL_TILES
    is_start = (i < PROMPT_POOL_TILES) | (in_seq == 0)
    is_end = (i < PROMPT_POOL_TILES) | (in_seq == LATENT_POOL_TILES - 1)

    x = x_ref[...]
    h = _modulate(x, mod_ref, 3)
    h_ref[POOL_HALO:POOL_HALO + POOL_TM, :] = h
    h_ref[0:POOL_HALO, :] = jnp.where(is_start, 0.0, _modulate(xp_ref[...], mod_ref, 3))
    h_ref[POOL_HALO + POOL_TM:, :] = jnp.where(is_end, 0.0, _modulate(xn_ref[...], mod_ref, 3))

    r = lax.broadcasted_iota(jnp.int32, (POOL_TM, POOL_GROUP_DIM), 0)
    gate = mod_ref[5:6, :]
    outs = []
    for gi, w in enumerate(POOL_WINDOWS):
        half = w // 2
        cols = slice(gi * POOL_GROUP_DIM, (gi + 1) * POOL_GROUP_DIM)
        total = h_ref[POOL_HALO - half:POOL_HALO - half + POOL_TM, cols]
        for d in range(-half + 1, half):
            total = total + h_ref[POOL_HALO + d:POOL_HALO + d + POOL_TM, cols]
        clipped_lo = jnp.where(is_start, jnp.maximum(half - r, 0), 0)
        clipped_hi = jnp.where(is_end, jnp.maximum(r + half - POOL_TM, 0), 0)
        cnt = (w - clipped_lo - clipped_hi).astype(F32)
        pooled = (total / cnt - h[:, cols]).astype(BF16)
        outs.append(jnp.dot(pooled, w_ref[gi], preferred_element_type=F32))
    mixed = jnp.concatenate(outs, axis=-1) * sc_ref[...]
    y = DEEPNORM_ALPHA * x + gate * mixed
    o_ref[...] = _layer_norm(y, lng_ref[...], lnb_ref[...])


def _pool(x, mods, ln_g, ln_b, w_pool, scale):
    halo_per_tile = POOL_TM // POOL_HALO
    last_halo = N_TOK // POOL_HALO - 1
    return pl.pallas_call(
        _pool_kernel,
        grid=(N_TOK // POOL_TM,),
        in_specs=[
            pl.BlockSpec((POOL_TM, D_MODEL), lambda i: (i, 0)),
            pl.BlockSpec((POOL_HALO, D_MODEL), lambda i: (jnp.maximum(i * halo_per_tile - 1, 0), 0)),
            pl.BlockSpec((POOL_HALO, D_MODEL),
                         lambda i: (jnp.minimum((i + 1) * halo_per_tile, last_halo), 0)),
            _mod_spec(POOL_TM, 1)] + _ln_specs(1, 1) + [
            _resident((len(POOL_WINDOWS), POOL_GROUP_DIM, POOL_GROUP_DIM)),
            _resident((1, D_MODEL)),
        ],
        out_specs=pl.BlockSpec((POOL_TM, D_MODEL), lambda i: (i, 0)),
        out_shape=jax.ShapeDtypeStruct((N_TOK, D_MODEL), F32),
        scratch_shapes=[pltpu.VMEM((POOL_TM + 2 * POOL_HALO, D_MODEL), F32)],
        compiler_params=_params(),
        name="pool",
    )(x, x, x, mods, ln_g, ln_b, w_pool, scale.reshape(1, D_MODEL))


def _ab_layout(a):
    lead = a.shape[:-1]
    a = a.reshape(lead + (N_KV_HEADS, 1, HEAD_DIM)).astype(BF16)
    z = jnp.zeros_like(a)
    return jnp.concatenate([a, z, z, a], axis=-2).reshape(lead + (D_KV_AB,))


def _rope_tables():
    n_rows = DEC_SEQ // GRID_W
    rows = jnp.repeat(jnp.arange(n_rows, dtype=F32), GRID_W)
    cols = jnp.tile(jnp.arange(GRID_W, dtype=F32), n_rows)
    inv = jnp.power(ROPE_BASE, -jnp.arange(N_FREQ, dtype=F32) / N_FREQ)
    ang_r = rows[:, None] * inv
    ang_c = cols[:, None] * inv
    ang = jnp.concatenate([ang_r, ang_r, ang_c, ang_c], axis=-1)
    cos = jnp.tile(jnp.cos(ang), (1, LANES // HEAD_DIM))
    sin = jnp.tile(jnp.sin(ang), (1, LANES // HEAD_DIM))
    first_half = (jnp.arange(LANES) % 32) < 16
    s_up = jnp.where(first_half, -sin, 0.0)
    s_dn = jnp.where(first_half, 0.0, sin)
    ident = jnp.ones((TM, LANES), F32)
    zeros = jnp.zeros((TM, LANES), F32)
    return (jnp.concatenate([cos, ident], axis=0),
            jnp.concatenate([s_up, zeros], axis=0),
            jnp.concatenate([s_dn, zeros], axis=0))


def kernel(x_prompt, x_sample, cache_k, cache_v, c, c_ctx, w_mod, b_mod, ln_g, ln_b,
           ffn_w_gate, ffn_w_up, ffn_w_down, attn_w_qkv, attn_w_o, attn_sink,
           pool_w, pool_scale):
    cond = jnp.concatenate(
        [c_ctx[None, :], c, jnp.zeros((COND_ROWS - N_COND, D_MODEL), F32)], axis=0)
    mods = _adaln(cond, w_mod, b_mod)

    wg = ffn_w_gate.astype(BF16)
    wu = ffn_w_up.astype(BF16)
    wd = ffn_w_down.astype(BF16)
    ln_g = ln_g.reshape(DEPTH * 3, 1, D_MODEL)
    ln_b = ln_b.reshape(DEPTH * 3, 1, D_MODEL)
    ffn = functools.partial(_ffn, mods=mods, ln_g=ln_g, ln_b=ln_b, wg=wg, wu=wu, wd=wd)

    x = ffn((x_prompt.reshape(N_PROMPT_TOK, D_MODEL), x_sample.reshape(N_LATENT_TOK, D_MODEL)),
            layer=0, which=0)
    cos_t, sup_t, sdn_t = _rope_tables()
    q, kab, vab, k_state, v_state = _qkv(x, mods, attn_w_qkv[0].astype(BF16), cos_t, sup_t, sdn_t)
    sink = attn_sink[0]
    o_ctx = _ctx_attn(sink, q, kab, vab)
    kab_ctx = _ab_layout(cache_k[:, 0].reshape(DEC_BATCH, PAST_LEN, D_KV))
    vab_ctx = _ab_layout(cache_v[:, 0].reshape(DEC_BATCH, PAST_LEN, D_KV))
    o_lat = _lat_attn(sink, q, kab, vab, kab_ctx, vab_ctx)
    x = _proj(x, o_ctx, o_lat, mods, ln_g, ln_b, attn_w_o[0].astype(BF16))
    x = ffn((x,), layer=0, which=1)

    x = ffn((x,), layer=1, which=0)
    x = _pool(x, mods, ln_g, ln_b, pool_w[0].astype(BF16), pool_scale[0])
    y_prompt, y_sample = ffn((x,), layer=1, which=1, split_out=True)

    def state(s_t):
        s_t = s_t.reshape(BATCH, N_KV_HEADS, HEAD_DIM, SEQ)
        return jnp.transpose(s_t, (0, 3, 1, 2))[:, None]

    return (y_prompt.reshape(BATCH, SEQ, D_MODEL),
            y_sample.reshape(DEC_BATCH, DEC_SEQ, D_MODEL),
            state(k_state), state(v_state))
```

```python
import functools

import jax
import jax.numpy as jnp
from jax import lax
from jax.experimental import pallas as pl
from jax.experimental.pallas import tpu as pltpu

D_MODEL = 1024
BATCH = 16
SEQ = 256
DEPTH = 2
DEC_BATCH = 2
DEC_SEQ = 4096
PAST_LEN = 256
GRID_W = 64
N_HEADS = 16
N_KV_HEADS = 4
HEAD_DIM = 64
WINDOW = 128
BLOCK = 128
ROPE_BASE = 10000.0
N_FREQ = HEAD_DIM // 4
POOL_WINDOWS = (2, 4, 8, 16)
POOL_GROUP_DIM = D_MODEL // 4
D_FF = 2816
N_MOD = 9
LN_EPS = 1e-5
DEEPNORM_ALPHA = (2.0 * DEPTH) ** 0.25
ATTN_SCALE = HEAD_DIM ** -0.5
NEG_INF = -1e30

N_PROMPT_TOK = BATCH * SEQ
N_LATENT_TOK = DEC_BATCH * DEC_SEQ
N_TOK = N_PROMPT_TOK + N_LATENT_TOK
MOD_GROUP_ROWS = 4096
N_COND = 1 + DEC_BATCH
COND_ROWS = 8

LANES = 128
HALF_LANES = LANES // 2
TM = 1024
N_PROMPT_TILES = N_PROMPT_TOK // TM
FF_CHUNK = 256
POOL_TM = 256
POOL_HALO = 8
D_Q = N_HEADS * HEAD_DIM
D_KV = N_KV_HEADS * HEAD_DIM
D_KV_AB = N_KV_HEADS * 2 * LANES
VMEM_LIMIT = 56 * 1024 * 1024

F32 = jnp.float32
BF16 = jnp.bfloat16


def _layer_norm(y, g, b):
    mu = jnp.mean(y, axis=-1, keepdims=True)
    yc = y - mu
    var = jnp.mean(yc * yc, axis=-1, keepdims=True)
    return yc * lax.rsqrt(var + LN_EPS) * g + b


def _modulate(x, mod_ref, row0):
    shift = mod_ref[row0:row0 + 1, :]
    scale = mod_ref[row0 + 1:row0 + 2, :]
    return x * (1.0 + scale) + shift


def _params(n_axes=1):
    return pltpu.CompilerParams(
        dimension_semantics=("arbitrary",) * n_axes,
        vmem_limit_bytes=VMEM_LIMIT)


def _resident(shape):
    nd = len(shape)
    return pl.BlockSpec(shape, lambda *_: (0,) * nd, pipeline_mode=pl.Buffered(1))


def _resident_at(index, tail):
    return pl.BlockSpec((None,) * len(index) + tuple(tail),
                        lambda *_: tuple(index) + (0,) * len(tail), pipeline_mode=pl.Buffered(1))


def _mod_spec(tile_rows, layer):
    tiles_per_group = MOD_GROUP_ROWS // tile_rows
    return pl.BlockSpec((None, None, N_MOD, D_MODEL),
                        lambda i: (layer, i // tiles_per_group, 0, 0))


def _ln_specs(layer, which):
    return [_resident_at((layer * 3 + which,), (1, D_MODEL))] * 2


def _slab_spec(cols):
    return pl.BlockSpec((TM, cols), lambda i: (i, 0))


def _prompt_spec(cols):
    return pl.BlockSpec((TM, cols), lambda i: (jnp.minimum(i, N_PROMPT_TILES - 1), 0))


def _latent_spec(cols):
    return pl.BlockSpec((TM, cols), lambda i: (jnp.maximum(i - N_PROMPT_TILES, 0), 0))


def _read_split(p_ref, l_ref):
    return jnp.where(pl.program_id(0) < N_PROMPT_TILES, p_ref[...], l_ref[...])


ADALN_NC = 1024


def _adaln_kernel(cond_ref, w_ref, b_ref, o_ref):
    c = cond_ref[...]
    s = (c * (1.0 / (1.0 + jnp.exp(-c)))).astype(BF16)
    o_ref[0] = jnp.dot(s, w_ref[0].astype(BF16), preferred_element_type=F32) + b_ref[0]


def _adaln(cond, w_mod, b_mod):
    n_out = N_MOD * D_MODEL
    out = pl.pallas_call(
        _adaln_kernel,
        grid=(DEPTH, n_out // ADALN_NC),
        in_specs=[
            pl.BlockSpec((COND_ROWS, D_MODEL), lambda l, n: (0, 0)),
            pl.BlockSpec((1, D_MODEL, ADALN_NC), lambda l, n: (l, 0, n)),
            pl.BlockSpec((1, 1, ADALN_NC), lambda l, n: (l, 0, n)),
        ],
        out_specs=pl.BlockSpec((1, COND_ROWS, ADALN_NC), lambda l, n: (l, 0, n)),
        out_shape=jax.ShapeDtypeStruct((DEPTH, COND_ROWS, n_out), F32),
        compiler_params=_params(2),
        name="adaln",
    )(cond, w_mod, b_mod.reshape(DEPTH, 1, n_out))
    return out[:, :N_COND].reshape(DEPTH, N_COND, N_MOD, D_MODEL)


def _ffn_body(x, mod_ref, lng_ref, lnb_ref, wg_ref, wu_ref, wd_ref, a_ref, row0):
    h = _modulate(x, mod_ref, row0).astype(BF16)
    for c in range(D_FF // FF_CHUNK):
        sl = slice(c * FF_CHUNK, (c + 1) * FF_CHUNK)
        g = jnp.dot(h, wg_ref[:, sl], preferred_element_type=F32)
        u = jnp.dot(h, wu_ref[:, sl], preferred_element_type=F32)
        a_ref[:, sl] = (g * (1.0 / (1.0 + jnp.exp(-g))) * u).astype(BF16)
    f = jnp.dot(a_ref[...], wd_ref[...], preferred_element_type=F32)
    gate = mod_ref[row0 + 2:row0 + 3, :]
    y = DEEPNORM_ALPHA * x + (0.5 * gate) * f
    return _layer_norm(y, lng_ref[...], lnb_ref[...])


def _ffn_kernel(*refs, row0, split_in, split_out):
    n_x = 2 if split_in else 1
    x_refs, (mod_ref, lng_ref, lnb_ref, wg_ref, wu_ref, wd_ref) = refs[:n_x], refs[n_x:n_x + 6]
    o_refs, a_ref = refs[n_x + 6:-1], refs[-1]
    x = _read_split(*x_refs) if split_in else x_refs[0][...]
    out = _ffn_body(x, mod_ref, lng_ref, lnb_ref, wg_ref, wu_ref, wd_ref, a_ref, row0)
    if split_out:
        is_prompt = pl.program_id(0) < N_PROMPT_TILES

        @pl.when(is_prompt)
        def _():
            o_refs[0][...] = out

        @pl.when(jnp.logical_not(is_prompt))
        def _():
            o_refs[1][...] = out
    else:
        o_refs[0][...] = out


def _ffn(xs, mods, ln_g, ln_b, wg, wu, wd, layer, which, split_out=False):
    split_in = len(xs) == 2
    row0 = 6 * which
    ln_row = 2 * which
    x_specs = [_prompt_spec(D_MODEL), _latent_spec(D_MODEL)] if split_in else [_slab_spec(D_MODEL)]
    if split_out:
        out_specs = [_prompt_spec(D_MODEL), _latent_spec(D_MODEL)]
        out_shape = [jax.ShapeDtypeStruct((N_PROMPT_TOK, D_MODEL), F32),
                     jax.ShapeDtypeStruct((N_LATENT_TOK, D_MODEL), F32)]
    else:
        out_specs = [_slab_spec(D_MODEL)]
        out_shape = [jax.ShapeDtypeStruct((N_TOK, D_MODEL), F32)]
    outs = pl.pallas_call(
        functools.partial(_ffn_kernel, row0=row0, split_in=split_in, split_out=split_out),
        grid=(N_TOK // TM,),
        in_specs=x_specs + [_mod_spec(TM, layer)] + _ln_specs(layer, ln_row) + [
            _resident_at((layer, which), (D_MODEL, D_FF)),
            _resident_at((layer, which), (D_MODEL, D_FF)),
            _resident_at((layer, which), (D_FF, D_MODEL)),
        ],
        out_specs=out_specs,
        out_shape=out_shape,
        scratch_shapes=[pltpu.VMEM((TM, D_FF), BF16)],
        compiler_params=_params(),
        name="ffn",
    )(*xs, mods, ln_g, ln_b, wg, wu, wd)
    return outs if split_out else outs[0]


ROPE_TILES = DEC_SEQ // TM


def _write_ab(dst_ref, pair, src):
    lo = lax.broadcasted_iota(jnp.int32, src.shape, 1) < HALF_LANES
    swapped = pltpu.roll(src, HALF_LANES, 1)
    blocks = (jnp.where(lo, src, 0.0), jnp.where(lo, 0.0, swapped),
              jnp.where(lo, swapped, 0.0), jnp.where(lo, 0.0, src))
    for n, blk in enumerate(blocks):
        c0 = (4 * pair + n) * LANES
        dst_ref[:, c0:c0 + LANES] = blk.astype(BF16)


def _qkv_kernel(x_ref, mod_ref, w_ref, cos_ref, sup_ref, sdn_ref,
                q_ref, kab_ref, vab_ref, ks_ref, vs_ref):
    h = _modulate(x_ref[...], mod_ref, 3).astype(BF16)
    qkv = jnp.dot(h, w_ref[...], preferred_element_type=F32)
    cos = cos_ref[...]
    s_up = sup_ref[...]
    s_dn = sdn_ref[...]

    def rope(blk):
        up = pltpu.roll(blk, LANES - 16, 1)
        dn = pltpu.roll(blk, 16, 1)
        return blk * cos + up * s_up + dn * s_dn

    for j in range(D_Q // LANES):
        cols = slice(j * LANES, (j + 1) * LANES)
        q_ref[:, cols] = (rope(qkv[:, cols]) * ATTN_SCALE).astype(BF16)
    for pair in range(D_KV // LANES):
        k_cols = slice(D_Q + pair * LANES, D_Q + (pair + 1) * LANES)
        v_cols = slice(D_Q + D_KV + pair * LANES, D_Q + D_KV + (pair + 1) * LANES)
        _write_ab(kab_ref, pair, rope(qkv[:, k_cols]))
        _write_ab(vab_ref, pair, qkv[:, v_cols])

    @pl.when(pl.program_id(0) < N_PROMPT_TILES)
    def _():
        k_t = qkv[:, D_Q:D_Q + D_KV].T
        v_t = qkv[:, D_Q + D_KV:].T
        for s in range(TM // SEQ):
            ks_ref[s] = k_t[:, s * SEQ:(s + 1) * SEQ]
            vs_ref[s] = v_t[:, s * SEQ:(s + 1) * SEQ]


def _rope_index(i):
    return (jnp.where(i < N_PROMPT_TILES, ROPE_TILES, (i - N_PROMPT_TILES) % ROPE_TILES), 0)


def _qkv(x, mods, w, cos_t, sup_t, sdn_t):
    seqs_per_tile = TM // SEQ
    state_spec = pl.BlockSpec((seqs_per_tile, D_KV, SEQ),
                              lambda i: (jnp.minimum(i, N_PROMPT_TILES - 1), 0, 0))
    return pl.pallas_call(
        _qkv_kernel,
        grid=(N_TOK // TM,),
        in_specs=[
            _slab_spec(D_MODEL),
            _mod_spec(TM, 0),
            _resident((D_MODEL, D_Q + 2 * D_KV)),
            pl.BlockSpec((TM, LANES), _rope_index),
            pl.BlockSpec((TM, LANES), _rope_index),
            pl.BlockSpec((TM, LANES), _rope_index),
        ],
        out_specs=[
            _slab_spec(D_Q), _slab_spec(D_KV_AB), _slab_spec(D_KV_AB), state_spec, state_spec,
        ],
        out_shape=[
            jax.ShapeDtypeStruct((N_TOK, D_Q), BF16),
            jax.ShapeDtypeStruct((N_TOK, D_KV_AB), BF16),
            jax.ShapeDtypeStruct((N_TOK, D_KV_AB), BF16),
            jax.ShapeDtypeStruct((BATCH, D_KV, SEQ), F32),
            jax.ShapeDtypeStruct((BATCH, D_KV, SEQ), F32),
        ],
        compiler_params=_params(),
        name="qkv",
    )(x, mods, w, cos_t, sup_t, sdn_t)


def _attend(q_ref, o_ref, sink_ref, k_slabs, v_slabs, masks, tq):
    nt = (((1,), (1,)), ((), ()))
    row_hi = lax.broadcasted_iota(jnp.int32, (2 * tq, 1), 0) >= tq
    lane_lo = lax.broadcasted_iota(jnp.int32, (2 * tq, LANES), 1) < HALF_LANES
    for kh in range(N_KV_HEADS):
        ks = k_slabs(kh)
        vs = v_slabs(kh)
        k_cat = jnp.concatenate([s[:, :LANES] for s in ks] + [s[:, LANES:] for s in ks], axis=0)
        v_cat = jnp.concatenate([s[:, :LANES] for s in vs] + [s[:, LANES:] for s in vs], axis=0)
        n_keys = k_cat.shape[0] // 2
        j0 = 2 * kh
        q2 = jnp.concatenate([q_ref[:, j0 * LANES:(j0 + 1) * LANES],
                              q_ref[:, (j0 + 1) * LANES:(j0 + 2) * LANES]], axis=0)
        s = lax.dot_general(q2, k_cat, nt, preferred_element_type=F32)
        es, inv_dens = [], []
        for half in range(2):
            segs, off = [], half * n_keys
            for slab, m in zip(ks, masks):
                seg = s[:, off:off + slab.shape[0]]
                segs.append(seg if m is None else jnp.where(m, seg, NEG_INF))
                off += slab.shape[0]
            logits = jnp.concatenate(segs, axis=1)
            sink = jnp.where(row_hi, sink_ref[4 * kh + 2 + half], sink_ref[4 * kh + half])
            m_row = jnp.maximum(jnp.max(logits, axis=-1, keepdims=True), sink)
            e = jnp.exp(logits - m_row)
            den = jnp.sum(e, axis=-1, keepdims=True) + jnp.exp(sink - m_row)
            es.append(e.astype(BF16))
            inv_dens.append(1.0 / den)
        p = jnp.concatenate(es, axis=1)
        o2 = jnp.dot(p, v_cat, preferred_element_type=F32)
        o2 = o2 * jnp.where(lane_lo, inv_dens[0], inv_dens[1])
        o_ref[:, j0 * LANES:(j0 + 1) * LANES] = o2[:tq].astype(BF16)
        o_ref[:, (j0 + 1) * LANES:(j0 + 2) * LANES] = o2[tq:].astype(BF16)


def _kv_cols(kh):
    return slice(kh * 2 * LANES, (kh + 1) * 2 * LANES)


def _ctx_attn_kernel(sink_ref, q_ref, k_ref, v_ref, o_ref):
    _attend(q_ref, o_ref, sink_ref,
            lambda kh: [k_ref[:, _kv_cols(kh)]],
            lambda kh: [v_ref[:, _kv_cols(kh)]],
            [None], SEQ)


def _ctx_attn(sink, q, kab, vab):
    return pl.pallas_call(
        _ctx_attn_kernel,
        grid=(BATCH,),
        in_specs=[
            pl.BlockSpec(memory_space=pltpu.SMEM),
            pl.BlockSpec((SEQ, D_Q), lambda b: (b, 0)),
            pl.BlockSpec((SEQ, D_KV_AB), lambda b: (b, 0)),
            pl.BlockSpec((SEQ, D_KV_AB), lambda b: (b, 0)),
        ],
        out_specs=pl.BlockSpec((SEQ, D_Q), lambda b: (b, 0)),
        out_shape=jax.ShapeDtypeStruct((N_PROMPT_TOK, D_Q), BF16),
        compiler_params=_params(),
        name="ctx_attn",
    )(sink, q, kab, vab)


N_QBLK = DEC_SEQ // BLOCK


def _lat_attn_kernel(sink_ref, q_ref, kp_ref, kc_ref, kn_ref, vp_ref, vc_ref, vn_ref,
                     kx_ref, vx_ref, o_ref):
    qb = pl.program_id(1)
    r = lax.broadcasted_iota(jnp.int32, (2 * BLOCK, BLOCK), 0) & (BLOCK - 1)
    c = lax.broadcasted_iota(jnp.int32, (2 * BLOCK, BLOCK), 1)
    m_prev = c >= r + jnp.where(qb > 0, 0, BLOCK)
    m_next = c <= r - jnp.where(qb < N_QBLK - 1, 0, BLOCK)

    def slabs(p_ref, c_ref, n_ref, x_ref):
        return lambda kh: [p_ref[:, _kv_cols(kh)], c_ref[:, _kv_cols(kh)],
                           n_ref[:, _kv_cols(kh)], x_ref[0, :, _kv_cols(kh)]]

    _attend(q_ref, o_ref, sink_ref,
            slabs(kp_ref, kc_ref, kn_ref, kx_ref), slabs(vp_ref, vc_ref, vn_ref, vx_ref),
            [m_prev, None, m_next, None], BLOCK)


def _lat_attn(sink, q, kab, vab, kab_ctx, vab_ctx):
    first = N_PROMPT_TOK // BLOCK

    def cur(b, i):
        return (first + b * N_QBLK + i, 0)

    def prev(b, i):
        return (first + b * N_QBLK + jnp.maximum(i - 1, 0), 0)

    def nxt(b, i):
        return (first + b * N_QBLK + jnp.minimum(i + 1, N_QBLK - 1), 0)

    kv_spec = lambda f: pl.BlockSpec((BLOCK, D_KV_AB), f)
    ctx_spec = pl.BlockSpec((1, PAST_LEN, D_KV_AB), lambda b, i: (b, 0, 0))
    return pl.pallas_call(
        _lat_attn_kernel,
        grid=(DEC_BATCH, N_QBLK),
        in_specs=[
            pl.BlockSpec(memory_space=pltpu.SMEM),
            pl.BlockSpec((BLOCK, D_Q), cur),
            kv_spec(prev), kv_spec(cur), kv_spec(nxt),
            kv_spec(prev), kv_spec(cur), kv_spec(nxt),
            ctx_spec, ctx_spec,
        ],
        out_specs=pl.BlockSpec((BLOCK, D_Q), lambda b, i: (b * N_QBLK + i, 0)),
        out_shape=jax.ShapeDtypeStruct((N_LATENT_TOK, D_Q), BF16),
        compiler_params=_params(2),
        name="lat_attn",
    )(sink, q, kab, kab, kab, vab, vab, vab, kab_ctx, vab_ctx)


def _proj_kernel(x_ref, ap_ref, al_ref, mod_ref, lng_ref, lnb_ref, wo_ref, o_ref):
    f = jnp.dot(_read_split(ap_ref, al_ref), wo_ref[...], preferred_element_type=F32)
    gate = mod_ref[5:6, :]
    y = DEEPNORM_ALPHA * x_ref[...] + gate * f
    o_ref[...] = _layer_norm(y, lng_ref[...], lnb_ref[...])


def _proj(x, attn_prompt, attn_latent, mods, ln_g, ln_b, w_o):
    return pl.pallas_call(
        _proj_kernel,
        grid=(N_TOK // TM,),
        in_specs=[
            _slab_spec(D_MODEL),
            _prompt_spec(D_Q),
            _latent_spec(D_Q),
            _mod_spec(TM, 0)] + _ln_specs(0, 1) + [
            _resident((D_Q, D_MODEL)),
        ],
        out_specs=_slab_spec(D_MODEL),
        out_shape=jax.ShapeDtypeStruct((N_TOK, D_MODEL), F32),
        compiler_params=_params(),
        name="attn_proj",
    )(x, attn_prompt, attn_latent, mods, ln_g, ln_b, w_o)


PROMPT_POOL_TILES = N_PROMPT_TOK // POOL_TM
LATENT_POOL_TILES = DEC_SEQ // POOL_TM


def _pool_kernel(x_ref, xp_ref, xn_ref, mod_ref, lng_ref, lnb_ref, w_ref, sc_ref, o_ref, h_ref):
    i = pl.program_id(0)
    in_seq = (i - PROMPT_POOL_TILES) % LATENT_POOL_TILES
    is_start = (i < PROMPT_POOL_TILES) | (in_seq == 0)
    is_end = (i < PROMPT_POOL_TILES) | (in_seq == LATENT_POOL_TILES - 1)

    x = x_ref[...]
    h = _modulate(x, mod_ref, 3)
    h_ref[POOL_HALO:POOL_HALO + POOL_TM, :] = h
    h_ref[0:POOL_HALO, :] = jnp.where(is_start, 0.0, _modulate(xp_ref[...], mod_ref, 3))
    h_ref[POOL_HALO + POOL_TM:, :] = jnp.where(is_end, 0.0, _modulate(xn_ref[...], mod_ref, 3))

    r = lax.broadcasted_iota(jnp.int32, (POOL_TM, POOL_GROUP_DIM), 0)
    gate = mod_ref[5:6, :]
    outs = []
    for gi, w in enumerate(POOL_WINDOWS):
        half = w // 2
        cols = slice(gi * POOL_GROUP_DIM, (gi + 1) * POOL_GROUP_DIM)
        total = h_ref[POOL_HALO - half:POOL_HALO - half + POOL_TM, cols]
        for d in range(-half + 1, half):
            total = total + h_ref[POOL_HALO + d:POOL_HALO + d + POOL_TM, cols]
        clipped_lo = jnp.where(is_start, jnp.maximum(half - r, 0), 0)
        clipped_hi = jnp.where(is_end, jnp.maximum(r + half - POOL_TM, 0), 0)
        cnt = (w - clipped_lo - clipped_hi).astype(F32)
        pooled = (total / cnt - h[:, cols]).astype(BF16)
        outs.append(jnp.dot(pooled, w_ref[gi], preferred_element_type=F32))
    mixed = jnp.concatenate(outs, axis=-1) * sc_ref[...]
    y = DEEPNORM_ALPHA * x + gate * mixed
    o_ref[...] = _layer_norm(y, lng_ref[...], lnb_ref[...])


def _pool(x, mods, ln_g, ln_b, w_pool, scale):
    halo_per_tile = POOL_TM // POOL_HALO
    last_halo = N_TOK // POOL_HALO - 1
    return pl.pallas_call(
        _pool_kernel,
        grid=(N_TOK // POOL_TM,),
        in_specs=[
            pl.BlockSpec((POOL_TM, D_MODEL), lambda i: (i, 0)),
            pl.BlockSpec((POOL_HALO, D_MODEL), lambda i: (jnp.maximum(i * halo_per_tile - 1, 0), 0)),
            pl.BlockSpec((POOL_HALO, D_MODEL),
                         lambda i: (jnp.minimum((i + 1) * halo_per_tile, last_halo), 0)),
            _mod_spec(POOL_TM, 1)] + _ln_specs(1, 1) + [
            _resident((len(POOL_WINDOWS), POOL_GROUP_DIM, POOL_GROUP_DIM)),
            _resident((1, D_MODEL)),
        ],
        out_specs=pl.BlockSpec((POOL_TM, D_MODEL), lambda i: (i, 0)),
        out_shape=jax.ShapeDtypeStruct((N_TOK, D_MODEL), F32),
        scratch_shapes=[pltpu.VMEM((POOL_TM + 2 * POOL_HALO, D_MODEL), F32)],
        compiler_params=_params(),
        name="pool",
    )(x, x, x, mods, ln_g, ln_b, w_pool, scale.reshape(1, D_MODEL))


def _ab_layout(a):
    lead = a.shape[:-1]
    a = a.reshape(lead + (N_KV_HEADS, 1, HEAD_DIM)).astype(BF16)
    z = jnp.zeros_like(a)
    return jnp.concatenate([a, z, z, a], axis=-2).reshape(lead + (D_KV_AB,))


def _rope_tables():
    n_rows = DEC_SEQ // GRID_W
    rows = jnp.repeat(jnp.arange(n_rows, dtype=F32), GRID_W)
    cols = jnp.tile(jnp.arange(GRID_W, dtype=F32), n_rows)
    inv = jnp.power(ROPE_BASE, -jnp.arange(N_FREQ, dtype=F32) / N_FREQ)
    ang_r = rows[:, None] * inv
    ang_c = cols[:, None] * inv
    ang = jnp.concatenate([ang_r, ang_r, ang_c, ang_c], axis=-1)
    cos = jnp.tile(jnp.cos(ang), (1, LANES // HEAD_DIM))
    sin = jnp.tile(jnp.sin(ang), (1, LANES // HEAD_DIM))
    first_half = (jnp.arange(LANES) % 32) < 16
    s_up = jnp.where(first_half, -sin, 0.0)
    s_dn = jnp.where(first_half, 0.0, sin)
    ident = jnp.ones((TM, LANES), F32)
    zeros = jnp.zeros((TM, LANES), F32)
    return (jnp.concatenate([cos, ident], axis=0),
            jnp.concatenate([s_up, zeros], axis=0),
            jnp.concatenate([s_dn, zeros], axis=0))


def kernel(x_prompt, x_sample, cache_k, cache_v, c, c_ctx, w_mod, b_mod, ln_g, ln_b,
           ffn_w_gate, ffn_w_up, ffn_w_down, attn_w_qkv, attn_w_o, attn_sink,
           pool_w, pool_scale):
    cond = jnp.concatenate(
        [c_ctx[None, :], c, jnp.zeros((COND_ROWS - N_COND, D_MODEL), F32)], axis=0)
    mods = _adaln(cond, w_mod, b_mod)

    wg = ffn_w_gate.astype(BF16)
    wu = ffn_w_up.astype(BF16)
    wd = ffn_w_down.astype(BF16)
    ln_g = ln_g.reshape(DEPTH * 3, 1, D_MODEL)
    ln_b = ln_b.reshape(DEPTH * 3, 1, D_MODEL)
    ffn = functools.partial(_ffn, mods=mods, ln_g=ln_g, ln_b=ln_b, wg=wg, wu=wu, wd=wd)

    x = ffn((x_prompt.reshape(N_PROMPT_TOK, D_MODEL), x_sample.reshape(N_LATENT_TOK, D_MODEL)),
            layer=0, which=0)
    cos_t, sup_t, sdn_t = _rope_tables()
    q, kab, vab, k_state, v_state = _qkv(x, mods, attn_w_qkv[0].astype(BF16), cos_t, sup_t, sdn_t)
    sink = attn_sink[0]
    o_ctx = _ctx_attn(sink, q, kab, vab)
    kab_ctx = _ab_layout(cache_k[:, 0].reshape(DEC_BATCH, PAST_LEN, D_KV))
    vab_ctx = _ab_layout(cache_v[:, 0].reshape(DEC_BATCH, PAST_LEN, D_KV))
    o_lat = _lat_attn(sink, q, kab, vab, kab_ctx, vab_ctx)
    x = _proj(x, o_ctx, o_lat, mods, ln_g, ln_b, attn_w_o[0].astype(BF16))
    x = ffn((x,), layer=0, which=1)

    x = ffn((x,), layer=1, which=0)
    x = _pool(x, mods, ln_g, ln_b, pool_w[0].astype(BF16), pool_scale[0])
    y_prompt, y_sample = ffn((x,), layer=1, which=1, split_out=True)

    def state(s_t):
        s_t = s_t.reshape(BATCH, N_KV_HEADS, HEAD_DIM, SEQ)
        return jnp.transpose(s_t, (0, 3, 1, 2))[:, None]

    return (y_prompt.reshape(BATCH, SEQ, D_MODEL),
            y_sample.reshape(DEC_BATCH, DEC_SEQ, D_MODEL),
            state(k_state), state(v_state))
```

```python
import functools

import jax
import jax.numpy as jnp
from jax import lax
from jax.experimental import pallas as pl
from jax.experimental.pallas import tpu as pltpu

D_MODEL = 1024
BATCH = 16
SEQ = 256
DEPTH = 2
DEC_BATCH = 2
DEC_SEQ = 4096
PAST_LEN = 256
GRID_W = 64
N_HEADS = 16
N_KV_HEADS = 4
HEAD_DIM = 64
WINDOW = 128
BLOCK = 128
ROPE_BASE = 10000.0
N_FREQ = HEAD_DIM // 4
POOL_WINDOWS = (2, 4, 8, 16)
POOL_GROUP_DIM = D_MODEL // 4
D_FF = 2816
N_MOD = 9
LN_EPS = 1e-5
DEEPNORM_ALPHA = (2.0 * DEPTH) ** 0.25
ATTN_SCALE = HEAD_DIM ** -0.5
NEG_INF = -1e30

N_PROMPT_TOK = BATCH * SEQ
N_LATENT_TOK = DEC_BATCH * DEC_SEQ
N_TOK = N_PROMPT_TOK + N_LATENT_TOK
MOD_GROUP_ROWS = 4096
N_COND = 1 + DEC_BATCH
COND_ROWS = 8

LANES = 128
HALF_LANES = LANES // 2
TM = 1024
N_PROMPT_TILES = N_PROMPT_TOK // TM
FF_CHUNK = 256
POOL_TM = 256
POOL_HALO = 8
D_Q = N_HEADS * HEAD_DIM
D_KV = N_KV_HEADS * HEAD_DIM
D_KV_AB = N_KV_HEADS * 2 * LANES
VMEM_LIMIT = 40 * 1024 * 1024
FFN_VMEM_LIMIT = 55 * 1024 * 1024

F32 = jnp.float32
BF16 = jnp.bfloat16


def _layer_norm(y, g, b):
    mu = jnp.mean(y, axis=-1, keepdims=True)
    yc = y - mu
    var = jnp.mean(yc * yc, axis=-1, keepdims=True)
    return yc * lax.rsqrt(var + LN_EPS) * g + b


def _modulate(x, mod_ref, row0):
    shift = mod_ref[row0:row0 + 1, :]
    scale = mod_ref[row0 + 1:row0 + 2, :]
    return x * (1.0 + scale) + shift


def _params(n_axes=1, vmem_limit=VMEM_LIMIT):
    return pltpu.CompilerParams(
        dimension_semantics=("arbitrary",) * n_axes,
        vmem_limit_bytes=vmem_limit)


def _resident(shape):
    nd = len(shape)
    return pl.BlockSpec(shape, lambda *_: (0,) * nd, pipeline_mode=pl.Buffered(1))


def _resident_at(index, tail):
    return pl.BlockSpec((None,) * len(index) + tuple(tail),
                        lambda *_: tuple(index) + (0,) * len(tail), pipeline_mode=pl.Buffered(1))


def _mod_spec(tile_rows, layer):
    tiles_per_group = MOD_GROUP_ROWS // tile_rows
    return pl.BlockSpec((None, None, N_MOD, D_MODEL),
                        lambda i: (layer, i // tiles_per_group, 0, 0))


def _ln_specs(layer, which):
    return [_resident_at((layer * 3 + which,), (1, D_MODEL))] * 2


def _slab_spec(cols, tm=TM):
    return pl.BlockSpec((tm, cols), lambda i: (i, 0))


def _prompt_spec(cols, tm=TM):
    return pl.BlockSpec((tm, cols), lambda i: (jnp.minimum(i, N_PROMPT_TOK // tm - 1), 0))


def _latent_spec(cols, tm=TM):
    return pl.BlockSpec((tm, cols), lambda i: (jnp.maximum(i - N_PROMPT_TOK // tm, 0), 0))


def _read_split(p_ref, l_ref, n_prompt_tiles=N_PROMPT_TILES):
    return jnp.where(pl.program_id(0) < n_prompt_tiles, p_ref[...], l_ref[...])


ADALN_NC = 1024


def _adaln_kernel(cond_ref, w_ref, b_ref, o_ref):
    c = cond_ref[...]
    s = (c * (1.0 / (1.0 + jnp.exp(-c)))).astype(BF16)
    o_ref[0] = jnp.dot(s, w_ref[0].astype(BF16), preferred_element_type=F32) + b_ref[0]


def _adaln(cond, w_mod, b_mod):
    n_out = N_MOD * D_MODEL
    out = pl.pallas_call(
        _adaln_kernel,
        grid=(DEPTH, n_out // ADALN_NC),
        in_specs=[
            pl.BlockSpec((COND_ROWS, D_MODEL), lambda l, n: (0, 0)),
            pl.BlockSpec((1, D_MODEL, ADALN_NC), lambda l, n: (l, 0, n)),
            pl.BlockSpec((1, 1, ADALN_NC), lambda l, n: (l, 0, n)),
        ],
        out_specs=pl.BlockSpec((1, COND_ROWS, ADALN_NC), lambda l, n: (l, 0, n)),
        out_shape=jax.ShapeDtypeStruct((DEPTH, COND_ROWS, n_out), F32),
        compiler_params=_params(2),
        name="adaln",
    )(cond, w_mod, b_mod.reshape(DEPTH, 1, n_out))
    return out[:, :N_COND].reshape(DEPTH, N_COND, N_MOD, D_MODEL)


def _ffn_body(x, mod_ref, lng_ref, lnb_ref, wg_ref, wu_ref, wd_ref, a_ref, row0):
    h = _modulate(x, mod_ref, row0).astype(BF16)
    for c in range(D_FF // FF_CHUNK):
        sl = slice(c * FF_CHUNK, (c + 1) * FF_CHUNK)
        g = jnp.dot(h, wg_ref[:, sl], preferred_element_type=F32)
        u = jnp.dot(h, wu_ref[:, sl], preferred_element_type=F32)
        a_ref[:, sl] = (g * (1.0 / (1.0 + jnp.exp(-g))) * u).astype(BF16)
    f = jnp.dot(a_ref[...], wd_ref[...], preferred_element_type=F32)
    gate = mod_ref[row0 + 2:row0 + 3, :]
    y = DEEPNORM_ALPHA * x + (0.5 * gate) * f
    return _layer_norm(y, lng_ref[...], lnb_ref[...])


CAST_STEPS = 8
FFN_WEIGHT_SHAPES = ((D_MODEL, D_FF), (D_MODEL, D_FF), (D_FF, D_MODEL))


def _ffn_kernel(*refs, row0, n_prompt_tiles, split_in, split_out, cast_next):
    n_x = 2 if split_in else 1
    n_cast = 3 if cast_next else 0
    x_refs, (mod_ref, lng_ref, lnb_ref, wg_ref, wu_ref, wd_ref) = refs[:n_x], refs[n_x:n_x + 6]
    next_f32 = refs[n_x + 6:n_x + 6 + n_cast]
    o_refs, next_bf16, a_ref = refs[n_x + 6 + n_cast:-1 - n_cast], refs[-1 - n_cast:-1], refs[-1]
    step = pl.program_id(0)
    x = _read_split(*x_refs, n_prompt_tiles) if split_in else x_refs[0][...]
    out = _ffn_body(x, mod_ref, lng_ref, lnb_ref, wg_ref, wu_ref, wd_ref, a_ref, row0)
    if split_out:
        @pl.when(step < n_prompt_tiles)
        def _():
            o_refs[0][...] = out

        @pl.when(step >= n_prompt_tiles)
        def _():
            o_refs[1][...] = out
    else:
        o_refs[0][...] = out
    if cast_next:
        @pl.when(step < CAST_STEPS)
        def _():
            for src, dst in zip(next_f32, next_bf16):
                dst[...] = src[...].astype(BF16)


def _cast_block_spec(shape, lead):
    block = (None,) * len(lead) + (shape[0] // CAST_STEPS, shape[1])
    return pl.BlockSpec(block, lambda i: tuple(lead) + (jnp.minimum(i, CAST_STEPS - 1), 0))


def _ffn(xs, mods, ln_g, ln_b, weights, layer, which, next_f32=None, split_out=False):
    split_in = len(xs) == 2
    cast_next = next_f32 is not None
    row0 = 6 * which
    ln_row = 2 * which
    tm = TM // 2 if split_in and cast_next else TM
    x_specs = ([_prompt_spec(D_MODEL, tm), _latent_spec(D_MODEL, tm)] if split_in
               else [_slab_spec(D_MODEL, tm)])
    if split_out:
        out_specs = [_prompt_spec(D_MODEL, tm), _latent_spec(D_MODEL, tm)]
        out_shape = [jax.ShapeDtypeStruct((N_PROMPT_TOK, D_MODEL), F32),
                     jax.ShapeDtypeStruct((N_LATENT_TOK, D_MODEL), F32)]
    else:
        out_specs = [_slab_spec(D_MODEL, tm)]
        out_shape = [jax.ShapeDtypeStruct((N_TOK, D_MODEL), F32)]
    n_x_out = len(out_specs)
    cast_in_specs, cast_args = [], ()
    if cast_next:
        cast_args, lead = next_f32
        cast_in_specs = [_cast_block_spec(s, lead) for s in FFN_WEIGHT_SHAPES]
        out_specs = out_specs + [_cast_block_spec(s, ()) for s in FFN_WEIGHT_SHAPES]
        out_shape = out_shape + [jax.ShapeDtypeStruct(s, BF16) for s in FFN_WEIGHT_SHAPES]
    outs = pl.pallas_call(
        functools.partial(_ffn_kernel, row0=row0, n_prompt_tiles=N_PROMPT_TOK // tm,
                          split_in=split_in, split_out=split_out, cast_next=cast_next),
        grid=(N_TOK // tm,),
        in_specs=x_specs + [_mod_spec(tm, layer)] + _ln_specs(layer, ln_row)
        + [_resident(s) for s in FFN_WEIGHT_SHAPES] + cast_in_specs,
        out_specs=out_specs,
        out_shape=out_shape,
        scratch_shapes=[pltpu.VMEM((tm, D_FF), BF16)],
        compiler_params=_params(vmem_limit=FFN_VMEM_LIMIT),
        name="ffn",
    )(*xs, mods, ln_g, ln_b, *weights, *cast_args)
    x_out = tuple(outs[:n_x_out]) if split_out else outs[0]
    return x_out, (tuple(outs[n_x_out:]) if cast_next else None)


ROPE_TILES = DEC_SEQ // TM


def _write_ab(dst_ref, pair, src):
    lo = lax.broadcasted_iota(jnp.int32, src.shape, 1) < HALF_LANES
    swapped = pltpu.roll(src, HALF_LANES, 1)
    blocks = (jnp.where(lo, src, 0.0), jnp.where(lo, 0.0, swapped),
              jnp.where(lo, swapped, 0.0), jnp.where(lo, 0.0, src))
    for n, blk in enumerate(blocks):
        c0 = (4 * pair + n) * LANES
        dst_ref[:, c0:c0 + LANES] = blk.astype(BF16)


def _qkv_kernel(x_ref, mod_ref, w_ref, cos_ref, sup_ref, sdn_ref,
                q_ref, kab_ref, vab_ref, ks_ref, vs_ref):
    h = _modulate(x_ref[...], mod_ref, 3).astype(BF16)
    qkv = jnp.dot(h, w_ref[...], preferred_element_type=F32)
    cos = cos_ref[...]
    s_up = sup_ref[...]
    s_dn = sdn_ref[...]

    def rope(blk):
        up = pltpu.roll(blk, LANES - 16, 1)
        dn = pltpu.roll(blk, 16, 1)
        return blk * cos + up * s_up + dn * s_dn

    for j in range(D_Q // LANES):
        cols = slice(j * LANES, (j + 1) * LANES)
        q_ref[:, cols] = (rope(qkv[:, cols]) * ATTN_SCALE).astype(BF16)
    for pair in range(D_KV // LANES):
        k_cols = slice(D_Q + pair * LANES, D_Q + (pair + 1) * LANES)
        v_cols = slice(D_Q + D_KV + pair * LANES, D_Q + D_KV + (pair + 1) * LANES)
        _write_ab(kab_ref, pair, rope(qkv[:, k_cols]))
        _write_ab(vab_ref, pair, qkv[:, v_cols])

    @pl.when(pl.program_id(0) < N_PROMPT_TILES)
    def _():
        k_t = qkv[:, D_Q:D_Q + D_KV].T
        v_t = qkv[:, D_Q + D_KV:].T
        for s in range(TM // SEQ):
            ks_ref[s] = k_t[:, s * SEQ:(s + 1) * SEQ]
            vs_ref[s] = v_t[:, s * SEQ:(s + 1) * SEQ]


def _rope_index(i):
    return (jnp.where(i < N_PROMPT_TILES, ROPE_TILES, (i - N_PROMPT_TILES) % ROPE_TILES), 0)


def _qkv(x, mods, w, cos_t, sup_t, sdn_t):
    seqs_per_tile = TM // SEQ
    state_spec = pl.BlockSpec((seqs_per_tile, D_KV, SEQ),
                              lambda i: (jnp.minimum(i, N_PROMPT_TILES - 1), 0, 0))
    return pl.pallas_call(
        _qkv_kernel,
        grid=(N_TOK // TM,),
        in_specs=[
            _slab_spec(D_MODEL),
            _mod_spec(TM, 0),
            _resident((D_MODEL, D_Q + 2 * D_KV)),
            pl.BlockSpec((TM, LANES), _rope_index),
            pl.BlockSpec((TM, LANES), _rope_index),
            pl.BlockSpec((TM, LANES), _rope_index),
        ],
        out_specs=[
            _slab_spec(D_Q), _slab_spec(D_KV_AB), _slab_spec(D_KV_AB), state_spec, state_spec,
        ],
        out_shape=[
            jax.ShapeDtypeStruct((N_TOK, D_Q), BF16),
            jax.ShapeDtypeStruct((N_TOK, D_KV_AB), BF16),
            jax.ShapeDtypeStruct((N_TOK, D_KV_AB), BF16),
            jax.ShapeDtypeStruct((BATCH, D_KV, SEQ), F32),
            jax.ShapeDtypeStruct((BATCH, D_KV, SEQ), F32),
        ],
        compiler_params=_params(),
        name="qkv",
    )(x, mods, w, cos_t, sup_t, sdn_t)


def _attend(q_ref, o_ref, sink_ref, k_slabs, v_slabs, masks, tq):
    nt = (((1,), (1,)), ((), ()))
    row_hi = lax.broadcasted_iota(jnp.int32, (2 * tq, 1), 0) >= tq
    lane_lo = lax.broadcasted_iota(jnp.int32, (2 * tq, LANES), 1) < HALF_LANES
    for kh in range(N_KV_HEADS):
        ks = k_slabs(kh)
        vs = v_slabs(kh)
        k_cat = jnp.concatenate([s[:, :LANES] for s in ks] + [s[:, LANES:] for s in ks], axis=0)
        v_cat = jnp.concatenate([s[:, :LANES] for s in vs] + [s[:, LANES:] for s in vs], axis=0)
        n_keys = k_cat.shape[0] // 2
        j0 = 2 * kh
        q2 = jnp.concatenate([q_ref[:, j0 * LANES:(j0 + 1) * LANES],
                              q_ref[:, (j0 + 1) * LANES:(j0 + 2) * LANES]], axis=0)
        s = lax.dot_general(q2, k_cat, nt, preferred_element_type=F32)
        es, inv_dens = [], []
        for half in range(2):
            segs, off = [], half * n_keys
            for slab, m in zip(ks, masks):
                seg = s[:, off:off + slab.shape[0]]
                segs.append(seg if m is None else jnp.where(m, seg, NEG_INF))
                off += slab.shape[0]
            logits = jnp.concatenate(segs, axis=1)
            sink = jnp.where(row_hi, sink_ref[4 * kh + 2 + half], sink_ref[4 * kh + half])
            m_row = jnp.maximum(jnp.max(logits, axis=-1, keepdims=True), sink)
            e = jnp.exp(logits - m_row)
            den = jnp.sum(e, axis=-1, keepdims=True) + jnp.exp(sink - m_row)
            es.append(e.astype(BF16))
            inv_dens.append(1.0 / den)
        p = jnp.concatenate(es, axis=1)
        o2 = jnp.dot(p, v_cat, preferred_element_type=F32)
        o2 = o2 * jnp.where(lane_lo, inv_dens[0], inv_dens[1])
        o_ref[:, j0 * LANES:(j0 + 1) * LANES] = o2[:tq].astype(BF16)
        o_ref[:, (j0 + 1) * LANES:(j0 + 2) * LANES] = o2[tq:].astype(BF16)


def _kv_cols(kh):
    return slice(kh * 2 * LANES, (kh + 1) * 2 * LANES)


def _ctx_attn_kernel(sink_ref, q_ref, k_ref, v_ref, o_ref):
    _attend(q_ref, o_ref, sink_ref,
            lambda kh: [k_ref[:, _kv_cols(kh)]],
            lambda kh: [v_ref[:, _kv_cols(kh)]],
            [None], SEQ)


def _ctx_attn(sink, q, kab, vab):
    return pl.pallas_call(
        _ctx_attn_kernel,
        grid=(BATCH,),
        in_specs=[
            pl.BlockSpec(memory_space=pltpu.SMEM),
            pl.BlockSpec((SEQ, D_Q), lambda b: (b, 0)),
            pl.BlockSpec((SEQ, D_KV_AB), lambda b: (b, 0)),
            pl.BlockSpec((SEQ, D_KV_AB), lambda b: (b, 0)),
        ],
        out_specs=pl.BlockSpec((SEQ, D_Q), lambda b: (b, 0)),
        out_shape=jax.ShapeDtypeStruct((N_PROMPT_TOK, D_Q), BF16),
        compiler_params=_params(),
        name="ctx_attn",
    )(sink, q, kab, vab)


N_QBLK = DEC_SEQ // BLOCK


def _lat_attn_kernel(sink_ref, q_ref, kp_ref, kc_ref, kn_ref, vp_ref, vc_ref, vn_ref,
                     kx_ref, vx_ref, o_ref):
    qb = pl.program_id(1)
    r = lax.broadcasted_iota(jnp.int32, (2 * BLOCK, BLOCK), 0) & (BLOCK - 1)
    c = lax.broadcasted_iota(jnp.int32, (2 * BLOCK, BLOCK), 1)
    m_prev = c >= r + jnp.where(qb > 0, 0, BLOCK)
    m_next = c <= r - jnp.where(qb < N_QBLK - 1, 0, BLOCK)

    def slabs(p_ref, c_ref, n_ref, x_ref):
        return lambda kh: [p_ref[:, _kv_cols(kh)], c_ref[:, _kv_cols(kh)],
                           n_ref[:, _kv_cols(kh)], x_ref[0, :, _kv_cols(kh)]]

    _attend(q_ref, o_ref, sink_ref,
            slabs(kp_ref, kc_ref, kn_ref, kx_ref), slabs(vp_ref, vc_ref, vn_ref, vx_ref),
            [m_prev, None, m_next, None], BLOCK)


def _lat_attn(sink, q, kab, vab, kab_ctx, vab_ctx):
    first = N_PROMPT_TOK // BLOCK

    def cur(b, i):
        return (first + b * N_QBLK + i, 0)

    def prev(b, i):
        return (first + b * N_QBLK + jnp.maximum(i - 1, 0), 0)

    def nxt(b, i):
        return (first + b * N_QBLK + jnp.minimum(i + 1, N_QBLK - 1), 0)

    kv_spec = lambda f: pl.BlockSpec((BLOCK, D_KV_AB), f)
    ctx_spec = pl.BlockSpec((1, PAST_LEN, D_KV_AB), lambda b, i: (b, 0, 0))
    return pl.pallas_call(
        _lat_attn_kernel,
        grid=(DEC_BATCH, N_QBLK),
        in_specs=[
            pl.BlockSpec(memory_space=pltpu.SMEM),
            pl.BlockSpec((BLOCK, D_Q), cur),
            kv_spec(prev), kv_spec(cur), kv_spec(nxt),
            kv_spec(prev), kv_spec(cur), kv_spec(nxt),
            ctx_spec, ctx_spec,
        ],
        out_specs=pl.BlockSpec((BLOCK, D_Q), lambda b, i: (b * N_QBLK + i, 0)),
        out_shape=jax.ShapeDtypeStruct((N_LATENT_TOK, D_Q), BF16),
        compiler_params=_params(2),
        name="lat_attn",
    )(sink, q, kab, kab, kab, vab, vab, vab, kab_ctx, vab_ctx)


def _proj_kernel(x_ref, ap_ref, al_ref, mod_ref, lng_ref, lnb_ref, wo_ref, o_ref):
    f = jnp.dot(_read_split(ap_ref, al_ref), wo_ref[...], preferred_element_type=F32)
    gate = mod_ref[5:6, :]
    y = DEEPNORM_ALPHA * x_ref[...] + gate * f
    o_ref[...] = _layer_norm(y, lng_ref[...], lnb_ref[...])


def _proj(x, attn_prompt, attn_latent, mods, ln_g, ln_b, w_o):
    return pl.pallas_call(
        _proj_kernel,
        grid=(N_TOK // TM,),
        in_specs=[
            _slab_spec(D_MODEL),
            _prompt_spec(D_Q),
            _latent_spec(D_Q),
            _mod_spec(TM, 0)] + _ln_specs(0, 1) + [
            _resident((D_Q, D_MODEL)),
        ],
        out_specs=_slab_spec(D_MODEL),
        out_shape=jax.ShapeDtypeStruct((N_TOK, D_MODEL), F32),
        compiler_params=_params(),
        name="attn_proj",
    )(x, attn_prompt, attn_latent, mods, ln_g, ln_b, w_o)


PROMPT_POOL_TILES = N_PROMPT_TOK // POOL_TM
LATENT_POOL_TILES = DEC_SEQ // POOL_TM


def _pool_kernel(x_ref, xp_ref, xn_ref, mod_ref, lng_ref, lnb_ref, w_ref, sc_ref, o_ref, h_ref):
    i = pl.program_id(0)
    in_seq = (i - PROMPT_POOL_TILES) % LATENT_POOL_TILES
    is_start = (i < PROMPT_POOL_TILES) | (in_seq == 0)
    is_end = (i < PROMPT_POOL_TILES) | (in_seq == LATENT_POOL_TILES - 1)

    x = x_ref[...]
    h = _modulate(x, mod_ref, 3)
    h_ref[POOL_HALO:POOL_HALO + POOL_TM, :] = h
    h_ref[0:POOL_HALO, :] = jnp.where(is_start, 0.0, _modulate(xp_ref[...], mod_ref, 3))
    h_ref[POOL_HALO + POOL_TM:, :] = jnp.where(is_end, 0.0, _modulate(xn_ref[...], mod_ref, 3))

    r = lax.broadcasted_iota(jnp.int32, (POOL_TM, POOL_GROUP_DIM), 0)
    gate = mod_ref[5:6, :]
    outs = []
    for gi, w in enumerate(POOL_WINDOWS):
        half = w // 2
        cols = slice(gi * POOL_GROUP_DIM, (gi + 1) * POOL_GROUP_DIM)
        total = h_ref[POOL_HALO - half:POOL_HALO - half + POOL_TM, cols]
        for d in range(-half + 1, half):
            total = total + h_ref[POOL_HALO + d:POOL_HALO + d + POOL_TM, cols]
        clipped_lo = jnp.where(is_start, jnp.maximum(half - r, 0), 0)
        clipped_hi = jnp.where(is_end, jnp.maximum(r + half - POOL_TM, 0), 0)
        cnt = (w - clipped_lo - clipped_hi).astype(F32)
        pooled = (total / cnt - h[:, cols]).astype(BF16)
        outs.append(jnp.dot(pooled, w_ref[gi], preferred_element_type=F32))
    mixed = jnp.concatenate(outs, axis=-1) * sc_ref[...]
    y = DEEPNORM_ALPHA * x + gate * mixed
    o_ref[...] = _layer_norm(y, lng_ref[...], lnb_ref[...])


def _pool(x, mods, ln_g, ln_b, w_pool, scale):
    halo_per_tile = POOL_TM // POOL_HALO
    last_halo = N_TOK // POOL_HALO - 1
    return pl.pallas_call(
        _pool_kernel,
        grid=(N_TOK // POOL_TM,),
        in_specs=[
            pl.BlockSpec((POOL_TM, D_MODEL), lambda i: (i, 0)),
            pl.BlockSpec((POOL_HALO, D_MODEL), lambda i: (jnp.maximum(i * halo_per_tile - 1, 0), 0)),
            pl.BlockSpec((POOL_HALO, D_MODEL),
                         lambda i: (jnp.minimum((i + 1) * halo_per_tile, last_halo), 0)),
            _mod_spec(POOL_TM, 1)] + _ln_specs(1, 1) + [
            _resident((len(POOL_WINDOWS), POOL_GROUP_DIM, POOL_GROUP_DIM)),
            _resident((1, D_MODEL)),
        ],
        out_specs=pl.BlockSpec((POOL_TM, D_MODEL), lambda i: (i, 0)),
        out_shape=jax.ShapeDtypeStruct((N_TOK, D_MODEL), F32),
        scratch_shapes=[pltpu.VMEM((POOL_TM + 2 * POOL_HALO, D_MODEL), F32)],
        compiler_params=_params(),
        name="pool",
    )(x, x, x, mods, ln_g, ln_b, w_pool, scale.reshape(1, D_MODEL))


def _ab_layout(a):
    lead = a.shape[:-1]
    a = a.reshape(lead + (N_KV_HEADS, 1, HEAD_DIM)).astype(BF16)
    z = jnp.zeros_like(a)
    return jnp.concatenate([a, z, z, a], axis=-2).reshape(lead + (D_KV_AB,))


def _rope_tables():
    n_rows = DEC_SEQ // GRID_W
    rows = jnp.repeat(jnp.arange(n_rows, dtype=F32), GRID_W)
    cols = jnp.tile(jnp.arange(GRID_W, dtype=F32), n_rows)
    inv = jnp.power(ROPE_BASE, -jnp.arange(N_FREQ, dtype=F32) / N_FREQ)
    ang_r = rows[:, None] * inv
    ang_c = cols[:, None] * inv
    ang = jnp.concatenate([ang_r, ang_r, ang_c, ang_c], axis=-1)
    cos = jnp.tile(jnp.cos(ang), (1, LANES // HEAD_DIM))
    sin = jnp.tile(jnp.sin(ang), (1, LANES // HEAD_DIM))
    first_half = (jnp.arange(LANES) % 32) < 16
    s_up = jnp.where(first_half, -sin, 0.0)
    s_dn = jnp.where(first_half, 0.0, sin)
    ident = jnp.ones((TM, LANES), F32)
    zeros = jnp.zeros((TM, LANES), F32)
    return (jnp.concatenate([cos, ident], axis=0),
            jnp.concatenate([s_up, zeros], axis=0),
            jnp.concatenate([s_dn, zeros], axis=0))


def kernel(x_prompt, x_sample, cache_k, cache_v, c, c_ctx, w_mod, b_mod, ln_g, ln_b,
           ffn_w_gate, ffn_w_up, ffn_w_down, attn_w_qkv, attn_w_o, attn_sink,
           pool_w, pool_scale):
    cond = jnp.concatenate(
        [c_ctx[None, :], c, jnp.zeros((COND_ROWS - N_COND, D_MODEL), F32)], axis=0)
    mods = _adaln(cond, w_mod, b_mod)

    ffn_w = (ffn_w_gate, ffn_w_up, ffn_w_down)
    ln_g = ln_g.reshape(DEPTH * 3, 1, D_MODEL)
    ln_b = ln_b.reshape(DEPTH * 3, 1, D_MODEL)
    ffn = functools.partial(_ffn, mods=mods, ln_g=ln_g, ln_b=ln_b)

    w_first = tuple(w[0, 0].astype(BF16) for w in ffn_w)
    x, w_next = ffn((x_prompt.reshape(N_PROMPT_TOK, D_MODEL), x_sample.reshape(N_LATENT_TOK, D_MODEL)),
                    weights=w_first, layer=0, which=0, next_f32=(ffn_w, (0, 1)))
    cos_t, sup_t, sdn_t = _rope_tables()
    q, kab, vab, k_state, v_state = _qkv(x, mods, attn_w_qkv[0].astype(BF16), cos_t, sup_t, sdn_t)
    sink = attn_sink[0]
    o_ctx = _ctx_attn(sink, q, kab, vab)
    kab_ctx = _ab_layout(cache_k[:, 0].reshape(DEC_BATCH, PAST_LEN, D_KV))
    vab_ctx = _ab_layout(cache_v[:, 0].reshape(DEC_BATCH, PAST_LEN, D_KV))
    o_lat = _lat_attn(sink, q, kab, vab, kab_ctx, vab_ctx)
    x = _proj(x, o_ctx, o_lat, mods, ln_g, ln_b, attn_w_o[0].astype(BF16))
    x, w_next = ffn((x,), weights=w_next, layer=0, which=1, next_f32=(ffn_w, (1, 0)))

    x, w_next = ffn((x,), weights=w_next, layer=1, which=0, next_f32=(ffn_w, (1, 1)))
    x = _pool(x, mods, ln_g, ln_b, pool_w[0].astype(BF16), pool_scale[0])
    (y_prompt, y_sample), _ = ffn((x,), weights=w_next, layer=1, which=1, split_out=True)

    def state(s_t):
        s_t = s_t.reshape(BATCH, N_KV_HEADS, HEAD_DIM, SEQ)
        return jnp.transpose(s_t, (0, 3, 1, 2))[:, None]

    return (y_prompt.reshape(BATCH, SEQ, D_MODEL),
            y_sample.reshape(DEC_BATCH, DEC_SEQ, D_MODEL),
            state(k_state), state(v_state))
```

```python
import functools

import jax
import jax.numpy as jnp
from jax import lax
from jax.experimental import pallas as pl
from jax.experimental.pallas import tpu as pltpu

D_MODEL = 1024
BATCH = 16
SEQ = 256
DEPTH = 2
DEC_BATCH = 2
DEC_SEQ = 4096
PAST_LEN = 256
GRID_W = 64
N_HEADS = 16
N_KV_HEADS = 4
HEAD_DIM = 64
WINDOW = 128
BLOCK = 128
ROPE_BASE = 10000.0
N_FREQ = HEAD_DIM // 4
POOL_WINDOWS = (2, 4, 8, 16)
POOL_GROUP_DIM = D_MODEL // 4
D_FF = 2816
N_MOD = 9
LN_EPS = 1e-5
DEEPNORM_ALPHA = (2.0 * DEPTH) ** 0.25
ATTN_SCALE = HEAD_DIM ** -0.5
NEG_INF = -1e30

N_PROMPT_TOK = BATCH * SEQ
N_LATENT_TOK = DEC_BATCH * DEC_SEQ
N_TOK = N_PROMPT_TOK + N_LATENT_TOK
MOD_GROUP_ROWS = 4096
N_COND = 1 + DEC_BATCH
COND_ROWS = 8

LANES = 128
HALF_LANES = LANES // 2
TM = 1024
N_PROMPT_TILES = N_PROMPT_TOK // TM
FF_CHUNK = 256
POOL_TM = 256
POOL_HALO = 8
D_Q = N_HEADS * HEAD_DIM
D_KV = N_KV_HEADS * HEAD_DIM
D_KV_AB = N_KV_HEADS * 2 * LANES
VMEM_LIMIT = 40 * 1024 * 1024
FFN_VMEM_LIMIT = 55 * 1024 * 1024

F32 = jnp.float32
BF16 = jnp.bfloat16


def _layer_norm(y, g, b):
    mu = jnp.mean(y, axis=-1, keepdims=True)
    yc = y - mu
    var = jnp.mean(yc * yc, axis=-1, keepdims=True)
    return yc * lax.rsqrt(var + LN_EPS) * g + b


def _modulate(x, mod_ref, row0):
    shift = mod_ref[row0:row0 + 1, :]
    scale = mod_ref[row0 + 1:row0 + 2, :]
    return x * (1.0 + scale) + shift


def _params(n_axes=1, vmem_limit=VMEM_LIMIT):
    return pltpu.CompilerParams(
        dimension_semantics=("arbitrary",) * n_axes,
        vmem_limit_bytes=vmem_limit)


def _resident(shape):
    nd = len(shape)
    return pl.BlockSpec(shape, lambda *_: (0,) * nd, pipeline_mode=pl.Buffered(1))


def _resident_at(index, tail):
    return pl.BlockSpec((None,) * len(index) + tuple(tail),
                        lambda *_: tuple(index) + (0,) * len(tail), pipeline_mode=pl.Buffered(1))


def _mod_spec(tile_rows, layer):
    tiles_per_group = MOD_GROUP_ROWS // tile_rows
    return pl.BlockSpec((None, None, N_MOD, D_MODEL),
                        lambda i: (layer, i // tiles_per_group, 0, 0))


def _ln_specs(layer, which):
    return [_resident_at((layer * 3 + which,), (1, D_MODEL))] * 2


def _slab_spec(cols, tm=TM):
    return pl.BlockSpec((tm, cols), lambda i: (i, 0))


def _prompt_spec(cols, tm=TM):
    return pl.BlockSpec((tm, cols), lambda i: (jnp.minimum(i, N_PROMPT_TOK // tm - 1), 0))


def _latent_spec(cols, tm=TM):
    return pl.BlockSpec((tm, cols), lambda i: (jnp.maximum(i - N_PROMPT_TOK // tm, 0), 0))


def _read_split(p_ref, l_ref, n_prompt_tiles=N_PROMPT_TILES):
    return jnp.where(pl.program_id(0) < n_prompt_tiles, p_ref[...], l_ref[...])


ADALN_NC = 1024


def _adaln_kernel(cond_ref, w_ref, b_ref, o_ref):
    c = cond_ref[...]
    s = (c * (1.0 / (1.0 + jnp.exp(-c)))).astype(BF16)
    o_ref[0] = jnp.dot(s, w_ref[0].astype(BF16), preferred_element_type=F32) + b_ref[0]


def _adaln(cond, w_mod, b_mod):
    n_out = N_MOD * D_MODEL
    out = pl.pallas_call(
        _adaln_kernel,
        grid=(DEPTH, n_out // ADALN_NC),
        in_specs=[
            pl.BlockSpec((COND_ROWS, D_MODEL), lambda l, n: (0, 0)),
            pl.BlockSpec((1, D_MODEL, ADALN_NC), lambda l, n: (l, 0, n)),
            pl.BlockSpec((1, 1, ADALN_NC), lambda l, n: (l, 0, n)),
        ],
        out_specs=pl.BlockSpec((1, COND_ROWS, ADALN_NC), lambda l, n: (l, 0, n)),
        out_shape=jax.ShapeDtypeStruct((DEPTH, COND_ROWS, n_out), F32),
        compiler_params=_params(2),
        name="adaln",
    )(cond, w_mod, b_mod.reshape(DEPTH, 1, n_out))
    return out[:, :N_COND].reshape(DEPTH, N_COND, N_MOD, D_MODEL)


def _ffn_body(x, mod_ref, lng_ref, lnb_ref, wg_ref, wu_ref, wd_ref, a_ref, row0):
    h = _modulate(x, mod_ref, row0).astype(BF16)
    for c in range(D_FF // FF_CHUNK):
        sl = slice(c * FF_CHUNK, (c + 1) * FF_CHUNK)
        g = jnp.dot(h, wg_ref[:, sl], preferred_element_type=F32)
        u = jnp.dot(h, wu_ref[:, sl], preferred_element_type=F32)
        a_ref[:, sl] = (g * (1.0 / (1.0 + jnp.exp(-g))) * u).astype(BF16)
    f = jnp.dot(a_ref[...], wd_ref[...], preferred_element_type=F32)
    gate = mod_ref[row0 + 2:row0 + 3, :]
    y = DEEPNORM_ALPHA * x + (0.5 * gate) * f
    return _layer_norm(y, lng_ref[...], lnb_ref[...])


CAST_STEPS = 8
FFN_WEIGHT_SHAPES = ((D_MODEL, D_FF), (D_MODEL, D_FF), (D_FF, D_MODEL))


def _ffn_kernel(*refs, row0, n_prompt_tiles, split_in, split_out, cast_next):
    n_x = 2 if split_in else 1
    n_cast = 3 if cast_next else 0
    x_refs, (mod_ref, lng_ref, lnb_ref, wg_ref, wu_ref, wd_ref) = refs[:n_x], refs[n_x:n_x + 6]
    next_f32 = refs[n_x + 6:n_x + 6 + n_cast]
    o_refs, next_bf16, a_ref = refs[n_x + 6 + n_cast:-1 - n_cast], refs[-1 - n_cast:-1], refs[-1]
    step = pl.program_id(0)
    x = _read_split(*x_refs, n_prompt_tiles) if split_in else x_refs[0][...]
    out = _ffn_body(x, mod_ref, lng_ref, lnb_ref, wg_ref, wu_ref, wd_ref, a_ref, row0)
    if split_out:
        @pl.when(step < n_prompt_tiles)
        def _():
            o_refs[0][...] = out

        @pl.when(step >= n_prompt_tiles)
        def _():
            o_refs[1][...] = out
    else:
        o_refs[0][...] = out
    if cast_next:
        @pl.when(step < CAST_STEPS)
        def _():
            for src, dst in zip(next_f32, next_bf16):
                dst[...] = src[...].astype(BF16)


def _cast_block_spec(shape, lead):
    block = (None,) * len(lead) + (shape[0] // CAST_STEPS, shape[1])
    return pl.BlockSpec(block, lambda i: tuple(lead) + (jnp.minimum(i, CAST_STEPS - 1), 0))


def _ffn(xs, mods, ln_g, ln_b, weights, layer, which, next_f32=None, split_out=False):
    split_in = len(xs) == 2
    cast_next = next_f32 is not None
    row0 = 6 * which
    ln_row = 2 * which
    tm = TM // 2 if split_in and cast_next else TM
    x_specs = ([_prompt_spec(D_MODEL, tm), _latent_spec(D_MODEL, tm)] if split_in
               else [_slab_spec(D_MODEL, tm)])
    if split_out:
        out_specs = [_prompt_spec(D_MODEL, tm), _latent_spec(D_MODEL, tm)]
        out_shape = [jax.ShapeDtypeStruct((N_PROMPT_TOK, D_MODEL), F32),
                     jax.ShapeDtypeStruct((N_LATENT_TOK, D_MODEL), F32)]
    else:
        out_specs = [_slab_spec(D_MODEL, tm)]
        out_shape = [jax.ShapeDtypeStruct((N_TOK, D_MODEL), F32)]
    n_x_out = len(out_specs)
    cast_in_specs, cast_args = [], ()
    if cast_next:
        cast_args, lead = next_f32
        cast_in_specs = [_cast_block_spec(s, lead) for s in FFN_WEIGHT_SHAPES]
        out_specs = out_specs + [_cast_block_spec(s, ()) for s in FFN_WEIGHT_SHAPES]
        out_shape = out_shape + [jax.ShapeDtypeStruct(s, BF16) for s in FFN_WEIGHT_SHAPES]
    outs = pl.pallas_call(
        functools.partial(_ffn_kernel, row0=row0, n_prompt_tiles=N_PROMPT_TOK // tm,
                          split_in=split_in, split_out=split_out, cast_next=cast_next),
        grid=(N_TOK // tm,),
        in_specs=x_specs + [_mod_spec(tm, layer)] + _ln_specs(layer, ln_row)
        + [_resident(s) for s in FFN_WEIGHT_SHAPES] + cast_in_specs,
        out_specs=out_specs,
        out_shape=out_shape,
        scratch_shapes=[pltpu.VMEM((tm, D_FF), BF16)],
        compiler_params=_params(vmem_limit=FFN_VMEM_LIMIT),
        name="ffn",
    )(*xs, mods, ln_g, ln_b, *weights, *cast_args)
    x_out = tuple(outs[:n_x_out]) if split_out else outs[0]
    return x_out, (tuple(outs[n_x_out:]) if cast_next else None)


ROPE_TILES = DEC_SEQ // TM


def _write_ab(dst_ref, pair, src):
    lo = lax.broadcasted_iota(jnp.int32, src.shape, 1) < HALF_LANES
    swapped = pltpu.roll(src, HALF_LANES, 1)
    blocks = (jnp.where(lo, src, 0.0), jnp.where(lo, 0.0, swapped),
              jnp.where(lo, swapped, 0.0), jnp.where(lo, 0.0, src))
    for n, blk in enumerate(blocks):
        c0 = (4 * pair + n) * LANES
        dst_ref[:, c0:c0 + LANES] = blk.astype(BF16)


def _qkv_kernel(x_ref, mod_ref, w_ref, cos_ref, sup_ref, sdn_ref,
                q_ref, kab_ref, vab_ref, ks_ref, vs_ref):
    h = _modulate(x_ref[...], mod_ref, 3).astype(BF16)
    qkv = jnp.dot(h, w_ref[...], preferred_element_type=F32)
    cos = cos_ref[...]
    s_up = sup_ref[...]
    s_dn = sdn_ref[...]

    def rope(blk):
        up = pltpu.roll(blk, LANES - 16, 1)
        dn = pltpu.roll(blk, 16, 1)
        return blk * cos + up * s_up + dn * s_dn

    for j in range(D_Q // LANES):
        cols = slice(j * LANES, (j + 1) * LANES)
        q_ref[:, cols] = (rope(qkv[:, cols]) * ATTN_SCALE).astype(BF16)
    for pair in range(D_KV // LANES):
        k_cols = slice(D_Q + pair * LANES, D_Q + (pair + 1) * LANES)
        v_cols = slice(D_Q + D_KV + pair * LANES, D_Q + D_KV + (pair + 1) * LANES)
        _write_ab(kab_ref, pair, rope(qkv[:, k_cols]))
        _write_ab(vab_ref, pair, qkv[:, v_cols])

    @pl.when(pl.program_id(0) < N_PROMPT_TILES)
    def _():
        k_t = qkv[:, D_Q:D_Q + D_KV].T
        v_t = qkv[:, D_Q + D_KV:].T
        for s in range(TM // SEQ):
            ks_ref[s] = k_t[:, s * SEQ:(s + 1) * SEQ]
            vs_ref[s] = v_t[:, s * SEQ:(s + 1) * SEQ]


def _rope_index(i):
    return (jnp.where(i < N_PROMPT_TILES, ROPE_TILES, (i - N_PROMPT_TILES) % ROPE_TILES), 0)


def _qkv(x, mods, w, cos_t, sup_t, sdn_t):
    seqs_per_tile = TM // SEQ
    state_spec = pl.BlockSpec((seqs_per_tile, D_KV, SEQ),
                              lambda i: (jnp.minimum(i, N_PROMPT_TILES - 1), 0, 0))
    return pl.pallas_call(
        _qkv_kernel,
        grid=(N_TOK // TM,),
        in_specs=[
            _slab_spec(D_MODEL),
            _mod_spec(TM, 0),
            _resident((D_MODEL, D_Q + 2 * D_KV)),
            pl.BlockSpec((TM, LANES), _rope_index),
            pl.BlockSpec((TM, LANES), _rope_index),
            pl.BlockSpec((TM, LANES), _rope_index),
        ],
        out_specs=[
            _slab_spec(D_Q), _slab_spec(D_KV_AB), _slab_spec(D_KV_AB), state_spec, state_spec,
        ],
        out_shape=[
            jax.ShapeDtypeStruct((N_TOK, D_Q), BF16),
            jax.ShapeDtypeStruct((N_TOK, D_KV_AB), BF16),
            jax.ShapeDtypeStruct((N_TOK, D_KV_AB), BF16),
            jax.ShapeDtypeStruct((BATCH, D_KV, SEQ), F32),
            jax.ShapeDtypeStruct((BATCH, D_KV, SEQ), F32),
        ],
        compiler_params=_params(),
        name="qkv",
    )(x, mods, w, cos_t, sup_t, sdn_t)


def _attend(q_ref, o_ref, sink_ref, k_slabs, v_slabs, masks, tq):
    nt = (((1,), (1,)), ((), ()))
    row_hi = lax.broadcasted_iota(jnp.int32, (2 * tq, 1), 0) >= tq
    lane_lo = lax.broadcasted_iota(jnp.int32, (2 * tq, LANES), 1) < HALF_LANES
    for kh in range(N_KV_HEADS):
        ks = k_slabs(kh)
        vs = v_slabs(kh)
        k_cat = jnp.concatenate([s[:, :LANES] for s in ks] + [s[:, LANES:] for s in ks], axis=0)
        v_cat = jnp.concatenate([s[:, :LANES] for s in vs] + [s[:, LANES:] for s in vs], axis=0)
        n_keys = k_cat.shape[0] // 2
        j0 = 2 * kh
        q2 = jnp.concatenate([q_ref[:, j0 * LANES:(j0 + 1) * LANES],
                              q_ref[:, (j0 + 1) * LANES:(j0 + 2) * LANES]], axis=0)
        s = lax.dot_general(q2, k_cat, nt, preferred_element_type=F32)
        es, inv_dens = [], []
        for half in range(2):
            segs, off = [], half * n_keys
            for slab, m in zip(ks, masks):
                seg = s[:, off:off + slab.shape[0]]
                segs.append(seg if m is None else jnp.where(m, seg, NEG_INF))
                off += slab.shape[0]
            logits = jnp.concatenate(segs, axis=1)
            sink = jnp.where(row_hi, sink_ref[4 * kh + 2 + half], sink_ref[4 * kh + half])
            m_row = jnp.maximum(jnp.max(logits, axis=-1, keepdims=True), sink)
            e = jnp.exp(logits - m_row)
            den = jnp.sum(e, axis=-1, keepdims=True) + jnp.exp(sink - m_row)
            es.append(e.astype(BF16))
            inv_dens.append(1.0 / den)
        p = jnp.concatenate(es, axis=1)
        o2 = jnp.dot(p, v_cat, preferred_element_type=F32)
        o2 = o2 * jnp.where(lane_lo, inv_dens[0], inv_dens[1])
        o_ref[:, j0 * LANES:(j0 + 1) * LANES] = o2[:tq].astype(BF16)
        o_ref[:, (j0 + 1) * LANES:(j0 + 2) * LANES] = o2[tq:].astype(BF16)


def _kv_cols(kh):
    return slice(kh * 2 * LANES, (kh + 1) * 2 * LANES)


def _ctx_attn_kernel(sink_ref, q_ref, k_ref, v_ref, o_ref):
    _attend(q_ref, o_ref, sink_ref,
            lambda kh: [k_ref[:, _kv_cols(kh)]],
            lambda kh: [v_ref[:, _kv_cols(kh)]],
            [None], SEQ)


def _ctx_attn(sink, q, kab, vab):
    return pl.pallas_call(
        _ctx_attn_kernel,
        grid=(BATCH,),
        in_specs=[
            pl.BlockSpec(memory_space=pltpu.SMEM),
            pl.BlockSpec((SEQ, D_Q), lambda b: (b, 0)),
            pl.BlockSpec((SEQ, D_KV_AB), lambda b: (b, 0)),
            pl.BlockSpec((SEQ, D_KV_AB), lambda b: (b, 0)),
        ],
        out_specs=pl.BlockSpec((SEQ, D_Q), lambda b: (b, 0)),
        out_shape=jax.ShapeDtypeStruct((N_PROMPT_TOK, D_Q), BF16),
        compiler_params=_params(),
        name="ctx_attn",
    )(sink, q, kab, vab)


N_QBLK = DEC_SEQ // BLOCK


def _lat_attn_kernel(sink_ref, q_ref, kp_ref, kc_ref, kn_ref, vp_ref, vc_ref, vn_ref,
                     kx_ref, vx_ref, o_ref):
    qb = pl.program_id(1)
    r = lax.broadcasted_iota(jnp.int32, (2 * BLOCK, BLOCK), 0) & (BLOCK - 1)
    c = lax.broadcasted_iota(jnp.int32, (2 * BLOCK, BLOCK), 1)
    m_prev = c >= r + jnp.where(qb > 0, 0, BLOCK)
    m_next = c <= r - jnp.where(qb < N_QBLK - 1, 0, BLOCK)

    def slabs(p_ref, c_ref, n_ref, x_ref):
        return lambda kh: [p_ref[:, _kv_cols(kh)], c_ref[:, _kv_cols(kh)],
                           n_ref[:, _kv_cols(kh)], x_ref[0, :, _kv_cols(kh)]]

    _attend(q_ref, o_ref, sink_ref,
            slabs(kp_ref, kc_ref, kn_ref, kx_ref), slabs(vp_ref, vc_ref, vn_ref, vx_ref),
            [m_prev, None, m_next, None], BLOCK)


def _lat_attn(sink, q, kab, vab, kab_ctx, vab_ctx):
    first = N_PROMPT_TOK // BLOCK

    def cur(b, i):
        return (first + b * N_QBLK + i, 0)

    def prev(b, i):
        return (first + b * N_QBLK + jnp.maximum(i - 1, 0), 0)

    def nxt(b, i):
        return (first + b * N_QBLK + jnp.minimum(i + 1, N_QBLK - 1), 0)

    kv_spec = lambda f: pl.BlockSpec((BLOCK, D_KV_AB), f)
    ctx_spec = pl.BlockSpec((1, PAST_LEN, D_KV_AB), lambda b, i: (b, 0, 0))
    return pl.pallas_call(
        _lat_attn_kernel,
        grid=(DEC_BATCH, N_QBLK),
        in_specs=[
            pl.BlockSpec(memory_space=pltpu.SMEM),
            pl.BlockSpec((BLOCK, D_Q), cur),
            kv_spec(prev), kv_spec(cur), kv_spec(nxt),
            kv_spec(prev), kv_spec(cur), kv_spec(nxt),
            ctx_spec, ctx_spec,
        ],
        out_specs=pl.BlockSpec((BLOCK, D_Q), lambda b, i: (b * N_QBLK + i, 0)),
        out_shape=jax.ShapeDtypeStruct((N_LATENT_TOK, D_Q), BF16),
        compiler_params=_params(2),
        name="lat_attn",
    )(sink, q, kab, kab, kab, vab, vab, vab, kab_ctx, vab_ctx)


def _proj_kernel(x_ref, ap_ref, al_ref, mod_ref, lng_ref, lnb_ref, wo_ref, o_ref):
    f = jnp.dot(_read_split(ap_ref, al_ref), wo_ref[...], preferred_element_type=F32)
    gate = mod_ref[5:6, :]
    y = DEEPNORM_ALPHA * x_ref[...] + gate * f
    o_ref[...] = _layer_norm(y, lng_ref[...], lnb_ref[...])


def _proj(x, attn_prompt, attn_latent, mods, ln_g, ln_b, w_o):
    return pl.pallas_call(
        _proj_kernel,
        grid=(N_TOK // TM,),
        in_specs=[
            _slab_spec(D_MODEL),
            _prompt_spec(D_Q),
            _latent_spec(D_Q),
            _mod_spec(TM, 0)] + _ln_specs(0, 1) + [
            _resident((D_Q, D_MODEL)),
        ],
        out_specs=_slab_spec(D_MODEL),
        out_shape=jax.ShapeDtypeStruct((N_TOK, D_MODEL), F32),
        compiler_params=_params(),
        name="attn_proj",
    )(x, attn_prompt, attn_latent, mods, ln_g, ln_b, w_o)


PROMPT_POOL_TILES = N_PROMPT_TOK // POOL_TM
LATENT_POOL_TILES = DEC_SEQ // POOL_TM


def _pool_kernel(x_ref, xp_ref, xn_ref, mod_ref, lng_ref, lnb_ref, w_ref, sc_ref, o_ref):
    i = pl.program_id(0)
    in_seq = (i - PROMPT_POOL_TILES) % LATENT_POOL_TILES
    is_start = (i < PROMPT_POOL_TILES) | (in_seq == 0)
    is_end = (i < PROMPT_POOL_TILES) | (in_seq == LATENT_POOL_TILES - 1)

    x = x_ref[...]
    h = _modulate(x, mod_ref, 3)
    h_ext = jnp.concatenate([
        jnp.where(is_start, 0.0, _modulate(xp_ref[...], mod_ref, 3)), h,
        jnp.where(is_end, 0.0, _modulate(xn_ref[...], mod_ref, 3))], axis=0)
    n_ext = POOL_TM + 2 * POOL_HALO

    r8 = lax.broadcasted_iota(jnp.int32, (POOL_HALO, POOL_GROUP_DIM), 0)
    gate = mod_ref[5:6, :]
    outs = []
    for gi, w in enumerate(POOL_WINDOWS):
        half = w // 2
        cols = slice(gi * POOL_GROUP_DIM, (gi + 1) * POOL_GROUP_DIM)
        acc = h_ext[:, cols]
        s = 1
        while s < w:
            acc = acc + pltpu.roll(acc, s, 0)
            s *= 2
        if half > 1:
            acc = pltpu.roll(acc, n_ext - (half - 1), 0)
        total = acc[POOL_HALO:POOL_HALO + POOL_TM]
        cnt_top = w - jnp.where(is_start, jnp.maximum(half - r8, 0), 0)
        cnt_bot = w - jnp.where(is_end, jnp.maximum(r8 + half - POOL_HALO, 0), 0)
        inv_cnt = jnp.concatenate([
            1.0 / cnt_top.astype(F32),
            jnp.full((POOL_TM - 2 * POOL_HALO, POOL_GROUP_DIM), 1.0 / w, F32),
            1.0 / cnt_bot.astype(F32)], axis=0)
        pooled = (total * inv_cnt - h[:, cols]).astype(BF16)
        outs.append(jnp.dot(pooled, w_ref[gi], preferred_element_type=F32))
    y = DEEPNORM_ALPHA * x + (gate * sc_ref[...]) * jnp.concatenate(outs, axis=-1)
    o_ref[...] = _layer_norm(y, lng_ref[...], lnb_ref[...])


def _pool(x, mods, ln_g, ln_b, w_pool, scale):
    halo_per_tile = POOL_TM // POOL_HALO
    last_halo = N_TOK // POOL_HALO - 1
    return pl.pallas_call(
        _pool_kernel,
        grid=(N_TOK // POOL_TM,),
        in_specs=[
            pl.BlockSpec((POOL_TM, D_MODEL), lambda i: (i, 0)),
            pl.BlockSpec((POOL_HALO, D_MODEL), lambda i: (jnp.maximum(i * halo_per_tile - 1, 0), 0)),
            pl.BlockSpec((POOL_HALO, D_MODEL),
                         lambda i: (jnp.minimum((i + 1) * halo_per_tile, last_halo), 0)),
            _mod_spec(POOL_TM, 1)] + _ln_specs(1, 1) + [
            _resident((len(POOL_WINDOWS), POOL_GROUP_DIM, POOL_GROUP_DIM)),
            _resident((1, D_MODEL)),
        ],
        out_specs=pl.BlockSpec((POOL_TM, D_MODEL), lambda i: (i, 0)),
        out_shape=jax.ShapeDtypeStruct((N_TOK, D_MODEL), F32),
        compiler_params=_params(),
        name="pool",
    )(x, x, x, mods, ln_g, ln_b, w_pool, scale.reshape(1, D_MODEL))


def _ab_layout(a):
    lead = a.shape[:-1]
    a = a.reshape(lead + (N_KV_HEADS, 1, HEAD_DIM)).astype(BF16)
    z = jnp.zeros_like(a)
    return jnp.concatenate([a, z, z, a], axis=-2).reshape(lead + (D_KV_AB,))


def _rope_tables():
    n_rows = DEC_SEQ // GRID_W
    rows = jnp.repeat(jnp.arange(n_rows, dtype=F32), GRID_W)
    cols = jnp.tile(jnp.arange(GRID_W, dtype=F32), n_rows)
    inv = jnp.power(ROPE_BASE, -jnp.arange(N_FREQ, dtype=F32) / N_FREQ)
    ang_r = rows[:, None] * inv
    ang_c = cols[:, None] * inv
    ang = jnp.concatenate([ang_r, ang_r, ang_c, ang_c], axis=-1)
    cos = jnp.tile(jnp.cos(ang), (1, LANES // HEAD_DIM))
    sin = jnp.tile(jnp.sin(ang), (1, LANES // HEAD_DIM))
    first_half = (jnp.arange(LANES) % 32) < 16
    s_up = jnp.where(first_half, -sin, 0.0)
    s_dn = jnp.where(first_half, 0.0, sin)
    ident = jnp.ones((TM, LANES), F32)
    zeros = jnp.zeros((TM, LANES), F32)
    return (jnp.concatenate([cos, ident], axis=0),
            jnp.concatenate([s_up, zeros], axis=0),
            jnp.concatenate([s_dn, zeros], axis=0))


def kernel(x_prompt, x_sample, cache_k, cache_v, c, c_ctx, w_mod, b_mod, ln_g, ln_b,
           ffn_w_gate, ffn_w_up, ffn_w_down, attn_w_qkv, attn_w_o, attn_sink,
           pool_w, pool_scale):
    cond = jnp.concatenate(
        [c_ctx[None, :], c, jnp.zeros((COND_ROWS - N_COND, D_MODEL), F32)], axis=0)
    mods = _adaln(cond, w_mod, b_mod)

    ffn_w = (ffn_w_gate, ffn_w_up, ffn_w_down)
    ln_g = ln_g.reshape(DEPTH * 3, 1, D_MODEL)
    ln_b = ln_b.reshape(DEPTH * 3, 1, D_MODEL)
    ffn = functools.partial(_ffn, mods=mods, ln_g=ln_g, ln_b=ln_b)

    w_first = tuple(w[0, 0].astype(BF16) for w in ffn_w)
    x, w_next = ffn((x_prompt.reshape(N_PROMPT_TOK, D_MODEL), x_sample.reshape(N_LATENT_TOK, D_MODEL)),
                    weights=w_first, layer=0, which=0, next_f32=(ffn_w, (0, 1)))
    cos_t, sup_t, sdn_t = _rope_tables()
    q, kab, vab, k_state, v_state = _qkv(x, mods, attn_w_qkv[0].astype(BF16), cos_t, sup_t, sdn_t)
    sink = attn_sink[0]
    o_ctx = _ctx_attn(sink, q, kab, vab)
    kab_ctx = _ab_layout(cache_k[:, 0].reshape(DEC_BATCH, PAST_LEN, D_KV))
    vab_ctx = _ab_layout(cache_v[:, 0].reshape(DEC_BATCH, PAST_LEN, D_KV))
    o_lat = _lat_attn(sink, q, kab, vab, kab_ctx, vab_ctx)
    x = _proj(x, o_ctx, o_lat, mods, ln_g, ln_b, attn_w_o[0].astype(BF16))
    x, w_next = ffn((x,), weights=w_next, layer=0, which=1, next_f32=(ffn_w, (1, 0)))

    x, w_next = ffn((x,), weights=w_next, layer=1, which=0, next_f32=(ffn_w, (1, 1)))
    x = _pool(x, mods, ln_g, ln_b, pool_w[0].astype(BF16), pool_scale[0])
    (y_prompt, y_sample), _ = ffn((x,), weights=w_next, layer=1, which=1, split_out=True)

    def state(s_t):
        s_t = s_t.reshape(BATCH, N_KV_HEADS, HEAD_DIM, SEQ)
        return jnp.transpose(s_t, (0, 3, 1, 2))[:, None]

    return (y_prompt.reshape(BATCH, SEQ, D_MODEL),
            y_sample.reshape(DEC_BATCH, DEC_SEQ, D_MODEL),
            state(k_state), state(v_state))
```

```python
import functools

import jax
import jax.numpy as jnp
from jax import lax
from jax.experimental import pallas as pl
from jax.experimental.pallas import tpu as pltpu

D_MODEL = 1024
BATCH = 16
SEQ = 256
DEPTH = 2
DEC_BATCH = 2
DEC_SEQ = 4096
PAST_LEN = 256
GRID_W = 64
N_HEADS = 16
N_KV_HEADS = 4
HEAD_DIM = 64
WINDOW = 128
BLOCK = 128
ROPE_BASE = 10000.0
N_FREQ = HEAD_DIM // 4
POOL_WINDOWS = (2, 4, 8, 16)
POOL_GROUP_DIM = D_MODEL // 4
D_FF = 2816
N_MOD = 9
LN_EPS = 1e-5
DEEPNORM_ALPHA = (2.0 * DEPTH) ** 0.25
ATTN_SCALE = HEAD_DIM ** -0.5
NEG_INF = -1e30

N_PROMPT_TOK = BATCH * SEQ
N_LATENT_TOK = DEC_BATCH * DEC_SEQ
N_TOK = N_PROMPT_TOK + N_LATENT_TOK
MOD_GROUP_ROWS = 4096
N_COND = 1 + DEC_BATCH
COND_ROWS = 8

LANES = 128
HALF_LANES = LANES // 2
TM = 1024
N_PROMPT_TILES = N_PROMPT_TOK // TM
FF_CHUNK = 256
POOL_TM = 256
POOL_HALO = 8
D_Q = N_HEADS * HEAD_DIM
D_KV = N_KV_HEADS * HEAD_DIM
D_KV_AB = N_KV_HEADS * 2 * LANES
VMEM_LIMIT = 40 * 1024 * 1024
FFN_VMEM_LIMIT = 55 * 1024 * 1024

F32 = jnp.float32
BF16 = jnp.bfloat16


def _layer_norm(y, g, b):
    mu = jnp.mean(y, axis=-1, keepdims=True)
    yc = y - mu
    var = jnp.mean(yc * yc, axis=-1, keepdims=True)
    return yc * lax.rsqrt(var + LN_EPS) * g + b


def _modulate(x, mod_ref, row0):
    shift = mod_ref[row0:row0 + 1, :]
    scale = mod_ref[row0 + 1:row0 + 2, :]
    return x * (1.0 + scale) + shift


def _params(n_axes=1, vmem_limit=VMEM_LIMIT):
    return pltpu.CompilerParams(
        dimension_semantics=("arbitrary",) * n_axes,
        vmem_limit_bytes=vmem_limit)


def _resident(shape):
    nd = len(shape)
    return pl.BlockSpec(shape, lambda *_: (0,) * nd, pipeline_mode=pl.Buffered(1))


def _resident_at(index, tail):
    return pl.BlockSpec((None,) * len(index) + tuple(tail),
                        lambda *_: tuple(index) + (0,) * len(tail), pipeline_mode=pl.Buffered(1))


def _mod_spec(tile_rows, layer):
    tiles_per_group = MOD_GROUP_ROWS // tile_rows
    return pl.BlockSpec((None, None, N_MOD, D_MODEL),
                        lambda i: (layer, i // tiles_per_group, 0, 0))


def _ln_specs(layer, which):
    return [_resident_at((layer * 3 + which,), (1, D_MODEL))] * 2


def _slab_spec(cols, tm=TM):
    return pl.BlockSpec((tm, cols), lambda i: (i, 0))


def _prompt_spec(cols, tm=TM):
    return pl.BlockSpec((tm, cols), lambda i: (jnp.minimum(i, N_PROMPT_TOK // tm - 1), 0))


def _latent_spec(cols, tm=TM):
    return pl.BlockSpec((tm, cols), lambda i: (jnp.maximum(i - N_PROMPT_TOK // tm, 0), 0))


def _read_split(p_ref, l_ref, n_prompt_tiles=N_PROMPT_TILES):
    return jnp.where(pl.program_id(0) < n_prompt_tiles, p_ref[...], l_ref[...])


ADALN_NC = 1024


def _adaln_kernel(cond_ref, w_ref, b_ref, o_ref):
    c = cond_ref[...]
    s = (c * (1.0 / (1.0 + jnp.exp(-c)))).astype(BF16)
    o_ref[0] = jnp.dot(s, w_ref[0].astype(BF16), preferred_element_type=F32) + b_ref[0]


def _adaln(cond, w_mod, b_mod):
    n_out = N_MOD * D_MODEL
    out = pl.pallas_call(
        _adaln_kernel,
        grid=(DEPTH, n_out // ADALN_NC),
        in_specs=[
            pl.BlockSpec((COND_ROWS, D_MODEL), lambda l, n: (0, 0)),
            pl.BlockSpec((1, D_MODEL, ADALN_NC), lambda l, n: (l, 0, n)),
            pl.BlockSpec((1, 1, ADALN_NC), lambda l, n: (l, 0, n)),
        ],
        out_specs=pl.BlockSpec((1, COND_ROWS, ADALN_NC), lambda l, n: (l, 0, n)),
        out_shape=jax.ShapeDtypeStruct((DEPTH, COND_ROWS, n_out), F32),
        compiler_params=_params(2),
        name="adaln",
    )(cond, w_mod, b_mod.reshape(DEPTH, 1, n_out))
    return out[:, :N_COND].reshape(DEPTH, N_COND, N_MOD, D_MODEL)


def _ffn_body(x, mod_ref, lng_ref, lnb_ref, wg_ref, wu_ref, wd_ref, a_ref, row0):
    h = _modulate(x, mod_ref, row0).astype(BF16)
    for c in range(D_FF // FF_CHUNK):
        sl = slice(c * FF_CHUNK, (c + 1) * FF_CHUNK)
        g = jnp.dot(h, wg_ref[:, sl], preferred_element_type=F32)
        u = jnp.dot(h, wu_ref[:, sl], preferred_element_type=F32)
        a_ref[:, sl] = (g * (1.0 / (1.0 + jnp.exp(-g))) * u).astype(BF16)
    f = jnp.dot(a_ref[...], wd_ref[...], preferred_element_type=F32)
    gate = mod_ref[row0 + 2:row0 + 3, :]
    y = DEEPNORM_ALPHA * x + (0.5 * gate) * f
    return _layer_norm(y, lng_ref[...], lnb_ref[...])


CAST_STEPS = 8
FFN_WEIGHT_SHAPES = ((D_MODEL, D_FF), (D_MODEL, D_FF), (D_FF, D_MODEL))


def _ffn_kernel(*refs, row0, n_prompt_tiles, split_in, split_out, cast_next):
    n_x = 2 if split_in else 1
    n_cast = 3 if cast_next else 0
    x_refs, (mod_ref, lng_ref, lnb_ref, wg_ref, wu_ref, wd_ref) = refs[:n_x], refs[n_x:n_x + 6]
    next_f32 = refs[n_x + 6:n_x + 6 + n_cast]
    o_refs, next_bf16, a_ref = refs[n_x + 6 + n_cast:-1 - n_cast], refs[-1 - n_cast:-1], refs[-1]
    step = pl.program_id(0)
    x = _read_split(*x_refs, n_prompt_tiles) if split_in else x_refs[0][...]
    out = _ffn_body(x, mod_ref, lng_ref, lnb_ref, wg_ref, wu_ref, wd_ref, a_ref, row0)
    if split_out:
        @pl.when(step < n_prompt_tiles)
        def _():
            o_refs[0][...] = out

        @pl.when(step >= n_prompt_tiles)
        def _():
            o_refs[1][...] = out
    else:
        o_refs[0][...] = out
    if cast_next:
        @pl.when(step < CAST_STEPS)
        def _():
            for src, dst in zip(next_f32, next_bf16):
                dst[...] = src[...].astype(BF16)


def _cast_block_spec(shape, lead):
    block = (None,) * len(lead) + (shape[0] // CAST_STEPS, shape[1])
    return pl.BlockSpec(block, lambda i: tuple(lead) + (jnp.minimum(i, CAST_STEPS - 1), 0))


def _ffn(xs, mods, ln_g, ln_b, weights, layer, which, next_f32=None, split_out=False):
    split_in = len(xs) == 2
    cast_next = next_f32 is not None
    row0 = 6 * which
    ln_row = 2 * which
    tm = TM // 2 if split_in and cast_next else TM
    x_specs = ([_prompt_spec(D_MODEL, tm), _latent_spec(D_MODEL, tm)] if split_in
               else [_slab_spec(D_MODEL, tm)])
    if split_out:
        out_specs = [_prompt_spec(D_MODEL, tm), _latent_spec(D_MODEL, tm)]
        out_shape = [jax.ShapeDtypeStruct((N_PROMPT_TOK, D_MODEL), F32),
                     jax.ShapeDtypeStruct((N_LATENT_TOK, D_MODEL), F32)]
    else:
        out_specs = [_slab_spec(D_MODEL, tm)]
        out_shape = [jax.ShapeDtypeStruct((N_TOK, D_MODEL), F32)]
    n_x_out = len(out_specs)
    cast_in_specs, cast_args = [], ()
    if cast_next:
        cast_args, lead = next_f32
        cast_in_specs = [_cast_block_spec(s, lead) for s in FFN_WEIGHT_SHAPES]
        out_specs = out_specs + [_cast_block_spec(s, ()) for s in FFN_WEIGHT_SHAPES]
        out_shape = out_shape + [jax.ShapeDtypeStruct(s, BF16) for s in FFN_WEIGHT_SHAPES]
    outs = pl.pallas_call(
        functools.partial(_ffn_kernel, row0=row0, n_prompt_tiles=N_PROMPT_TOK // tm,
                          split_in=split_in, split_out=split_out, cast_next=cast_next),
        grid=(N_TOK // tm,),
        in_specs=x_specs + [_mod_spec(tm, layer)] + _ln_specs(layer, ln_row)
        + [_resident(s) for s in FFN_WEIGHT_SHAPES] + cast_in_specs,
        out_specs=out_specs,
        out_shape=out_shape,
        scratch_shapes=[pltpu.VMEM((tm, D_FF), BF16)],
        compiler_params=_params(vmem_limit=FFN_VMEM_LIMIT),
        name="ffn",
    )(*xs, mods, ln_g, ln_b, *weights, *cast_args)
    x_out = tuple(outs[:n_x_out]) if split_out else outs[0]
    return x_out, (tuple(outs[n_x_out:]) if cast_next else None)


ROPE_TILES = DEC_SEQ // TM


def _write_ab(dst_ref, pair, src):
    lo = lax.broadcasted_iota(jnp.int32, src.shape, 1) < HALF_LANES
    swapped = pltpu.roll(src, HALF_LANES, 1)
    blocks = (jnp.where(lo, src, 0.0), jnp.where(lo, 0.0, swapped),
              jnp.where(lo, swapped, 0.0), jnp.where(lo, 0.0, src))
    for n, blk in enumerate(blocks):
        c0 = (4 * pair + n) * LANES
        dst_ref[:, c0:c0 + LANES] = blk.astype(BF16)


def _qkv_kernel(x_ref, mod_ref, w_ref, cos_ref, sup_ref, sdn_ref,
                q_ref, kab_ref, vab_ref, ks_ref, vs_ref):
    h = _modulate(x_ref[...], mod_ref, 3).astype(BF16)
    qkv = jnp.dot(h, w_ref[...], preferred_element_type=F32)
    cos = cos_ref[...]
    s_up = sup_ref[...]
    s_dn = sdn_ref[...]

    def rope(blk):
        up = pltpu.roll(blk, LANES - 16, 1)
        dn = pltpu.roll(blk, 16, 1)
        return blk * cos + up * s_up + dn * s_dn

    for j in range(D_Q // LANES):
        cols = slice(j * LANES, (j + 1) * LANES)
        q_ref[:, cols] = (rope(qkv[:, cols]) * ATTN_SCALE).astype(BF16)
    for pair in range(D_KV // LANES):
        k_cols = slice(D_Q + pair * LANES, D_Q + (pair + 1) * LANES)
        v_cols = slice(D_Q + D_KV + pair * LANES, D_Q + D_KV + (pair + 1) * LANES)
        _write_ab(kab_ref, pair, rope(qkv[:, k_cols]))
        _write_ab(vab_ref, pair, qkv[:, v_cols])

    @pl.when(pl.program_id(0) < N_PROMPT_TILES)
    def _():
        k_t = qkv[:, D_Q:D_Q + D_KV].T
        v_t = qkv[:, D_Q + D_KV:].T
        for s in range(TM // SEQ):
            ks_ref[s] = k_t[:, s * SEQ:(s + 1) * SEQ]
            vs_ref[s] = v_t[:, s * SEQ:(s + 1) * SEQ]


def _rope_index(i):
    return (jnp.where(i < N_PROMPT_TILES, ROPE_TILES, (i - N_PROMPT_TILES) % ROPE_TILES), 0)


def _qkv(x, mods, w, cos_t, sup_t, sdn_t):
    seqs_per_tile = TM // SEQ
    state_spec = pl.BlockSpec((seqs_per_tile, D_KV, SEQ),
                              lambda i: (jnp.minimum(i, N_PROMPT_TILES - 1), 0, 0))
    return pl.pallas_call(
        _qkv_kernel,
        grid=(N_TOK // TM,),
        in_specs=[
            _slab_spec(D_MODEL),
            _mod_spec(TM, 0),
            _resident((D_MODEL, D_Q + 2 * D_KV)),
            pl.BlockSpec((TM, LANES), _rope_index),
            pl.BlockSpec((TM, LANES), _rope_index),
            pl.BlockSpec((TM, LANES), _rope_index),
        ],
        out_specs=[
            _slab_spec(D_Q), _slab_spec(D_KV_AB), _slab_spec(D_KV_AB), state_spec, state_spec,
        ],
        out_shape=[
            jax.ShapeDtypeStruct((N_TOK, D_Q), BF16),
            jax.ShapeDtypeStruct((N_TOK, D_KV_AB), BF16),
            jax.ShapeDtypeStruct((N_TOK, D_KV_AB), BF16),
            jax.ShapeDtypeStruct((BATCH, D_KV, SEQ), F32),
            jax.ShapeDtypeStruct((BATCH, D_KV, SEQ), F32),
        ],
        compiler_params=_params(),
        name="qkv",
    )(x, mods, w, cos_t, sup_t, sdn_t)


def _attend(q_ref, o_ref, sink_ref, k_slabs, v_slabs, masks, tq, row0=0):
    nt = (((1,), (1,)), ((), ()))
    rows = slice(row0, row0 + tq)
    row_hi = lax.broadcasted_iota(jnp.int32, (2 * tq, 1), 0) >= tq
    lane_lo = lax.broadcasted_iota(jnp.int32, (2 * tq, LANES), 1) < HALF_LANES
    for kh in range(N_KV_HEADS):
        ks = k_slabs(kh)
        vs = v_slabs(kh)
        k_cat = jnp.concatenate([s[:, :LANES] for s in ks] + [s[:, LANES:] for s in ks], axis=0)
        v_cat = jnp.concatenate([s[:, :LANES] for s in vs] + [s[:, LANES:] for s in vs], axis=0)
        n_keys = k_cat.shape[0] // 2
        j0 = 2 * kh
        q2 = jnp.concatenate([q_ref[rows, j0 * LANES:(j0 + 1) * LANES],
                              q_ref[rows, (j0 + 1) * LANES:(j0 + 2) * LANES]], axis=0)
        s = lax.dot_general(q2, k_cat, nt, preferred_element_type=F32)
        es, inv_dens = [], []
        for half in range(2):
            segs, off = [], half * n_keys
            for slab, m in zip(ks, masks):
                seg = s[:, off:off + slab.shape[0]]
                segs.append(seg if m is None else jnp.where(m, seg, NEG_INF))
                off += slab.shape[0]
            logits = jnp.concatenate(segs, axis=1)
            sink = jnp.where(row_hi, sink_ref[4 * kh + 2 + half], sink_ref[4 * kh + half])
            m_row = jnp.maximum(jnp.max(logits, axis=-1, keepdims=True), sink)
            e = jnp.exp(logits - m_row)
            den = jnp.sum(e, axis=-1, keepdims=True) + jnp.exp(sink - m_row)
            es.append(e.astype(BF16))
            inv_dens.append(1.0 / den)
        p = jnp.concatenate(es, axis=1)
        o2 = jnp.dot(p, v_cat, preferred_element_type=F32)
        o2 = o2 * jnp.where(lane_lo, inv_dens[0], inv_dens[1])
        o_ref[rows, j0 * LANES:(j0 + 1) * LANES] = o2[:tq].astype(BF16)
        o_ref[rows, (j0 + 1) * LANES:(j0 + 2) * LANES] = o2[tq:].astype(BF16)


def _kv_cols(kh):
    return slice(kh * 2 * LANES, (kh + 1) * 2 * LANES)


def _ctx_attn_kernel(sink_ref, q_ref, k_ref, v_ref, o_ref):
    _attend(q_ref, o_ref, sink_ref,
            lambda kh: [k_ref[:, _kv_cols(kh)]],
            lambda kh: [v_ref[:, _kv_cols(kh)]],
            [None], SEQ)


def _ctx_attn(sink, q, kab, vab):
    return pl.pallas_call(
        _ctx_attn_kernel,
        grid=(BATCH,),
        in_specs=[
            pl.BlockSpec(memory_space=pltpu.SMEM),
            pl.BlockSpec((SEQ, D_Q), lambda b: (b, 0)),
            pl.BlockSpec((SEQ, D_KV_AB), lambda b: (b, 0)),
            pl.BlockSpec((SEQ, D_KV_AB), lambda b: (b, 0)),
        ],
        out_specs=pl.BlockSpec((SEQ, D_Q), lambda b: (b, 0)),
        out_shape=jax.ShapeDtypeStruct((N_PROMPT_TOK, D_Q), BF16),
        compiler_params=_params(),
        name="ctx_attn",
    )(sink, q, kab, vab)


N_QBLK = DEC_SEQ // BLOCK


LAT_SUB = 2
LAT_STEPS = N_QBLK // LAT_SUB


def _lat_attn_kernel(sink_ref, q_ref, kp_ref, kc_ref, kn_ref, vp_ref, vc_ref, vn_ref,
                     kx_ref, vx_ref, o_ref):
    step = pl.program_id(1)
    r = lax.broadcasted_iota(jnp.int32, (2 * BLOCK, BLOCK), 0) & (BLOCK - 1)
    c = lax.broadcasted_iota(jnp.int32, (2 * BLOCK, BLOCK), 1)

    def blocks(p_ref, c_ref, n_ref, kh):
        cols = _kv_cols(kh)
        return ([p_ref[:, cols]]
                + [c_ref[j * BLOCK:(j + 1) * BLOCK, cols] for j in range(LAT_SUB)]
                + [n_ref[:, cols]])

    for sub in range(LAT_SUB):
        no_prev = (step == 0) if sub == 0 else False
        no_next = (step == LAT_STEPS - 1) if sub == LAT_SUB - 1 else False
        m_prev = c >= r + jnp.where(no_prev, BLOCK, 0)
        m_next = c <= r - jnp.where(no_next, BLOCK, 0)

        def slabs(p_ref, c_ref, n_ref, x_ref, sub=sub):
            return lambda kh: blocks(p_ref, c_ref, n_ref, kh)[sub:sub + 3] + [x_ref[0, :, _kv_cols(kh)]]

        _attend(q_ref, o_ref, sink_ref,
                slabs(kp_ref, kc_ref, kn_ref, kx_ref), slabs(vp_ref, vc_ref, vn_ref, vx_ref),
                [m_prev, None, m_next, None], BLOCK, row0=sub * BLOCK)


def _lat_attn(sink, q, kab, vab, kab_ctx, vab_ctx):
    tq = LAT_SUB * BLOCK
    first = N_PROMPT_TOK // BLOCK

    def cur(b, i):
        return (N_PROMPT_TOK // tq + b * LAT_STEPS + i, 0)

    def prev(b, i):
        return (first + b * N_QBLK + jnp.maximum(i * LAT_SUB - 1, 0), 0)

    def nxt(b, i):
        return (first + b * N_QBLK + jnp.minimum((i + 1) * LAT_SUB, N_QBLK - 1), 0)

    edge_spec = lambda f: pl.BlockSpec((BLOCK, D_KV_AB), f)
    cur_spec = pl.BlockSpec((tq, D_KV_AB), cur)
    ctx_spec = pl.BlockSpec((1, PAST_LEN, D_KV_AB), lambda b, i: (b, 0, 0))
    return pl.pallas_call(
        _lat_attn_kernel,
        grid=(DEC_BATCH, LAT_STEPS),
        in_specs=[
            pl.BlockSpec(memory_space=pltpu.SMEM),
            pl.BlockSpec((tq, D_Q), cur),
            edge_spec(prev), cur_spec, edge_spec(nxt),
            edge_spec(prev), cur_spec, edge_spec(nxt),
            ctx_spec, ctx_spec,
        ],
        out_specs=pl.BlockSpec((tq, D_Q), lambda b, i: (b * LAT_STEPS + i, 0)),
        out_shape=jax.ShapeDtypeStruct((N_LATENT_TOK, D_Q), BF16),
        compiler_params=_params(2),
        name="lat_attn",
    )(sink, q, kab, kab, kab, vab, vab, vab, kab_ctx, vab_ctx)


def _proj_kernel(x_ref, ap_ref, al_ref, mod_ref, lng_ref, lnb_ref, wo_ref, o_ref):
    f = jnp.dot(_read_split(ap_ref, al_ref), wo_ref[...], preferred_element_type=F32)
    gate = mod_ref[5:6, :]
    y = DEEPNORM_ALPHA * x_ref[...] + gate * f
    o_ref[...] = _layer_norm(y, lng_ref[...], lnb_ref[...])


def _proj(x, attn_prompt, attn_latent, mods, ln_g, ln_b, w_o):
    return pl.pallas_call(
        _proj_kernel,
        grid=(N_TOK // TM,),
        in_specs=[
            _slab_spec(D_MODEL),
            _prompt_spec(D_Q),
            _latent_spec(D_Q),
            _mod_spec(TM, 0)] + _ln_specs(0, 1) + [
            _resident((D_Q, D_MODEL)),
        ],
        out_specs=_slab_spec(D_MODEL),
        out_shape=jax.ShapeDtypeStruct((N_TOK, D_MODEL), F32),
        compiler_params=_params(),
        name="attn_proj",
    )(x, attn_prompt, attn_latent, mods, ln_g, ln_b, w_o)


PROMPT_POOL_TILES = N_PROMPT_TOK // POOL_TM
LATENT_POOL_TILES = DEC_SEQ // POOL_TM


def _pool_kernel(x_ref, xp_ref, xn_ref, mod_ref, lng_ref, lnb_ref, w_ref, sc_ref, o_ref):
    i = pl.program_id(0)
    in_seq = (i - PROMPT_POOL_TILES) % LATENT_POOL_TILES
    is_start = (i < PROMPT_POOL_TILES) | (in_seq == 0)
    is_end = (i < PROMPT_POOL_TILES) | (in_seq == LATENT_POOL_TILES - 1)

    x = x_ref[...]
    h = _modulate(x, mod_ref, 3)
    h_ext = jnp.concatenate([
        jnp.where(is_start, 0.0, _modulate(xp_ref[...], mod_ref, 3)), h,
        jnp.where(is_end, 0.0, _modulate(xn_ref[...], mod_ref, 3))], axis=0)
    n_ext = POOL_TM + 2 * POOL_HALO

    r8 = lax.broadcasted_iota(jnp.int32, (POOL_HALO, POOL_GROUP_DIM), 0)
    gate = mod_ref[5:6, :]
    outs = []
    for gi, w in enumerate(POOL_WINDOWS):
        half = w // 2
        cols = slice(gi * POOL_GROUP_DIM, (gi + 1) * POOL_GROUP_DIM)
        acc = h_ext[:, cols]
        s = 1
        while s < w:
            acc = acc + pltpu.roll(acc, s, 0)
            s *= 2
        if half > 1:
            acc = pltpu.roll(acc, n_ext - (half - 1), 0)
        total = acc[POOL_HALO:POOL_HALO + POOL_TM]
        cnt_top = w - jnp.where(is_start, jnp.maximum(half - r8, 0), 0)
        cnt_bot = w - jnp.where(is_end, jnp.maximum(r8 + half - POOL_HALO, 0), 0)
        inv_cnt = jnp.concatenate([
            1.0 / cnt_top.astype(F32),
            jnp.full((POOL_TM - 2 * POOL_HALO, POOL_GROUP_DIM), 1.0 / w, F32),
            1.0 / cnt_bot.astype(F32)], axis=0)
        pooled = (total * inv_cnt - h[:, cols]).astype(BF16)
        outs.append(jnp.dot(pooled, w_ref[gi], preferred_element_type=F32))
    y = DEEPNORM_ALPHA * x + (gate * sc_ref[...]) * jnp.concatenate(outs, axis=-1)
    o_ref[...] = _layer_norm(y, lng_ref[...], lnb_ref[...])


def _pool(x, mods, ln_g, ln_b, w_pool, scale):
    halo_per_tile = POOL_TM // POOL_HALO
    last_halo = N_TOK // POOL_HALO - 1
    return pl.pallas_call(
        _pool_kernel,
        grid=(N_TOK // POOL_TM,),
        in_specs=[
            pl.BlockSpec((POOL_TM, D_MODEL), lambda i: (i, 0)),
            pl.BlockSpec((POOL_HALO, D_MODEL), lambda i: (jnp.maximum(i * halo_per_tile - 1, 0), 0)),
            pl.BlockSpec((POOL_HALO, D_MODEL),
                         lambda i: (jnp.minimum((i + 1) * halo_per_tile, last_halo), 0)),
            _mod_spec(POOL_TM, 1)] + _ln_specs(1, 1) + [
            _resident((len(POOL_WINDOWS), POOL_GROUP_DIM, POOL_GROUP_DIM)),
            _resident((1, D_MODEL)),
        ],
        out_specs=pl.BlockSpec((POOL_TM, D_MODEL), lambda i: (i, 0)),
        out_shape=jax.ShapeDtypeStruct((N_TOK, D_MODEL), F32),
        compiler_params=_params(),
        name="pool",
    )(x, x, x, mods, ln_g, ln_b, w_pool, scale.reshape(1, D_MODEL))


def _ab_layout(a):
    lead = a.shape[:-1]
    a = a.reshape(lead + (N_KV_HEADS, 1, HEAD_DIM)).astype(BF16)
    z = jnp.zeros_like(a)
    return jnp.concatenate([a, z, z, a], axis=-2).reshape(lead + (D_KV_AB,))


def _rope_tables():
    n_rows = DEC_SEQ // GRID_W
    rows = jnp.repeat(jnp.arange(n_rows, dtype=F32), GRID_W)
    cols = jnp.tile(jnp.arange(GRID_W, dtype=F32), n_rows)
    inv = jnp.power(ROPE_BASE, -jnp.arange(N_FREQ, dtype=F32) / N_FREQ)
    ang_r = rows[:, None] * inv
    ang_c = cols[:, None] * inv
    ang = jnp.concatenate([ang_r, ang_r, ang_c, ang_c], axis=-1)
    cos = jnp.tile(jnp.cos(ang), (1, LANES // HEAD_DIM))
    sin = jnp.tile(jnp.sin(ang), (1, LANES // HEAD_DIM))
    first_half = (jnp.arange(LANES) % 32) < 16
    s_up = jnp.where(first_half, -sin, 0.0)
    s_dn = jnp.where(first_half, 0.0, sin)
    ident = jnp.ones((TM, LANES), F32)
    zeros = jnp.zeros((TM, LANES), F32)
    return (jnp.concatenate([cos, ident], axis=0),
            jnp.concatenate([s_up, zeros], axis=0),
            jnp.concatenate([s_dn, zeros], axis=0))


def kernel(x_prompt, x_sample, cache_k, cache_v, c, c_ctx, w_mod, b_mod, ln_g, ln_b,
           ffn_w_gate, ffn_w_up, ffn_w_down, attn_w_qkv, attn_w_o, attn_sink,
           pool_w, pool_scale):
    cond = jnp.concatenate(
        [c_ctx[None, :], c, jnp.zeros((COND_ROWS - N_COND, D_MODEL), F32)], axis=0)
    mods = _adaln(cond, w_mod, b_mod)

    ffn_w = (ffn_w_gate, ffn_w_up, ffn_w_down)
    ln_g = ln_g.reshape(DEPTH * 3, 1, D_MODEL)
    ln_b = ln_b.reshape(DEPTH * 3, 1, D_MODEL)
    ffn = functools.partial(_ffn, mods=mods, ln_g=ln_g, ln_b=ln_b)

    w_first = tuple(w[0, 0].astype(BF16) for w in ffn_w)
    x, w_next = ffn((x_prompt.reshape(N_PROMPT_TOK, D_MODEL), x_sample.reshape(N_LATENT_TOK, D_MODEL)),
                    weights=w_first, layer=0, which=0, next_f32=(ffn_w, (0, 1)))
    cos_t, sup_t, sdn_t = _rope_tables()
    q, kab, vab, k_state, v_state = _qkv(x, mods, attn_w_qkv[0].astype(BF16), cos_t, sup_t, sdn_t)
    sink = attn_sink[0]
    o_ctx = _ctx_attn(sink, q, kab, vab)
    kab_ctx = _ab_layout(cache_k[:, 0].reshape(DEC_BATCH, PAST_LEN, D_KV))
    vab_ctx = _ab_layout(cache_v[:, 0].reshape(DEC_BATCH, PAST_LEN, D_KV))
    o_lat = _lat_attn(sink, q, kab, vab, kab_ctx, vab_ctx)
    x = _proj(x, o_ctx, o_lat, mods, ln_g, ln_b, attn_w_o[0].astype(BF16))
    x, w_next = ffn((x,), weights=w_next, layer=0, which=1, next_f32=(ffn_w, (1, 0)))

    x, w_next = ffn((x,), weights=w_next, layer=1, which=0, next_f32=(ffn_w, (1, 1)))
    x = _pool(x, mods, ln_g, ln_b, pool_w[0].astype(BF16), pool_scale[0])
    (y_prompt, y_sample), _ = ffn((x,), weights=w_next, layer=1, which=1, split_out=True)

    def state(s_t):
        s_t = s_t.reshape(BATCH, N_KV_HEADS, HEAD_DIM, SEQ)
        return jnp.transpose(s_t, (0, 3, 1, 2))[:, None]

    return (y_prompt.reshape(BATCH, SEQ, D_MODEL),
            y_sample.reshape(DEC_BATCH, DEC_SEQ, D_MODEL),
            state(k_state), state(v_state))
```

```python
import functools

import jax
import jax.numpy as jnp
from jax import lax
from jax.experimental import pallas as pl
from jax.experimental.pallas import tpu as pltpu

D_MODEL = 1024
BATCH = 16
SEQ = 256
DEPTH = 2
DEC_BATCH = 2
DEC_SEQ = 4096
PAST_LEN = 256
GRID_W = 64
N_HEADS = 16
N_KV_HEADS = 4
HEAD_DIM = 64
WINDOW = 128
BLOCK = 128
ROPE_BASE = 10000.0
N_FREQ = HEAD_DIM // 4
POOL_WINDOWS = (2, 4, 8, 16)
POOL_GROUP_DIM = D_MODEL // 4
D_FF = 2816
N_MOD = 9
LN_EPS = 1e-5
DEEPNORM_ALPHA = (2.0 * DEPTH) ** 0.25
ATTN_SCALE = HEAD_DIM ** -0.5
NEG_INF = -1e30

N_PROMPT_TOK = BATCH * SEQ
N_LATENT_TOK = DEC_BATCH * DEC_SEQ
N_TOK = N_PROMPT_TOK + N_LATENT_TOK
MOD_GROUP_ROWS = 4096
N_COND = 1 + DEC_BATCH
COND_ROWS = 8

LANES = 128
HALF_LANES = LANES // 2
TM = 1024
N_PROMPT_TILES = N_PROMPT_TOK // TM
FF_CHUNK = 256
EPILOGUE_ROWS = 256
POOL_TM = 256
POOL_HALO = 8
D_Q = N_HEADS * HEAD_DIM
D_KV = N_KV_HEADS * HEAD_DIM
D_KV_AB = N_KV_HEADS * 2 * LANES
VMEM_LIMIT = 40 * 1024 * 1024
FFN_VMEM_LIMIT = 55 * 1024 * 1024

F32 = jnp.float32
BF16 = jnp.bfloat16


def _layer_norm(y, g, b):
    mu = jnp.mean(y, axis=-1, keepdims=True)
    yc = y - mu
    var = jnp.mean(yc * yc, axis=-1, keepdims=True)
    return yc * lax.rsqrt(var + LN_EPS) * g + b


def _modulate(x, mod_ref, row0):
    shift = mod_ref[row0:row0 + 1, :]
    scale = mod_ref[row0 + 1:row0 + 2, :]
    return x * (1.0 + scale) + shift


def _params(n_axes=1, vmem_limit=VMEM_LIMIT):
    return pltpu.CompilerParams(
        dimension_semantics=("arbitrary",) * n_axes,
        vmem_limit_bytes=vmem_limit)


def _resident(shape):
    nd = len(shape)
    return pl.BlockSpec(shape, lambda *_: (0,) * nd, pipeline_mode=pl.Buffered(1))


def _resident_at(index, tail):
    return pl.BlockSpec((None,) * len(index) + tuple(tail),
                        lambda *_: tuple(index) + (0,) * len(tail), pipeline_mode=pl.Buffered(1))


def _mod_spec(tile_rows, layer):
    tiles_per_group = MOD_GROUP_ROWS // tile_rows
    return pl.BlockSpec((None, None, N_MOD, D_MODEL),
                        lambda i: (layer, i // tiles_per_group, 0, 0))


def _ln_specs(layer, which):
    return [_resident_at((layer * 3 + which,), (1, D_MODEL))] * 2


def _slab_spec(cols, tm=TM):
    return pl.BlockSpec((tm, cols), lambda i: (i, 0))


def _prompt_spec(cols, tm=TM):
    return pl.BlockSpec((tm, cols), lambda i: (jnp.minimum(i, N_PROMPT_TOK // tm - 1), 0))


def _latent_spec(cols, tm=TM):
    return pl.BlockSpec((tm, cols), lambda i: (jnp.maximum(i - N_PROMPT_TOK // tm, 0), 0))


def _read_split(p_ref, l_ref, n_prompt_tiles=N_PROMPT_TILES):
    return jnp.where(pl.program_id(0) < n_prompt_tiles, p_ref[...], l_ref[...])


ADALN_NC = 1024


def _adaln_kernel(cond_ref, w_ref, b_ref, o_ref):
    c = cond_ref[...]
    s = (c * (1.0 / (1.0 + jnp.exp(-c)))).astype(BF16)
    o_ref[0] = jnp.dot(s, w_ref[0].astype(BF16), preferred_element_type=F32) + b_ref[0]


def _adaln(cond, w_mod, b_mod):
    n_out = N_MOD * D_MODEL
    out = pl.pallas_call(
        _adaln_kernel,
        grid=(DEPTH, n_out // ADALN_NC),
        in_specs=[
            pl.BlockSpec((COND_ROWS, D_MODEL), lambda l, n: (0, 0)),
            pl.BlockSpec((1, D_MODEL, ADALN_NC), lambda l, n: (l, 0, n)),
            pl.BlockSpec((1, 1, ADALN_NC), lambda l, n: (l, 0, n)),
        ],
        out_specs=pl.BlockSpec((1, COND_ROWS, ADALN_NC), lambda l, n: (l, 0, n)),
        out_shape=jax.ShapeDtypeStruct((DEPTH, COND_ROWS, n_out), F32),
        compiler_params=_params(2),
        name="adaln",
    )(cond, w_mod, b_mod.reshape(DEPTH, 1, n_out))
    return out[:, :N_COND].reshape(DEPTH, N_COND, N_MOD, D_MODEL)


def _ffn_body(x, mod_ref, lng_ref, lnb_ref, wg_ref, wu_ref, wd_ref, a_ref, o_ref, row0, side_work):
    h = _modulate(x, mod_ref, row0).astype(BF16)
    for c in range(D_FF // FF_CHUNK):
        sl = slice(c * FF_CHUNK, (c + 1) * FF_CHUNK)
        g = jnp.dot(h, wg_ref[:, sl], preferred_element_type=F32)
        u = jnp.dot(h, wu_ref[:, sl], preferred_element_type=F32)
        a_ref[:, sl] = (g * (1.0 / (1.0 + jnp.exp(-g))) * u).astype(BF16)
    half_gate = 0.5 * mod_ref[row0 + 2:row0 + 3, :]
    side_work()
    for r in range(x.shape[0] // EPILOGUE_ROWS):
        rows = slice(r * EPILOGUE_ROWS, (r + 1) * EPILOGUE_ROWS)
        f = jnp.dot(a_ref[rows, :], wd_ref[...], preferred_element_type=F32)
        y = DEEPNORM_ALPHA * x[rows] + half_gate * f
        o_ref[rows, :] = _layer_norm(y, lng_ref[...], lnb_ref[...])


CAST_STEPS = 8
FFN_WEIGHT_SHAPES = ((D_MODEL, D_FF), (D_MODEL, D_FF), (D_FF, D_MODEL))


def _ffn_kernel(*refs, row0, n_prompt_tiles, split_in, split_out, cast_next):
    n_x = 2 if split_in else 1
    n_cast = 3 if cast_next else 0
    x_refs, (mod_ref, lng_ref, lnb_ref, wg_ref, wu_ref, wd_ref) = refs[:n_x], refs[n_x:n_x + 6]
    next_f32 = refs[n_x + 6:n_x + 6 + n_cast]
    o_refs, next_bf16, a_ref = refs[n_x + 6 + n_cast:-1 - n_cast], refs[-1 - n_cast:-1], refs[-1]
    step = pl.program_id(0)
    def cast_next_weights():
        for src, dst in zip(next_f32, next_bf16):
            dst[...] = src[...].astype(BF16)

    def run(o_ref):
        x = _read_split(*x_refs, n_prompt_tiles) if split_in else x_refs[0][...]
        _ffn_body(x, mod_ref, lng_ref, lnb_ref, wg_ref, wu_ref, wd_ref, a_ref, o_ref, row0,
                  cast_next_weights)

    if split_out:
        pl.when(step < n_prompt_tiles)(lambda: run(o_refs[0]))
        pl.when(step >= n_prompt_tiles)(lambda: run(o_refs[1]))
    else:
        run(o_refs[0])


def _cast_block_spec(shape, lead):
    block = (None,) * len(lead) + (shape[0] // CAST_STEPS, shape[1])
    return pl.BlockSpec(block, lambda i: tuple(lead) + (jnp.minimum(i, CAST_STEPS - 1), 0))


def _ffn(xs, mods, ln_g, ln_b, weights, layer, which, next_f32=None, split_out=False):
    split_in = len(xs) == 2
    cast_next = next_f32 is not None
    row0 = 6 * which
    ln_row = 2 * which
    tm = TM // 2 if split_in and cast_next else TM
    x_specs = ([_prompt_spec(D_MODEL, tm), _latent_spec(D_MODEL, tm)] if split_in
               else [_slab_spec(D_MODEL, tm)])
    if split_out:
        out_specs = [_prompt_spec(D_MODEL, tm), _latent_spec(D_MODEL, tm)]
        out_shape = [jax.ShapeDtypeStruct((N_PROMPT_TOK, D_MODEL), F32),
                     jax.ShapeDtypeStruct((N_LATENT_TOK, D_MODEL), F32)]
    else:
        out_specs = [_slab_spec(D_MODEL, tm)]
        out_shape = [jax.ShapeDtypeStruct((N_TOK, D_MODEL), F32)]
    n_x_out = len(out_specs)
    cast_in_specs, cast_args = [], ()
    if cast_next:
        cast_args, lead = next_f32
        cast_in_specs = [_cast_block_spec(s, lead) for s in FFN_WEIGHT_SHAPES]
        out_specs = out_specs + [_cast_block_spec(s, ()) for s in FFN_WEIGHT_SHAPES]
        out_shape = out_shape + [jax.ShapeDtypeStruct(s, BF16) for s in FFN_WEIGHT_SHAPES]
    outs = pl.pallas_call(
        functools.partial(_ffn_kernel, row0=row0, n_prompt_tiles=N_PROMPT_TOK // tm,
                          split_in=split_in, split_out=split_out, cast_next=cast_next),
        grid=(N_TOK // tm,),
        in_specs=x_specs + [_mod_spec(tm, layer)] + _ln_specs(layer, ln_row)
        + [_resident(s) for s in FFN_WEIGHT_SHAPES] + cast_in_specs,
        out_specs=out_specs,
        out_shape=out_shape,
        scratch_shapes=[pltpu.VMEM((tm, D_FF), BF16)],
        compiler_params=_params(vmem_limit=FFN_VMEM_LIMIT),
        name="ffn",
    )(*xs, mods, ln_g, ln_b, *weights, *cast_args)
    x_out = tuple(outs[:n_x_out]) if split_out else outs[0]
    return x_out, (tuple(outs[n_x_out:]) if cast_next else None)


ROPE_TILES = DEC_SEQ // TM


def _write_ab(dst_ref, rows, pair, src):
    lo = lax.broadcasted_iota(jnp.int32, src.shape, 1) < HALF_LANES
    swapped = pltpu.roll(src, HALF_LANES, 1)
    blocks = (jnp.where(lo, src, 0.0), jnp.where(lo, 0.0, swapped),
              jnp.where(lo, swapped, 0.0), jnp.where(lo, 0.0, src))
    for n, blk in enumerate(blocks):
        c0 = (4 * pair + n) * LANES
        dst_ref[rows, c0:c0 + LANES] = blk.astype(BF16)


def _qkv_kernel(x_ref, mod_ref, w_ref, cos_ref, sup_ref, sdn_ref,
                q_ref, kab_ref, vab_ref, ks_ref, vs_ref):
    kv_raw = []
    for r in range(TM // EPILOGUE_ROWS):
        rows = slice(r * EPILOGUE_ROWS, (r + 1) * EPILOGUE_ROWS)
        h = _modulate(x_ref[rows, :], mod_ref, 3).astype(BF16)
        qkv = jnp.dot(h, w_ref[...], preferred_element_type=F32)
        cos = cos_ref[rows, :]
        s_up = sup_ref[rows, :]
        s_dn = sdn_ref[rows, :]

        def rope(blk):
            up = pltpu.roll(blk, LANES - 16, 1)
            dn = pltpu.roll(blk, 16, 1)
            return blk * cos + up * s_up + dn * s_dn

        for j in range(D_Q // LANES):
            cols = slice(j * LANES, (j + 1) * LANES)
            q_ref[rows, cols] = (rope(qkv[:, cols]) * ATTN_SCALE).astype(BF16)
        for pair in range(D_KV // LANES):
            k_cols = slice(D_Q + pair * LANES, D_Q + (pair + 1) * LANES)
            v_cols = slice(D_Q + D_KV + pair * LANES, D_Q + D_KV + (pair + 1) * LANES)
            _write_ab(kab_ref, rows, pair, rope(qkv[:, k_cols]))
            _write_ab(vab_ref, rows, pair, qkv[:, v_cols])
        kv_raw.append(qkv[:, D_Q:])

    @pl.when(pl.program_id(0) < N_PROMPT_TILES)
    def _():
        kv_t = jnp.concatenate(kv_raw, axis=0).T
        for s in range(TM // SEQ):
            ks_ref[s] = kv_t[:D_KV, s * SEQ:(s + 1) * SEQ]
            vs_ref[s] = kv_t[D_KV:, s * SEQ:(s + 1) * SEQ]


def _rope_index(i):
    return (jnp.where(i < N_PROMPT_TILES, ROPE_TILES, (i - N_PROMPT_TILES) % ROPE_TILES), 0)


def _qkv(x, mods, w, cos_t, sup_t, sdn_t):
    seqs_per_tile = TM // SEQ
    state_spec = pl.BlockSpec((seqs_per_tile, D_KV, SEQ),
                              lambda i: (jnp.minimum(i, N_PROMPT_TILES - 1), 0, 0))
    return pl.pallas_call(
        _qkv_kernel,
        grid=(N_TOK // TM,),
        in_specs=[
            _slab_spec(D_MODEL),
            _mod_spec(TM, 0),
            _resident((D_MODEL, D_Q + 2 * D_KV)),
            pl.BlockSpec((TM, LANES), _rope_index),
            pl.BlockSpec((TM, LANES), _rope_index),
            pl.BlockSpec((TM, LANES), _rope_index),
        ],
        out_specs=[
            _slab_spec(D_Q), _slab_spec(D_KV_AB), _slab_spec(D_KV_AB), state_spec, state_spec,
        ],
        out_shape=[
            jax.ShapeDtypeStruct((N_TOK, D_Q), BF16),
            jax.ShapeDtypeStruct((N_TOK, D_KV_AB), BF16),
            jax.ShapeDtypeStruct((N_TOK, D_KV_AB), BF16),
            jax.ShapeDtypeStruct((BATCH, D_KV, SEQ), F32),
            jax.ShapeDtypeStruct((BATCH, D_KV, SEQ), F32),
        ],
        compiler_params=_params(),
        name="qkv",
    )(x, mods, w, cos_t, sup_t, sdn_t)


def _attend(q_ref, o_ref, sink_ref, k_slabs, v_slabs, masks, tq, row0=0):
    nt = (((1,), (1,)), ((), ()))
    rows = slice(row0, row0 + tq)
    row_hi = lax.broadcasted_iota(jnp.int32, (2 * tq, 1), 0) >= tq
    lane_lo = lax.broadcasted_iota(jnp.int32, (2 * tq, LANES), 1) < HALF_LANES
    for kh in range(N_KV_HEADS):
        ks = k_slabs(kh)
        vs = v_slabs(kh)
        k_cat = jnp.concatenate([s[:, :LANES] for s in ks] + [s[:, LANES:] for s in ks], axis=0)
        v_cat = jnp.concatenate([s[:, :LANES] for s in vs] + [s[:, LANES:] for s in vs], axis=0)
        n_keys = k_cat.shape[0] // 2
        j0 = 2 * kh
        q2 = jnp.concatenate([q_ref[rows, j0 * LANES:(j0 + 1) * LANES],
                              q_ref[rows, (j0 + 1) * LANES:(j0 + 2) * LANES]], axis=0)
        s = lax.dot_general(q2, k_cat, nt, preferred_element_type=F32)
        es, inv_dens = [], []
        for half in range(2):
            segs, off = [], half * n_keys
            for slab, m in zip(ks, masks):
                seg = s[:, off:off + slab.shape[0]]
                segs.append(seg if m is None else jnp.where(m, seg, NEG_INF))
                off += slab.shape[0]
            logits = jnp.concatenate(segs, axis=1)
            sink = jnp.where(row_hi, sink_ref[4 * kh + 2 + half], sink_ref[4 * kh + half])
            m_row = jnp.maximum(jnp.max(logits, axis=-1, keepdims=True), sink)
            e = jnp.exp(logits - m_row)
            den = jnp.sum(e, axis=-1, keepdims=True) + jnp.exp(sink - m_row)
            es.append(e.astype(BF16))
            inv_dens.append(1.0 / den)
        p = jnp.concatenate(es, axis=1)
        o2 = jnp.dot(p, v_cat, preferred_element_type=F32)
        o2 = o2 * jnp.where(lane_lo, inv_dens[0], inv_dens[1])
        o_ref[rows, j0 * LANES:(j0 + 1) * LANES] = o2[:tq].astype(BF16)
        o_ref[rows, (j0 + 1) * LANES:(j0 + 2) * LANES] = o2[tq:].astype(BF16)


def _kv_cols(kh):
    return slice(kh * 2 * LANES, (kh + 1) * 2 * LANES)


def _ctx_attn_kernel(sink_ref, q_ref, k_ref, v_ref, o_ref):
    _attend(q_ref, o_ref, sink_ref,
            lambda kh: [k_ref[:, _kv_cols(kh)]],
            lambda kh: [v_ref[:, _kv_cols(kh)]],
            [None], SEQ)


def _ctx_attn(sink, q, kab, vab):
    return pl.pallas_call(
        _ctx_attn_kernel,
        grid=(BATCH,),
        in_specs=[
            pl.BlockSpec(memory_space=pltpu.SMEM),
            pl.BlockSpec((SEQ, D_Q), lambda b: (b, 0)),
            pl.BlockSpec((SEQ, D_KV_AB), lambda b: (b, 0)),
            pl.BlockSpec((SEQ, D_KV_AB), lambda b: (b, 0)),
        ],
        out_specs=pl.BlockSpec((SEQ, D_Q), lambda b: (b, 0)),
        out_shape=jax.ShapeDtypeStruct((N_PROMPT_TOK, D_Q), BF16),
        compiler_params=_params(),
        name="ctx_attn",
    )(sink, q, kab, vab)


N_QBLK = DEC_SEQ // BLOCK


LAT_SUB = 2
LAT_STEPS = N_QBLK // LAT_SUB


def _lat_attn_kernel(sink_ref, q_ref, kp_ref, kc_ref, kn_ref, vp_ref, vc_ref, vn_ref,
                     kx_ref, vx_ref, o_ref):
    step = pl.program_id(1)
    r = lax.broadcasted_iota(jnp.int32, (2 * BLOCK, BLOCK), 0) & (BLOCK - 1)
    c = lax.broadcasted_iota(jnp.int32, (2 * BLOCK, BLOCK), 1)

    def blocks(p_ref, c_ref, n_ref, kh):
        cols = _kv_cols(kh)
        return ([p_ref[:, cols]]
                + [c_ref[j * BLOCK:(j + 1) * BLOCK, cols] for j in range(LAT_SUB)]
                + [n_ref[:, cols]])

    for sub in range(LAT_SUB):
        no_prev = (step == 0) if sub == 0 else False
        no_next = (step == LAT_STEPS - 1) if sub == LAT_SUB - 1 else False
        m_prev = c >= r + jnp.where(no_prev, BLOCK, 0)
        m_next = c <= r - jnp.where(no_next, BLOCK, 0)

        def slabs(p_ref, c_ref, n_ref, x_ref, sub=sub):
            return lambda kh: blocks(p_ref, c_ref, n_ref, kh)[sub:sub + 3] + [x_ref[0, :, _kv_cols(kh)]]

        _attend(q_ref, o_ref, sink_ref,
                slabs(kp_ref, kc_ref, kn_ref, kx_ref), slabs(vp_ref, vc_ref, vn_ref, vx_ref),
                [m_prev, None, m_next, None], BLOCK, row0=sub * BLOCK)


def _lat_attn(sink, q, kab, vab, kab_ctx, vab_ctx):
    tq = LAT_SUB * BLOCK
    first = N_PROMPT_TOK // BLOCK

    def cur(b, i):
        return (N_PROMPT_TOK // tq + b * LAT_STEPS + i, 0)

    def prev(b, i):
        return (first + b * N_QBLK + jnp.maximum(i * LAT_SUB - 1, 0), 0)

    def nxt(b, i):
        return (first + b * N_QBLK + jnp.minimum((i + 1) * LAT_SUB, N_QBLK - 1), 0)

    edge_spec = lambda f: pl.BlockSpec((BLOCK, D_KV_AB), f)
    cur_spec = pl.BlockSpec((tq, D_KV_AB), cur)
    ctx_spec = pl.BlockSpec((1, PAST_LEN, D_KV_AB), lambda b, i: (b, 0, 0))
    return pl.pallas_call(
        _lat_attn_kernel,
        grid=(DEC_BATCH, LAT_STEPS),
        in_specs=[
            pl.BlockSpec(memory_space=pltpu.SMEM),
            pl.BlockSpec((tq, D_Q), cur),
            edge_spec(prev), cur_spec, edge_spec(nxt),
            edge_spec(prev), cur_spec, edge_spec(nxt),
            ctx_spec, ctx_spec,
        ],
        out_specs=pl.BlockSpec((tq, D_Q), lambda b, i: (b * LAT_STEPS + i, 0)),
        out_shape=jax.ShapeDtypeStruct((N_LATENT_TOK, D_Q), BF16),
        compiler_params=_params(2),
        name="lat_attn",
    )(sink, q, kab, kab, kab, vab, vab, vab, kab_ctx, vab_ctx)


def _proj_kernel(x_ref, ap_ref, al_ref, mod_ref, lng_ref, lnb_ref, wo_ref, o_ref):
    a = _read_split(ap_ref, al_ref)
    gate = mod_ref[5:6, :]
    for r in range(TM // EPILOGUE_ROWS):
        rows = slice(r * EPILOGUE_ROWS, (r + 1) * EPILOGUE_ROWS)
        f = jnp.dot(a[rows], wo_ref[...], preferred_element_type=F32)
        y = DEEPNORM_ALPHA * x_ref[rows, :] + gate * f
        o_ref[rows, :] = _layer_norm(y, lng_ref[...], lnb_ref[...])


def _proj(x, attn_prompt, attn_latent, mods, ln_g, ln_b, w_o):
    return pl.pallas_call(
        _proj_kernel,
        grid=(N_TOK // TM,),
        in_specs=[
            _slab_spec(D_MODEL),
            _prompt_spec(D_Q),
            _latent_spec(D_Q),
            _mod_spec(TM, 0)] + _ln_specs(0, 1) + [
            _resident((D_Q, D_MODEL)),
        ],
        out_specs=_slab_spec(D_MODEL),
        out_shape=jax.ShapeDtypeStruct((N_TOK, D_MODEL), F32),
        compiler_params=_params(),
        name="attn_proj",
    )(x, attn_prompt, attn_latent, mods, ln_g, ln_b, w_o)


PROMPT_POOL_TILES = N_PROMPT_TOK // POOL_TM
LATENT_POOL_TILES = DEC_SEQ // POOL_TM


def _pool_kernel(x_ref, xp_ref, xn_ref, mod_ref, lng_ref, lnb_ref, w_ref, sc_ref, o_ref):
    i = pl.program_id(0)
    in_seq = (i - PROMPT_POOL_TILES) % LATENT_POOL_TILES
    is_start = (i < PROMPT_POOL_TILES) | (in_seq == 0)
    is_end = (i < PROMPT_POOL_TILES) | (in_seq == LATENT_POOL_TILES - 1)

    x = x_ref[...]
    h = _modulate(x, mod_ref, 3)
    h_ext = jnp.concatenate([
        jnp.where(is_start, 0.0, _modulate(xp_ref[...], mod_ref, 3)), h,
        jnp.where(is_end, 0.0, _modulate(xn_ref[...], mod_ref, 3))], axis=0)
    n_ext = POOL_TM + 2 * POOL_HALO

    r8 = lax.broadcasted_iota(jnp.int32, (POOL_HALO, POOL_GROUP_DIM), 0)
    gate = mod_ref[5:6, :]
    outs = []
    for gi, w in enumerate(POOL_WINDOWS):
        half = w // 2
        cols = slice(gi * POOL_GROUP_DIM, (gi + 1) * POOL_GROUP_DIM)
        acc = h_ext[:, cols]
        s = 1
        while s < w:
            acc = acc + pltpu.roll(acc, s, 0)
            s *= 2
        if half > 1:
            acc = pltpu.roll(acc, n_ext - (half - 1), 0)
        total = acc[POOL_HALO:POOL_HALO + POOL_TM]
        cnt_top = w - jnp.where(is_start, jnp.maximum(half - r8, 0), 0)
        cnt_bot = w - jnp.where(is_end, jnp.maximum(r8 + half - POOL_HALO, 0), 0)
        inv_cnt = jnp.concatenate([
            1.0 / cnt_top.astype(F32),
            jnp.full((POOL_TM - 2 * POOL_HALO, POOL_GROUP_DIM), 1.0 / w, F32),
            1.0 / cnt_bot.astype(F32)], axis=0)
        pooled = (total * inv_cnt - h[:, cols]).astype(BF16)
        outs.append(jnp.dot(pooled, w_ref[gi], preferred_element_type=F32))
    y = DEEPNORM_ALPHA * x + (gate * sc_ref[...]) * jnp.concatenate(outs, axis=-1)
    o_ref[...] = _layer_norm(y, lng_ref[...], lnb_ref[...])


def _pool(x, mods, ln_g, ln_b, w_pool, scale):
    halo_per_tile = POOL_TM // POOL_HALO
    last_halo = N_TOK // POOL_HALO - 1
    return pl.pallas_call(
        _pool_kernel,
        grid=(N_TOK // POOL_TM,),
        in_specs=[
            pl.BlockSpec((POOL_TM, D_MODEL), lambda i: (i, 0)),
            pl.BlockSpec((POOL_HALO, D_MODEL), lambda i: (jnp.maximum(i * halo_per_tile - 1, 0), 0)),
            pl.BlockSpec((POOL_HALO, D_MODEL),
                         lambda i: (jnp.minimum((i + 1) * halo_per_tile, last_halo), 0)),
            _mod_spec(POOL_TM, 1)] + _ln_specs(1, 1) + [
            _resident((len(POOL_WINDOWS), POOL_GROUP_DIM, POOL_GROUP_DIM)),
            _resident((1, D_MODEL)),
        ],
        out_specs=pl.BlockSpec((POOL_TM, D_MODEL), lambda i: (i, 0)),
        out_shape=jax.ShapeDtypeStruct((N_TOK, D_MODEL), F32),
        compiler_params=_params(),
        name="pool",
    )(x, x, x, mods, ln_g, ln_b, w_pool, scale.reshape(1, D_MODEL))


def _ab_layout(a):
    lead = a.shape[:-1]
    a = a.reshape(lead + (N_KV_HEADS, 1, HEAD_DIM)).astype(BF16)
    z = jnp.zeros_like(a)
    return jnp.concatenate([a, z, z, a], axis=-2).reshape(lead + (D_KV_AB,))


def _rope_tables():
    n_rows = DEC_SEQ // GRID_W
    rows = jnp.repeat(jnp.arange(n_rows, dtype=F32), GRID_W)
    cols = jnp.tile(jnp.arange(GRID_W, dtype=F32), n_rows)
    inv = jnp.power(ROPE_BASE, -jnp.arange(N_FREQ, dtype=F32) / N_FREQ)
    ang_r = rows[:, None] * inv
    ang_c = cols[:, None] * inv
    ang = jnp.concatenate([ang_r, ang_r, ang_c, ang_c], axis=-1)
    cos = jnp.tile(jnp.cos(ang), (1, LANES // HEAD_DIM))
    sin = jnp.tile(jnp.sin(ang), (1, LANES // HEAD_DIM))
    first_half = (jnp.arange(LANES) % 32) < 16
    s_up = jnp.where(first_half, -sin, 0.0)
    s_dn = jnp.where(first_half, 0.0, sin)
    ident = jnp.ones((TM, LANES), F32)
    zeros = jnp.zeros((TM, LANES), F32)
    return (jnp.concatenate([cos, ident], axis=0),
            jnp.concatenate([s_up, zeros], axis=0),
            jnp.concatenate([s_dn, zeros], axis=0))


def kernel(x_prompt, x_sample, cache_k, cache_v, c, c_ctx, w_mod, b_mod, ln_g, ln_b,
           ffn_w_gate, ffn_w_up, ffn_w_down, attn_w_qkv, attn_w_o, attn_sink,
           pool_w, pool_scale):
    cond = jnp.concatenate(
        [c_ctx[None, :], c, jnp.zeros((COND_ROWS - N_COND, D_MODEL), F32)], axis=0)
    mods = _adaln(cond, w_mod, b_mod)

    ffn_w = (ffn_w_gate, ffn_w_up, ffn_w_down)
    ln_g = ln_g.reshape(DEPTH * 3, 1, D_MODEL)
    ln_b = ln_b.reshape(DEPTH * 3, 1, D_MODEL)
    ffn = functools.partial(_ffn, mods=mods, ln_g=ln_g, ln_b=ln_b)

    w_first = tuple(w[0, 0].astype(BF16) for w in ffn_w)
    x, w_next = ffn((x_prompt.reshape(N_PROMPT_TOK, D_MODEL), x_sample.reshape(N_LATENT_TOK, D_MODEL)),
                    weights=w_first, layer=0, which=0, next_f32=(ffn_w, (0, 1)))
    cos_t, sup_t, sdn_t = _rope_tables()
    q, kab, vab, k_state, v_state = _qkv(x, mods, attn_w_qkv[0].astype(BF16), cos_t, sup_t, sdn_t)
    sink = attn_sink[0]
    o_ctx = _ctx_attn(sink, q, kab, vab)
    kab_ctx = _ab_layout(cache_k[:, 0].reshape(DEC_BATCH, PAST_LEN, D_KV))
    vab_ctx = _ab_layout(cache_v[:, 0].reshape(DEC_BATCH, PAST_LEN, D_KV))
    o_lat = _lat_attn(sink, q, kab, vab, kab_ctx, vab_ctx)
    x = _proj(x, o_ctx, o_lat, mods, ln_g, ln_b, attn_w_o[0].astype(BF16))
    x, w_next = ffn((x,), weights=w_next, layer=0, which=1, next_f32=(ffn_w, (1, 0)))

    x, w_next = ffn((x,), weights=w_next, layer=1, which=0, next_f32=(ffn_w, (1, 1)))
    x = _pool(x, mods, ln_g, ln_b, pool_w[0].astype(BF16), pool_scale[0])
    (y_prompt, y_sample), _ = ffn((x,), weights=w_next, layer=1, which=1, split_out=True)

    def state(s_t):
        s_t = s_t.reshape(BATCH, N_KV_HEADS, HEAD_DIM, SEQ)
        return jnp.transpose(s_t, (0, 3, 1, 2))[:, None]

    return (y_prompt.reshape(BATCH, SEQ, D_MODEL),
            y_sample.reshape(DEC_BATCH, DEC_SEQ, D_MODEL),
            state(k_state), state(v_state))
```

```python
import functools

import jax
import jax.numpy as jnp
from jax import lax
from jax.experimental import pallas as pl
from jax.experimental.pallas import tpu as pltpu

D_MODEL = 1024
BATCH = 16
SEQ = 256
DEPTH = 2
DEC_BATCH = 2
DEC_SEQ = 4096
PAST_LEN = 256
GRID_W = 64
N_HEADS = 16
N_KV_HEADS = 4
HEAD_DIM = 64
WINDOW = 128
BLOCK = 128
ROPE_BASE = 10000.0
N_FREQ = HEAD_DIM // 4
POOL_WINDOWS = (2, 4, 8, 16)
POOL_GROUP_DIM = D_MODEL // 4
D_FF = 2816
N_MOD = 9
LN_EPS = 1e-5
DEEPNORM_ALPHA = (2.0 * DEPTH) ** 0.25
ATTN_SCALE = HEAD_DIM ** -0.5
LOG2_E = 1.4426950408889634
Q_SCALE = ATTN_SCALE * LOG2_E
NEG_INF = -1e30

N_PROMPT_TOK = BATCH * SEQ
N_LATENT_TOK = DEC_BATCH * DEC_SEQ
N_TOK = N_PROMPT_TOK + N_LATENT_TOK
MOD_GROUP_ROWS = 4096
N_COND = 1 + DEC_BATCH
COND_ROWS = 8

LANES = 128
HALF_LANES = LANES // 2
TM = 1024
N_PROMPT_TILES = N_PROMPT_TOK // TM
FF_CHUNK = 256
EPILOGUE_ROWS = 256
POOL_TM = 256
POOL_HALO = 8
D_Q = N_HEADS * HEAD_DIM
D_KV = N_KV_HEADS * HEAD_DIM
D_KV_AB = N_KV_HEADS * 2 * LANES
VMEM_LIMIT = 40 * 1024 * 1024
FFN_VMEM_LIMIT = 55 * 1024 * 1024

F32 = jnp.float32
BF16 = jnp.bfloat16


def _layer_norm(y, g, b):
    mu = jnp.mean(y, axis=-1, keepdims=True)
    yc = y - mu
    var = jnp.mean(yc * yc, axis=-1, keepdims=True)
    return yc * lax.rsqrt(var + LN_EPS) * g + b


def _modulate(x, mod_ref, row0):
    shift = mod_ref[row0:row0 + 1, :]
    scale = mod_ref[row0 + 1:row0 + 2, :]
    return x * (1.0 + scale) + shift


def _params(n_axes=1, vmem_limit=VMEM_LIMIT):
    return pltpu.CompilerParams(
        dimension_semantics=("arbitrary",) * n_axes,
        vmem_limit_bytes=vmem_limit)


def _resident(shape):
    nd = len(shape)
    return pl.BlockSpec(shape, lambda *_: (0,) * nd, pipeline_mode=pl.Buffered(1))


def _resident_at(index, tail):
    return pl.BlockSpec((None,) * len(index) + tuple(tail),
                        lambda *_: tuple(index) + (0,) * len(tail), pipeline_mode=pl.Buffered(1))


def _mod_spec(tile_rows, layer):
    tiles_per_group = MOD_GROUP_ROWS // tile_rows
    return pl.BlockSpec((None, None, N_MOD, D_MODEL),
                        lambda i: (layer, i // tiles_per_group, 0, 0))


def _ln_specs(layer, which):
    return [_resident_at((layer * 3 + which,), (1, D_MODEL))] * 2


def _slab_spec(cols, tm=TM):
    return pl.BlockSpec((tm, cols), lambda i: (i, 0))


def _prompt_spec(cols, tm=TM):
    return pl.BlockSpec((tm, cols), lambda i: (jnp.minimum(i, N_PROMPT_TOK // tm - 1), 0))


def _latent_spec(cols, tm=TM):
    return pl.BlockSpec((tm, cols), lambda i: (jnp.maximum(i - N_PROMPT_TOK // tm, 0), 0))


def _read_split(p_ref, l_ref, n_prompt_tiles=N_PROMPT_TILES):
    return jnp.where(pl.program_id(0) < n_prompt_tiles, p_ref[...], l_ref[...])


ADALN_NC = 1024


def _adaln_kernel(cond_ref, w_ref, b_ref, o_ref):
    c = cond_ref[...]
    s = (c * (1.0 / (1.0 + jnp.exp(-c)))).astype(BF16)
    o_ref[0] = jnp.dot(s, w_ref[0].astype(BF16), preferred_element_type=F32) + b_ref[0]


def _adaln(cond, w_mod, b_mod):
    n_out = N_MOD * D_MODEL
    out = pl.pallas_call(
        _adaln_kernel,
        grid=(DEPTH, n_out // ADALN_NC),
        in_specs=[
            pl.BlockSpec((COND_ROWS, D_MODEL), lambda l, n: (0, 0)),
            pl.BlockSpec((1, D_MODEL, ADALN_NC), lambda l, n: (l, 0, n)),
            pl.BlockSpec((1, 1, ADALN_NC), lambda l, n: (l, 0, n)),
        ],
        out_specs=pl.BlockSpec((1, COND_ROWS, ADALN_NC), lambda l, n: (l, 0, n)),
        out_shape=jax.ShapeDtypeStruct((DEPTH, COND_ROWS, n_out), F32),
        compiler_params=_params(2),
        name="adaln",
    )(cond, w_mod, b_mod.reshape(DEPTH, 1, n_out))
    return out[:, :N_COND].reshape(DEPTH, N_COND, N_MOD, D_MODEL)


def _ffn_body(x, mod_ref, lng_ref, lnb_ref, wg_ref, wu_ref, wd_ref, a_ref, row0):
    h = _modulate(x, mod_ref, row0).astype(BF16)
    for c in range(D_FF // FF_CHUNK):
        sl = slice(c * FF_CHUNK, (c + 1) * FF_CHUNK)
        g = jnp.dot(h, wg_ref[:, sl], preferred_element_type=F32)
        u = jnp.dot(h, wu_ref[:, sl], preferred_element_type=F32)
        a_ref[:, sl] = (g * (1.0 / (1.0 + jnp.exp(-g))) * u).astype(BF16)
    f = jnp.dot(a_ref[...], wd_ref[...], preferred_element_type=F32)
    gate = mod_ref[row0 + 2:row0 + 3, :]
    y = DEEPNORM_ALPHA * x + (0.5 * gate) * f
    return _layer_norm(y, lng_ref[...], lnb_ref[...])


CAST_STEPS = 8
FFN_WEIGHT_SHAPES = ((D_MODEL, D_FF), (D_MODEL, D_FF), (D_FF, D_MODEL))


def _ffn_kernel(*refs, row0, n_prompt_tiles, split_in, split_out, cast_next):
    n_x = 2 if split_in else 1
    n_cast = 3 if cast_next else 0
    x_refs, (mod_ref, lng_ref, lnb_ref, wg_ref, wu_ref, wd_ref) = refs[:n_x], refs[n_x:n_x + 6]
    next_f32 = refs[n_x + 6:n_x + 6 + n_cast]
    o_refs, next_bf16, a_ref = refs[n_x + 6 + n_cast:-1 - n_cast], refs[-1 - n_cast:-1], refs[-1]
    step = pl.program_id(0)
    x = _read_split(*x_refs, n_prompt_tiles) if split_in else x_refs[0][...]
    out = _ffn_body(x, mod_ref, lng_ref, lnb_ref, wg_ref, wu_ref, wd_ref, a_ref, row0)
    if split_out:
        @pl.when(step < n_prompt_tiles)
        def _():
            o_refs[0][...] = out

        @pl.when(step >= n_prompt_tiles)
        def _():
            o_refs[1][...] = out
    else:
        o_refs[0][...] = out
    if cast_next:
        @pl.when(step < CAST_STEPS)
        def _():
            for src, dst in zip(next_f32, next_bf16):
                dst[...] = src[...].astype(BF16)


def _cast_block_spec(shape, lead):
    block = (None,) * len(lead) + (shape[0] // CAST_STEPS, shape[1])
    return pl.BlockSpec(block, lambda i: tuple(lead) + (jnp.minimum(i, CAST_STEPS - 1), 0))


def _ffn(xs, mods, ln_g, ln_b, weights, layer, which, next_f32=None, split_out=False):
    split_in = len(xs) == 2
    cast_next = next_f32 is not None
    row0 = 6 * which
    ln_row = 2 * which
    tm = TM // 2 if split_in and cast_next else TM
    x_specs = ([_prompt_spec(D_MODEL, tm), _latent_spec(D_MODEL, tm)] if split_in
               else [_slab_spec(D_MODEL, tm)])
    if split_out:
        out_specs = [_prompt_spec(D_MODEL, tm), _latent_spec(D_MODEL, tm)]
        out_shape = [jax.ShapeDtypeStruct((N_PROMPT_TOK, D_MODEL), F32),
                     jax.ShapeDtypeStruct((N_LATENT_TOK, D_MODEL), F32)]
    else:
        out_specs = [_slab_spec(D_MODEL, tm)]
        out_shape = [jax.ShapeDtypeStruct((N_TOK, D_MODEL), F32)]
    n_x_out = len(out_specs)
    cast_in_specs, cast_args = [], ()
    if cast_next:
        cast_args, lead = next_f32
        cast_in_specs = [_cast_block_spec(s, lead) for s in FFN_WEIGHT_SHAPES]
        out_specs = out_specs + [_cast_block_spec(s, ()) for s in FFN_WEIGHT_SHAPES]
        out_shape = out_shape + [jax.ShapeDtypeStruct(s, BF16) for s in FFN_WEIGHT_SHAPES]
    outs = pl.pallas_call(
        functools.partial(_ffn_kernel, row0=row0, n_prompt_tiles=N_PROMPT_TOK // tm,
                          split_in=split_in, split_out=split_out, cast_next=cast_next),
        grid=(N_TOK // tm,),
        in_specs=x_specs + [_mod_spec(tm, layer)] + _ln_specs(layer, ln_row)
        + [_resident(s) for s in FFN_WEIGHT_SHAPES] + cast_in_specs,
        out_specs=out_specs,
        out_shape=out_shape,
        scratch_shapes=[pltpu.VMEM((tm, D_FF), BF16)],
        compiler_params=_params(vmem_limit=FFN_VMEM_LIMIT),
        name="ffn",
    )(*xs, mods, ln_g, ln_b, *weights, *cast_args)
    x_out = tuple(outs[:n_x_out]) if split_out else outs[0]
    return x_out, (tuple(outs[n_x_out:]) if cast_next else None)


ROPE_TILES = DEC_SEQ // TM


def _write_ab(dst_ref, rows, pair, src):
    lo = lax.broadcasted_iota(jnp.int32, src.shape, 1) < HALF_LANES
    swapped = pltpu.roll(src, HALF_LANES, 1)
    blocks = (jnp.where(lo, src, 0.0), jnp.where(lo, 0.0, swapped),
              jnp.where(lo, swapped, 0.0), jnp.where(lo, 0.0, src))
    for n, blk in enumerate(blocks):
        c0 = (4 * pair + n) * LANES
        dst_ref[rows, c0:c0 + LANES] = blk.astype(BF16)


def _qkv_kernel(x_ref, mod_ref, w_ref, cos_ref, sup_ref, sdn_ref,
                q_ref, kab_ref, vab_ref, ks_ref, vs_ref):
    kv_raw = []
    for r in range(TM // EPILOGUE_ROWS):
        rows = slice(r * EPILOGUE_ROWS, (r + 1) * EPILOGUE_ROWS)
        h = _modulate(x_ref[rows, :], mod_ref, 3).astype(BF16)
        qkv = jnp.dot(h, w_ref[...], preferred_element_type=F32)
        cos = cos_ref[rows, :]
        s_up = sup_ref[rows, :]
        s_dn = sdn_ref[rows, :]

        def rope(blk):
            up = pltpu.roll(blk, LANES - 16, 1)
            dn = pltpu.roll(blk, 16, 1)
            return blk * cos + up * s_up + dn * s_dn

        for j in range(D_Q // LANES):
            cols = slice(j * LANES, (j + 1) * LANES)
            q_ref[rows, cols] = (rope(qkv[:, cols]) * Q_SCALE).astype(BF16)
        for pair in range(D_KV // LANES):
            k_cols = slice(D_Q + pair * LANES, D_Q + (pair + 1) * LANES)
            v_cols = slice(D_Q + D_KV + pair * LANES, D_Q + D_KV + (pair + 1) * LANES)
            _write_ab(kab_ref, rows, pair, rope(qkv[:, k_cols]))
            _write_ab(vab_ref, rows, pair, qkv[:, v_cols])
        kv_raw.append(qkv[:, D_Q:])

    @pl.when(pl.program_id(0) < N_PROMPT_TILES)
    def _():
        kv_t = jnp.concatenate(kv_raw, axis=0).T
        for s in range(TM // SEQ):
            ks_ref[s] = kv_t[:D_KV, s * SEQ:(s + 1) * SEQ]
            vs_ref[s] = kv_t[D_KV:, s * SEQ:(s + 1) * SEQ]


def _rope_index(i):
    return (jnp.where(i < N_PROMPT_TILES, ROPE_TILES, (i - N_PROMPT_TILES) % ROPE_TILES), 0)


def _qkv(x, mods, w, cos_t, sup_t, sdn_t):
    seqs_per_tile = TM // SEQ
    state_spec = pl.BlockSpec((seqs_per_tile, D_KV, SEQ),
                              lambda i: (jnp.minimum(i, N_PROMPT_TILES - 1), 0, 0))
    return pl.pallas_call(
        _qkv_kernel,
        grid=(N_TOK // TM,),
        in_specs=[
            _slab_spec(D_MODEL),
            _mod_spec(TM, 0),
            _resident((D_MODEL, D_Q + 2 * D_KV)),
            pl.BlockSpec((TM, LANES), _rope_index),
            pl.BlockSpec((TM, LANES), _rope_index),
            pl.BlockSpec((TM, LANES), _rope_index),
        ],
        out_specs=[
            _slab_spec(D_Q), _slab_spec(D_KV_AB), _slab_spec(D_KV_AB), state_spec, state_spec,
        ],
        out_shape=[
            jax.ShapeDtypeStruct((N_TOK, D_Q), BF16),
            jax.ShapeDtypeStruct((N_TOK, D_KV_AB), BF16),
            jax.ShapeDtypeStruct((N_TOK, D_KV_AB), BF16),
            jax.ShapeDtypeStruct((BATCH, D_KV, SEQ), F32),
            jax.ShapeDtypeStruct((BATCH, D_KV, SEQ), F32),
        ],
        compiler_params=_params(),
        name="qkv",
    )(x, mods, w, cos_t, sup_t, sdn_t)


def _attend(q_ref, o_ref, sink_ref, k_slabs, v_slabs, masks, tq, row0=0):
    nt = (((1,), (1,)), ((), ()))
    rows = slice(row0, row0 + tq)
    row_hi = lax.broadcasted_iota(jnp.int32, (2 * tq, 1), 0) >= tq
    lane_lo = lax.broadcasted_iota(jnp.int32, (2 * tq, LANES), 1) < HALF_LANES
    for kh in range(N_KV_HEADS):
        ks = k_slabs(kh)
        vs = v_slabs(kh)
        k_cat = jnp.concatenate([s[:, :LANES] for s in ks] + [s[:, LANES:] for s in ks], axis=0)
        v_cat = jnp.concatenate([s[:, :LANES] for s in vs] + [s[:, LANES:] for s in vs], axis=0)
        n_keys = k_cat.shape[0] // 2
        j0 = 2 * kh
        q2 = jnp.concatenate([q_ref[rows, j0 * LANES:(j0 + 1) * LANES],
                              q_ref[rows, (j0 + 1) * LANES:(j0 + 2) * LANES]], axis=0)
        s = lax.dot_general(q2, k_cat, nt, preferred_element_type=F32)
        es, inv_dens = [], []
        for half in range(2):
            segs, off = [], half * n_keys
            for slab, m in zip(ks, masks):
                seg = s[:, off:off + slab.shape[0]]
                segs.append(seg if m is None else jnp.where(m, seg, NEG_INF))
                off += slab.shape[0]
            logits = jnp.concatenate(segs, axis=1)
            sink = LOG2_E * jnp.where(row_hi, sink_ref[4 * kh + 2 + half], sink_ref[4 * kh + half])
            m_row = jnp.maximum(jnp.max(logits, axis=-1, keepdims=True), sink)
            e = jnp.exp2(logits - m_row)
            den = jnp.sum(e, axis=-1, keepdims=True) + jnp.exp2(sink - m_row)
            es.append(e.astype(BF16))
            inv_dens.append(1.0 / den)
        p = jnp.concatenate(es, axis=1)
        o2 = jnp.dot(p, v_cat, preferred_element_type=F32)
        o2 = o2 * jnp.where(lane_lo, inv_dens[0], inv_dens[1])
        o_ref[rows, j0 * LANES:(j0 + 1) * LANES] = o2[:tq].astype(BF16)
        o_ref[rows, (j0 + 1) * LANES:(j0 + 2) * LANES] = o2[tq:].astype(BF16)


def _kv_cols(kh):
    return slice(kh * 2 * LANES, (kh + 1) * 2 * LANES)


def _ctx_attn_kernel(sink_ref, q_ref, k_ref, v_ref, o_ref):
    _attend(q_ref, o_ref, sink_ref,
            lambda kh: [k_ref[:, _kv_cols(kh)]],
            lambda kh: [v_ref[:, _kv_cols(kh)]],
            [None], SEQ)


def _ctx_attn(sink, q, kab, vab):
    return pl.pallas_call(
        _ctx_attn_kernel,
        grid=(BATCH,),
        in_specs=[
            pl.BlockSpec(memory_space=pltpu.SMEM),
            pl.BlockSpec((SEQ, D_Q), lambda b: (b, 0)),
            pl.BlockSpec((SEQ, D_KV_AB), lambda b: (b, 0)),
            pl.BlockSpec((SEQ, D_KV_AB), lambda b: (b, 0)),
        ],
        out_specs=pl.BlockSpec((SEQ, D_Q), lambda b: (b, 0)),
        out_shape=jax.ShapeDtypeStruct((N_PROMPT_TOK, D_Q), BF16),
        compiler_params=_params(),
        name="ctx_attn",
    )(sink, q, kab, vab)


N_QBLK = DEC_SEQ // BLOCK


LAT_SUB = 4
LAT_STEPS = N_QBLK // LAT_SUB


def _lat_attn_kernel(sink_ref, q_ref, kp_ref, kc_ref, kn_ref, vp_ref, vc_ref, vn_ref,
                     kx_ref, vx_ref, o_ref):
    step = pl.program_id(1)
    r = lax.broadcasted_iota(jnp.int32, (2 * BLOCK, BLOCK), 0) & (BLOCK - 1)
    c = lax.broadcasted_iota(jnp.int32, (2 * BLOCK, BLOCK), 1)

    def blocks(p_ref, c_ref, n_ref, kh):
        cols = _kv_cols(kh)
        return ([p_ref[:, cols]]
                + [c_ref[j * BLOCK:(j + 1) * BLOCK, cols] for j in range(LAT_SUB)]
                + [n_ref[:, cols]])

    for sub in range(LAT_SUB):
        no_prev = (step == 0) if sub == 0 else False
        no_next = (step == LAT_STEPS - 1) if sub == LAT_SUB - 1 else False
        m_prev = c >= r + jnp.where(no_prev, BLOCK, 0)
        m_next = c <= r - jnp.where(no_next, BLOCK, 0)

        def slabs(p_ref, c_ref, n_ref, x_ref, sub=sub):
            return lambda kh: blocks(p_ref, c_ref, n_ref, kh)[sub:sub + 3] + [x_ref[0, :, _kv_cols(kh)]]

        _attend(q_ref, o_ref, sink_ref,
                slabs(kp_ref, kc_ref, kn_ref, kx_ref), slabs(vp_ref, vc_ref, vn_ref, vx_ref),
                [m_prev, None, m_next, None], BLOCK, row0=sub * BLOCK)


def _lat_attn(sink, q, kab, vab, kab_ctx, vab_ctx):
    tq = LAT_SUB * BLOCK
    first = N_PROMPT_TOK // BLOCK

    def cur(b, i):
        return (N_PROMPT_TOK // tq + b * LAT_STEPS + i, 0)

    def prev(b, i):
        return (first + b * N_QBLK + jnp.maximum(i * LAT_SUB - 1, 0), 0)

    def nxt(b, i):
        return (first + b * N_QBLK + jnp.minimum((i + 1) * LAT_SUB, N_QBLK - 1), 0)

    edge_spec = lambda f: pl.BlockSpec((BLOCK, D_KV_AB), f)
    cur_spec = pl.BlockSpec((tq, D_KV_AB), cur)
    ctx_spec = pl.BlockSpec((1, PAST_LEN, D_KV_AB), lambda b, i: (b, 0, 0))
    return pl.pallas_call(
        _lat_attn_kernel,
        grid=(DEC_BATCH, LAT_STEPS),
        in_specs=[
            pl.BlockSpec(memory_space=pltpu.SMEM),
            pl.BlockSpec((tq, D_Q), cur),
            edge_spec(prev), cur_spec, edge_spec(nxt),
            edge_spec(prev), cur_spec, edge_spec(nxt),
            ctx_spec, ctx_spec,
        ],
        out_specs=pl.BlockSpec((tq, D_Q), lambda b, i: (b * LAT_STEPS + i, 0)),
        out_shape=jax.ShapeDtypeStruct((N_LATENT_TOK, D_Q), BF16),
        compiler_params=_params(2),
        name="lat_attn",
    )(sink, q, kab, kab, kab, vab, vab, vab, kab_ctx, vab_ctx)


def _proj_kernel(x_ref, ap_ref, al_ref, mod_ref, lng_ref, lnb_ref, wo_ref, o_ref):
    a = _read_split(ap_ref, al_ref)
    gate = mod_ref[5:6, :]
    for r in range(TM // EPILOGUE_ROWS):
        rows = slice(r * EPILOGUE_ROWS, (r + 1) * EPILOGUE_ROWS)
        f = jnp.dot(a[rows], wo_ref[...], preferred_element_type=F32)
        y = DEEPNORM_ALPHA * x_ref[rows, :] + gate * f
        o_ref[rows, :] = _layer_norm(y, lng_ref[...], lnb_ref[...])


def _proj(x, attn_prompt, attn_latent, mods, ln_g, ln_b, w_o):
    return pl.pallas_call(
        _proj_kernel,
        grid=(N_TOK // TM,),
        in_specs=[
            _slab_spec(D_MODEL),
            _prompt_spec(D_Q),
            _latent_spec(D_Q),
            _mod_spec(TM, 0)] + _ln_specs(0, 1) + [
            _resident((D_Q, D_MODEL)),
        ],
        out_specs=_slab_spec(D_MODEL),
        out_shape=jax.ShapeDtypeStruct((N_TOK, D_MODEL), F32),
        compiler_params=_params(),
        name="attn_proj",
    )(x, attn_prompt, attn_latent, mods, ln_g, ln_b, w_o)


PROMPT_POOL_TILES = N_PROMPT_TOK // POOL_TM
LATENT_POOL_TILES = DEC_SEQ // POOL_TM


def _pool_kernel(x_ref, xp_ref, xn_ref, mod_ref, lng_ref, lnb_ref, w_ref, sc_ref, o_ref):
    i = pl.program_id(0)
    in_seq = (i - PROMPT_POOL_TILES) % LATENT_POOL_TILES
    is_start = (i < PROMPT_POOL_TILES) | (in_seq == 0)
    is_end = (i < PROMPT_POOL_TILES) | (in_seq == LATENT_POOL_TILES - 1)

    x = x_ref[...]
    h = _modulate(x, mod_ref, 3)
    h_ext = jnp.concatenate([
        jnp.where(is_start, 0.0, _modulate(xp_ref[...], mod_ref, 3)), h,
        jnp.where(is_end, 0.0, _modulate(xn_ref[...], mod_ref, 3))], axis=0)
    n_ext = POOL_TM + 2 * POOL_HALO

    r8 = lax.broadcasted_iota(jnp.int32, (POOL_HALO, POOL_GROUP_DIM), 0)
    gate = mod_ref[5:6, :]
    outs = []
    for gi, w in enumerate(POOL_WINDOWS):
        half = w // 2
        cols = slice(gi * POOL_GROUP_DIM, (gi + 1) * POOL_GROUP_DIM)
        acc = h_ext[:, cols]
        s = 1
        while s < w:
            acc = acc + pltpu.roll(acc, s, 0)
            s *= 2
        if half > 1:
            acc = pltpu.roll(acc, n_ext - (half - 1), 0)
        total = acc[POOL_HALO:POOL_HALO + POOL_TM]
        cnt_top = w - jnp.where(is_start, jnp.maximum(half - r8, 0), 0)
        cnt_bot = w - jnp.where(is_end, jnp.maximum(r8 + half - POOL_HALO, 0), 0)
        inv_cnt = jnp.concatenate([
            1.0 / cnt_top.astype(F32),
            jnp.full((POOL_TM - 2 * POOL_HALO, POOL_GROUP_DIM), 1.0 / w, F32),
            1.0 / cnt_bot.astype(F32)], axis=0)
        pooled = (total * inv_cnt - h[:, cols]).astype(BF16)
        outs.append(jnp.dot(pooled, w_ref[gi], preferred_element_type=F32))
    y = DEEPNORM_ALPHA * x + (gate * sc_ref[...]) * jnp.concatenate(outs, axis=-1)
    o_ref[...] = _layer_norm(y, lng_ref[...], lnb_ref[...])


def _pool(x, mods, ln_g, ln_b, w_pool, scale):
    halo_per_tile = POOL_TM // POOL_HALO
    last_halo = N_TOK // POOL_HALO - 1
    return pl.pallas_call(
        _pool_kernel,
        grid=(N_TOK // POOL_TM,),
        in_specs=[
            pl.BlockSpec((POOL_TM, D_MODEL), lambda i: (i, 0)),
            pl.BlockSpec((POOL_HALO, D_MODEL), lambda i: (jnp.maximum(i * halo_per_tile - 1, 0), 0)),
            pl.BlockSpec((POOL_HALO, D_MODEL),
                         lambda i: (jnp.minimum((i + 1) * halo_per_tile, last_halo), 0)),
            _mod_spec(POOL_TM, 1)] + _ln_specs(1, 1) + [
            _resident((len(POOL_WINDOWS), POOL_GROUP_DIM, POOL_GROUP_DIM)),
            _resident((1, D_MODEL)),
        ],
        out_specs=pl.BlockSpec((POOL_TM, D_MODEL), lambda i: (i, 0)),
        out_shape=jax.ShapeDtypeStruct((N_TOK, D_MODEL), F32),
        compiler_params=_params(),
        name="pool",
    )(x, x, x, mods, ln_g, ln_b, w_pool, scale.reshape(1, D_MODEL))


def _ab_layout(a):
    lead = a.shape[:-1]
    a = a.reshape(lead + (N_KV_HEADS, 1, HEAD_DIM)).astype(BF16)
    z = jnp.zeros_like(a)
    return jnp.concatenate([a, z, z, a], axis=-2).reshape(lead + (D_KV_AB,))


def _rope_tables():
    n_rows = DEC_SEQ // GRID_W
    rows = jnp.repeat(jnp.arange(n_rows, dtype=F32), GRID_W)
    cols = jnp.tile(jnp.arange(GRID_W, dtype=F32), n_rows)
    inv = jnp.power(ROPE_BASE, -jnp.arange(N_FREQ, dtype=F32) / N_FREQ)
    ang_r = rows[:, None] * inv
    ang_c = cols[:, None] * inv
    ang = jnp.concatenate([ang_r, ang_r, ang_c, ang_c], axis=-1)
    cos = jnp.tile(jnp.cos(ang), (1, LANES // HEAD_DIM))
    sin = jnp.tile(jnp.sin(ang), (1, LANES // HEAD_DIM))
    first_half = (jnp.arange(LANES) % 32) < 16
    s_up = jnp.where(first_half, -sin, 0.0)
    s_dn = jnp.where(first_half, 0.0, sin)
    ident = jnp.ones((TM, LANES), F32)
    zeros = jnp.zeros((TM, LANES), F32)
    return (jnp.concatenate([cos, ident], axis=0),
            jnp.concatenate([s_up, zeros], axis=0),
            jnp.concatenate([s_dn, zeros], axis=0))


def kernel(x_prompt, x_sample, cache_k, cache_v, c, c_ctx, w_mod, b_mod, ln_g, ln_b,
           ffn_w_gate, ffn_w_up, ffn_w_down, attn_w_qkv, attn_w_o, attn_sink,
           pool_w, pool_scale):
    cond = jnp.concatenate(
        [c_ctx[None, :], c, jnp.zeros((COND_ROWS - N_COND, D_MODEL), F32)], axis=0)
    mods = _adaln(cond, w_mod, b_mod)

    ffn_w = (ffn_w_gate, ffn_w_up, ffn_w_down)
    ln_g = ln_g.reshape(DEPTH * 3, 1, D_MODEL)
    ln_b = ln_b.reshape(DEPTH * 3, 1, D_MODEL)
    ffn = functools.partial(_ffn, mods=mods, ln_g=ln_g, ln_b=ln_b)

    w_first = tuple(w[0, 0].astype(BF16) for w in ffn_w)
    x, w_next = ffn((x_prompt.reshape(N_PROMPT_TOK, D_MODEL), x_sample.reshape(N_LATENT_TOK, D_MODEL)),
                    weights=w_first, layer=0, which=0, next_f32=(ffn_w, (0, 1)))
    cos_t, sup_t, sdn_t = _rope_tables()
    q, kab, vab, k_state, v_state = _qkv(x, mods, attn_w_qkv[0].astype(BF16), cos_t, sup_t, sdn_t)
    sink = attn_sink[0]
    o_ctx = _ctx_attn(sink, q, kab, vab)
    kab_ctx = _ab_layout(cache_k[:, 0].reshape(DEC_BATCH, PAST_LEN, D_KV))
    vab_ctx = _ab_layout(cache_v[:, 0].reshape(DEC_BATCH, PAST_LEN, D_KV))
    o_lat = _lat_attn(sink, q, kab, vab, kab_ctx, vab_ctx)
    x = _proj(x, o_ctx, o_lat, mods, ln_g, ln_b, attn_w_o[0].astype(BF16))
    x, w_next = ffn((x,), weights=w_next, layer=0, which=1, next_f32=(ffn_w, (1, 0)))

    x, w_next = ffn((x,), weights=w_next, layer=1, which=0, next_f32=(ffn_w, (1, 1)))
    x = _pool(x, mods, ln_g, ln_b, pool_w[0].astype(BF16), pool_scale[0])
    (y_prompt, y_sample), _ = ffn((x,), weights=w_next, layer=1, which=1, split_out=True)

    def state(s_t):
        s_t = s_t.reshape(BATCH, N_KV_HEADS, HEAD_DIM, SEQ)
        return jnp.transpose(s_t, (0, 3, 1, 2))[:, None]

    return (y_prompt.reshape(BATCH, SEQ, D_MODEL),
            y_sample.reshape(DEC_BATCH, DEC_SEQ, D_MODEL),
            state(k_state), state(v_state))
```

```python
import functools

import jax
import jax.numpy as jnp
from jax import lax
from jax.experimental import pallas as pl
from jax.experimental.pallas import tpu as pltpu

D_MODEL = 1024
BATCH = 16
SEQ = 256
DEPTH = 2
DEC_BATCH = 2
DEC_SEQ = 4096
PAST_LEN = 256
GRID_W = 64
N_HEADS = 16
N_KV_HEADS = 4
HEAD_DIM = 64
WINDOW = 128
BLOCK = 128
ROPE_BASE = 10000.0
N_FREQ = HEAD_DIM // 4
POOL_WINDOWS = (2, 4, 8, 16)
POOL_GROUP_DIM = D_MODEL // 4
D_FF = 2816
N_MOD = 9
LN_EPS = 1e-5
DEEPNORM_ALPHA = (2.0 * DEPTH) ** 0.25
ATTN_SCALE = HEAD_DIM ** -0.5
LOG2_E = 1.4426950408889634
Q_SCALE = ATTN_SCALE * LOG2_E
NEG_INF = -1e30

N_PROMPT_TOK = BATCH * SEQ
N_LATENT_TOK = DEC_BATCH * DEC_SEQ
N_TOK = N_PROMPT_TOK + N_LATENT_TOK
MOD_GROUP_ROWS = 4096
N_COND = 1 + DEC_BATCH
COND_ROWS = 8

LANES = 128
HALF_LANES = LANES // 2
TM = 1024
N_PROMPT_TILES = N_PROMPT_TOK // TM
FF_CHUNK = 256
EPILOGUE_ROWS = 256
POOL_TM = 256
POOL_HALO = 8
D_Q = N_HEADS * HEAD_DIM
D_KV = N_KV_HEADS * HEAD_DIM
D_KV_AB = N_KV_HEADS * 2 * LANES
VMEM_LIMIT = 40 * 1024 * 1024
FFN_VMEM_LIMIT = 55 * 1024 * 1024

F32 = jnp.float32
BF16 = jnp.bfloat16


def _layer_norm(y, g, b):
    mu = jnp.mean(y, axis=-1, keepdims=True)
    yc = y - mu
    var = jnp.mean(yc * yc, axis=-1, keepdims=True)
    return yc * lax.rsqrt(var + LN_EPS) * g + b


def _modulate(x, mod_ref, row0):
    shift = mod_ref[row0:row0 + 1, :]
    scale = mod_ref[row0 + 1:row0 + 2, :]
    return x * (1.0 + scale) + shift


def _params(n_axes=1, vmem_limit=VMEM_LIMIT):
    return pltpu.CompilerParams(
        dimension_semantics=("arbitrary",) * n_axes,
        vmem_limit_bytes=vmem_limit)


def _resident(shape):
    nd = len(shape)
    return pl.BlockSpec(shape, lambda *_: (0,) * nd, pipeline_mode=pl.Buffered(1))


def _resident_at(index, tail):
    return pl.BlockSpec((None,) * len(index) + tuple(tail),
                        lambda *_: tuple(index) + (0,) * len(tail), pipeline_mode=pl.Buffered(1))


def _mod_spec(tile_rows, layer):
    tiles_per_group = MOD_GROUP_ROWS // tile_rows
    return pl.BlockSpec((None, None, N_MOD, D_MODEL),
                        lambda i: (layer, i // tiles_per_group, 0, 0))


def _ln_specs(layer, which):
    return [_resident_at((layer * 3 + which,), (1, D_MODEL))] * 2


def _slab_spec(cols, tm=TM):
    return pl.BlockSpec((tm, cols), lambda i: (i, 0))


def _prompt_spec(cols, tm=TM):
    return pl.BlockSpec((tm, cols), lambda i: (jnp.minimum(i, N_PROMPT_TOK // tm - 1), 0))


def _latent_spec(cols, tm=TM):
    return pl.BlockSpec((tm, cols), lambda i: (jnp.maximum(i - N_PROMPT_TOK // tm, 0), 0))


def _read_split(p_ref, l_ref, n_prompt_tiles=N_PROMPT_TILES):
    return jnp.where(pl.program_id(0) < n_prompt_tiles, p_ref[...], l_ref[...])


ADALN_NC = 1024


def _adaln_kernel(cond_ref, w_ref, b_ref, o_ref):
    c = cond_ref[...]
    s = (c * (1.0 / (1.0 + jnp.exp(-c)))).astype(BF16)
    o_ref[0] = jnp.dot(s, w_ref[0].astype(BF16), preferred_element_type=F32) + b_ref[0]


def _adaln(cond, w_mod, b_mod):
    n_out = N_MOD * D_MODEL
    out = pl.pallas_call(
        _adaln_kernel,
        grid=(DEPTH, n_out // ADALN_NC),
        in_specs=[
            pl.BlockSpec((COND_ROWS, D_MODEL), lambda l, n: (0, 0)),
            pl.BlockSpec((1, D_MODEL, ADALN_NC), lambda l, n: (l, 0, n)),
            pl.BlockSpec((1, 1, ADALN_NC), lambda l, n: (l, 0, n)),
        ],
        out_specs=pl.BlockSpec((1, COND_ROWS, ADALN_NC), lambda l, n: (l, 0, n)),
        out_shape=jax.ShapeDtypeStruct((DEPTH, COND_ROWS, n_out), F32),
        compiler_params=_params(2),
        name="adaln",
    )(cond, w_mod, b_mod.reshape(DEPTH, 1, n_out))
    return out[:, :N_COND].reshape(DEPTH, N_COND, N_MOD, D_MODEL)


def _ffn_body(x, mod_ref, lng_ref, lnb_ref, wg_ref, wu_ref, wd_ref, a_ref, row0):
    h = _modulate(x, mod_ref, row0).astype(BF16)
    for c in range(D_FF // FF_CHUNK):
        sl = slice(c * FF_CHUNK, (c + 1) * FF_CHUNK)
        g = jnp.dot(h, wg_ref[:, sl], preferred_element_type=F32)
        u = jnp.dot(h, wu_ref[:, sl], preferred_element_type=F32)
        a_ref[:, sl] = (g * (1.0 / (1.0 + jnp.exp(-g))) * u).astype(BF16)
    f = jnp.dot(a_ref[...], wd_ref[...], preferred_element_type=F32)
    gate = mod_ref[row0 + 2:row0 + 3, :]
    y = DEEPNORM_ALPHA * x + (0.5 * gate) * f
    return _layer_norm(y, lng_ref[...], lnb_ref[...])


CAST_STEPS = 8
FFN_WEIGHT_SHAPES = ((D_MODEL, D_FF), (D_MODEL, D_FF), (D_FF, D_MODEL))


def _weight_chunk_copy(src_ref, lead, stage_ref, sem_ref, chunk, slot):
    rows = stage_ref.shape[1]
    src = src_ref.at[lead[0], lead[1], pl.ds(chunk * rows, rows), :]
    return pltpu.make_async_copy(src, stage_ref.at[slot], sem_ref.at[slot])


def _load_own_weights(lead, srcs, dsts, stages, sems):
    jobs, used = [], {}
    for src, dst in zip(srcs, dsts):
        key = dst.shape[1]
        for chunk in range(CAST_STEPS):
            slot = used.get(key, 0) % 2
            used[key] = used.get(key, 0) + 1
            jobs.append((_weight_chunk_copy(src, lead, stages[key], sems[key], chunk, slot),
                         stages[key], slot, dst, chunk))
    jobs[0][0].start()
    for n, (copy, stage, slot, dst, chunk) in enumerate(jobs):
        if n + 1 < len(jobs):
            jobs[n + 1][0].start()
        copy.wait()
        rows = stage.shape[1]
        dst[chunk * rows:(chunk + 1) * rows, :] = stage[slot].astype(BF16)


def _ffn_kernel(*refs, row0, n_prompt_tiles, split_in, split_out, cast_next, own_lead):
    n_x = 2 if split_in else 1
    n_cast = 3 if cast_next else 0
    n_own = 7 if own_lead is not None else 0
    own = refs[len(refs) - n_own:]
    refs = refs[:len(refs) - n_own]
    x_refs, (mod_ref, lng_ref, lnb_ref, wg_ref, wu_ref, wd_ref) = refs[:n_x], refs[n_x:n_x + 6]
    next_f32 = refs[n_x + 6:n_x + 6 + n_cast]
    o_refs, next_bf16, a_ref = refs[n_x + 6 + n_cast:-1 - n_cast], refs[-1 - n_cast:-1], refs[-1]
    step = pl.program_id(0)
    if own_lead is not None:
        own_w, stage_a, stage_b, sem_a, sem_b = own[:3], own[3], own[4], own[5], own[6]
        stages = {stage_a.shape[2]: stage_a, stage_b.shape[2]: stage_b}
        sems = {stage_a.shape[2]: sem_a, stage_b.shape[2]: sem_b}

        @pl.when(step == 0)
        def _():
            _load_own_weights(own_lead, (wg_ref, wu_ref, wd_ref), own_w, stages, sems)

        wg_ref, wu_ref, wd_ref = own_w
    x = _read_split(*x_refs, n_prompt_tiles) if split_in else x_refs[0][...]
    out = _ffn_body(x, mod_ref, lng_ref, lnb_ref, wg_ref, wu_ref, wd_ref, a_ref, row0)
    if split_out:
        @pl.when(step < n_prompt_tiles)
        def _():
            o_refs[0][...] = out

        @pl.when(step >= n_prompt_tiles)
        def _():
            o_refs[1][...] = out
    else:
        o_refs[0][...] = out
    if cast_next:
        @pl.when(step < CAST_STEPS)
        def _():
            for src, dst in zip(next_f32, next_bf16):
                dst[...] = src[...].astype(BF16)


def _cast_block_spec(shape, lead):
    block = (None,) * len(lead) + (shape[0] // CAST_STEPS, shape[1])
    return pl.BlockSpec(block, lambda i: tuple(lead) + (jnp.minimum(i, CAST_STEPS - 1), 0))


def _ffn(xs, mods, ln_g, ln_b, weights, layer, which, next_f32=None, split_out=False):
    split_in = len(xs) == 2
    cast_next = next_f32 is not None
    own_lead = (layer, which) if weights is None else None
    row0 = 6 * which
    ln_row = 2 * which
    tm = TM // 2 if split_in and cast_next else TM
    x_specs = ([_prompt_spec(D_MODEL, tm), _latent_spec(D_MODEL, tm)] if split_in
               else [_slab_spec(D_MODEL, tm)])
    if split_out:
        out_specs = [_prompt_spec(D_MODEL, tm), _latent_spec(D_MODEL, tm)]
        out_shape = [jax.ShapeDtypeStruct((N_PROMPT_TOK, D_MODEL), F32),
                     jax.ShapeDtypeStruct((N_LATENT_TOK, D_MODEL), F32)]
    else:
        out_specs = [_slab_spec(D_MODEL, tm)]
        out_shape = [jax.ShapeDtypeStruct((N_TOK, D_MODEL), F32)]
    n_x_out = len(out_specs)
    cast_in_specs, cast_args = [], ()
    if cast_next:
        cast_args, lead = next_f32
        cast_in_specs = [_cast_block_spec(s, lead) for s in FFN_WEIGHT_SHAPES]
        out_specs = out_specs + [_cast_block_spec(s, ()) for s in FFN_WEIGHT_SHAPES]
        out_shape = out_shape + [jax.ShapeDtypeStruct(s, BF16) for s in FFN_WEIGHT_SHAPES]
    scratch = [pltpu.VMEM((tm, D_FF), BF16)]
    if own_lead is None:
        weight_specs = [_resident(s) for s in FFN_WEIGHT_SHAPES]
    else:
        weights = next_f32[0]
        weight_specs = [pl.BlockSpec(memory_space=pl.ANY)] * 3
        stage_shapes = sorted({(2, s[0] // CAST_STEPS, s[1]) for s in FFN_WEIGHT_SHAPES})
        scratch += [pltpu.VMEM(s, BF16) for s in FFN_WEIGHT_SHAPES]
        scratch += [pltpu.VMEM(s, F32) for s in stage_shapes]
        scratch += [pltpu.SemaphoreType.DMA((2,)) for _ in stage_shapes]
    outs = pl.pallas_call(
        functools.partial(_ffn_kernel, row0=row0, n_prompt_tiles=N_PROMPT_TOK // tm,
                          split_in=split_in, split_out=split_out, cast_next=cast_next,
                          own_lead=own_lead),
        grid=(N_TOK // tm,),
        in_specs=x_specs + [_mod_spec(tm, layer)] + _ln_specs(layer, ln_row)
        + weight_specs + cast_in_specs,
        out_specs=out_specs,
        out_shape=out_shape,
        scratch_shapes=scratch,
        compiler_params=_params(vmem_limit=FFN_VMEM_LIMIT),
        name="ffn",
    )(*xs, mods, ln_g, ln_b, *weights, *cast_args)
    x_out = tuple(outs[:n_x_out]) if split_out else outs[0]
    return x_out, (tuple(outs[n_x_out:]) if cast_next else None)


ROPE_TILES = DEC_SEQ // TM


def _write_ab(dst_ref, rows, pair, src):
    lo = lax.broadcasted_iota(jnp.int32, src.shape, 1) < HALF_LANES
    swapped = pltpu.roll(src, HALF_LANES, 1)
    blocks = (jnp.where(lo, src, 0.0), jnp.where(lo, 0.0, swapped),
              jnp.where(lo, swapped, 0.0), jnp.where(lo, 0.0, src))
    for n, blk in enumerate(blocks):
        c0 = (4 * pair + n) * LANES
        dst_ref[rows, c0:c0 + LANES] = blk.astype(BF16)


def _qkv_kernel(x_ref, mod_ref, w_ref, cos_ref, sup_ref, sdn_ref,
                q_ref, kab_ref, vab_ref, ks_ref, vs_ref):
    kv_raw = []
    for r in range(TM // EPILOGUE_ROWS):
        rows = slice(r * EPILOGUE_ROWS, (r + 1) * EPILOGUE_ROWS)
        h = _modulate(x_ref[rows, :], mod_ref, 3).astype(BF16)
        qkv = jnp.dot(h, w_ref[...], preferred_element_type=F32)
        cos = cos_ref[rows, :]
        s_up = sup_ref[rows, :]
        s_dn = sdn_ref[rows, :]

        def rope(blk):
            up = pltpu.roll(blk, LANES - 16, 1)
            dn = pltpu.roll(blk, 16, 1)
            return blk * cos + up * s_up + dn * s_dn

        for j in range(D_Q // LANES):
            cols = slice(j * LANES, (j + 1) * LANES)
            q_ref[rows, cols] = (rope(qkv[:, cols]) * Q_SCALE).astype(BF16)
        for pair in range(D_KV // LANES):
            k_cols = slice(D_Q + pair * LANES, D_Q + (pair + 1) * LANES)
            v_cols = slice(D_Q + D_KV + pair * LANES, D_Q + D_KV + (pair + 1) * LANES)
            _write_ab(kab_ref, rows, pair, rope(qkv[:, k_cols]))
            _write_ab(vab_ref, rows, pair, qkv[:, v_cols])
        kv_raw.append(qkv[:, D_Q:])

    @pl.when(pl.program_id(0) < N_PROMPT_TILES)
    def _():
        kv_t = jnp.concatenate(kv_raw, axis=0).T
        for s in range(TM // SEQ):
            ks_ref[s] = kv_t[:D_KV, s * SEQ:(s + 1) * SEQ]
            vs_ref[s] = kv_t[D_KV:, s * SEQ:(s + 1) * SEQ]


def _rope_index(i):
    return (jnp.where(i < N_PROMPT_TILES, ROPE_TILES, (i - N_PROMPT_TILES) % ROPE_TILES), 0)


def _qkv(x, mods, w, cos_t, sup_t, sdn_t):
    seqs_per_tile = TM // SEQ
    state_spec = pl.BlockSpec((seqs_per_tile, D_KV, SEQ),
                              lambda i: (jnp.minimum(i, N_PROMPT_TILES - 1), 0, 0))
    return pl.pallas_call(
        _qkv_kernel,
        grid=(N_TOK // TM,),
        in_specs=[
            _slab_spec(D_MODEL),
            _mod_spec(TM, 0),
            _resident((D_MODEL, D_Q + 2 * D_KV)),
            pl.BlockSpec((TM, LANES), _rope_index),
            pl.BlockSpec((TM, LANES), _rope_index),
            pl.BlockSpec((TM, LANES), _rope_index),
        ],
        out_specs=[
            _slab_spec(D_Q), _slab_spec(D_KV_AB), _slab_spec(D_KV_AB), state_spec, state_spec,
        ],
        out_shape=[
            jax.ShapeDtypeStruct((N_TOK, D_Q), BF16),
            jax.ShapeDtypeStruct((N_TOK, D_KV_AB), BF16),
            jax.ShapeDtypeStruct((N_TOK, D_KV_AB), BF16),
            jax.ShapeDtypeStruct((BATCH, D_KV, SEQ), F32),
            jax.ShapeDtypeStruct((BATCH, D_KV, SEQ), F32),
        ],
        compiler_params=_params(),
        name="qkv",
    )(x, mods, w, cos_t, sup_t, sdn_t)


def _attend(q_ref, o_ref, sink_ref, k_slabs, v_slabs, masks, tq, row0=0):
    nt = (((1,), (1,)), ((), ()))
    rows = slice(row0, row0 + tq)
    row_hi = lax.broadcasted_iota(jnp.int32, (2 * tq, 1), 0) >= tq
    lane_lo = lax.broadcasted_iota(jnp.int32, (2 * tq, LANES), 1) < HALF_LANES
    for kh in range(N_KV_HEADS):
        ks = k_slabs(kh)
        vs = v_slabs(kh)
        k_cat = jnp.concatenate([s[:, :LANES] for s in ks] + [s[:, LANES:] for s in ks], axis=0)
        v_cat = jnp.concatenate([s[:, :LANES] for s in vs] + [s[:, LANES:] for s in vs], axis=0)
        n_keys = k_cat.shape[0] // 2
        j0 = 2 * kh
        q2 = jnp.concatenate([q_ref[rows, j0 * LANES:(j0 + 1) * LANES],
                              q_ref[rows, (j0 + 1) * LANES:(j0 + 2) * LANES]], axis=0)
        s = lax.dot_general(q2, k_cat, nt, preferred_element_type=F32)
        es, inv_dens = [], []
        for half in range(2):
            segs, off = [], half * n_keys
            for slab, m in zip(ks, masks):
                seg = s[:, off:off + slab.shape[0]]
                segs.append(seg if m is None else jnp.where(m, seg, NEG_INF))
                off += slab.shape[0]
            logits = jnp.concatenate(segs, axis=1)
            sink = LOG2_E * jnp.where(row_hi, sink_ref[4 * kh + 2 + half], sink_ref[4 * kh + half])
            m_row = jnp.maximum(jnp.max(logits, axis=-1, keepdims=True), sink)
            e = jnp.exp2(logits - m_row)
            den = jnp.sum(e, axis=-1, keepdims=True) + jnp.exp2(sink - m_row)
            es.append(e.astype(BF16))
            inv_dens.append(1.0 / den)
        p = jnp.concatenate(es, axis=1)
        o2 = jnp.dot(p, v_cat, preferred_element_type=F32)
        o2 = o2 * jnp.where(lane_lo, inv_dens[0], inv_dens[1])
        o_ref[rows, j0 * LANES:(j0 + 1) * LANES] = o2[:tq].astype(BF16)
        o_ref[rows, (j0 + 1) * LANES:(j0 + 2) * LANES] = o2[tq:].astype(BF16)


def _kv_cols(kh):
    return slice(kh * 2 * LANES, (kh + 1) * 2 * LANES)


def _ctx_attn_kernel(sink_ref, q_ref, k_ref, v_ref, o_ref):
    _attend(q_ref, o_ref, sink_ref,
            lambda kh: [k_ref[:, _kv_cols(kh)]],
            lambda kh: [v_ref[:, _kv_cols(kh)]],
            [None], SEQ)


def _ctx_attn(sink, q, kab, vab):
    return pl.pallas_call(
        _ctx_attn_kernel,
        grid=(BATCH,),
        in_specs=[
            pl.BlockSpec(memory_space=pltpu.SMEM),
            pl.BlockSpec((SEQ, D_Q), lambda b: (b, 0)),
            pl.BlockSpec((SEQ, D_KV_AB), lambda b: (b, 0)),
            pl.BlockSpec((SEQ, D_KV_AB), lambda b: (b, 0)),
        ],
        out_specs=pl.BlockSpec((SEQ, D_Q), lambda b: (b, 0)),
        out_shape=jax.ShapeDtypeStruct((N_PROMPT_TOK, D_Q), BF16),
        compiler_params=_params(),
        name="ctx_attn",
    )(sink, q, kab, vab)


N_QBLK = DEC_SEQ // BLOCK


LAT_SUB = 4
LAT_STEPS = N_QBLK // LAT_SUB


def _lat_attn_kernel(sink_ref, q_ref, kp_ref, kc_ref, kn_ref, vp_ref, vc_ref, vn_ref,
                     kx_ref, vx_ref, o_ref):
    step = pl.program_id(1)
    r = lax.broadcasted_iota(jnp.int32, (2 * BLOCK, BLOCK), 0) & (BLOCK - 1)
    c = lax.broadcasted_iota(jnp.int32, (2 * BLOCK, BLOCK), 1)

    def blocks(p_ref, c_ref, n_ref, kh):
        cols = _kv_cols(kh)
        return ([p_ref[:, cols]]
                + [c_ref[j * BLOCK:(j + 1) * BLOCK, cols] for j in range(LAT_SUB)]
                + [n_ref[:, cols]])

    for sub in range(LAT_SUB):
        no_prev = (step == 0) if sub == 0 else False
        no_next = (step == LAT_STEPS - 1) if sub == LAT_SUB - 1 else False
        m_prev = c >= r + jnp.where(no_prev, BLOCK, 0)
        m_next = c <= r - jnp.where(no_next, BLOCK, 0)

        def slabs(p_ref, c_ref, n_ref, x_ref, sub=sub):
            return lambda kh: blocks(p_ref, c_ref, n_ref, kh)[sub:sub + 3] + [x_ref[0, :, _kv_cols(kh)]]

        _attend(q_ref, o_ref, sink_ref,
                slabs(kp_ref, kc_ref, kn_ref, kx_ref), slabs(vp_ref, vc_ref, vn_ref, vx_ref),
                [m_prev, None, m_next, None], BLOCK, row0=sub * BLOCK)


def _lat_attn(sink, q, kab, vab, kab_ctx, vab_ctx):
    tq = LAT_SUB * BLOCK
    first = N_PROMPT_TOK // BLOCK

    def cur(b, i):
        return (N_PROMPT_TOK // tq + b * LAT_STEPS + i, 0)

    def prev(b, i):
        return (first + b * N_QBLK + jnp.maximum(i * LAT_SUB - 1, 0), 0)

    def nxt(b, i):
        return (first + b * N_QBLK + jnp.minimum((i + 1) * LAT_SUB, N_QBLK - 1), 0)

    edge_spec = lambda f: pl.BlockSpec((BLOCK, D_KV_AB), f)
    cur_spec = pl.BlockSpec((tq, D_KV_AB), cur)
    ctx_spec = pl.BlockSpec((1, PAST_LEN, D_KV_AB), lambda b, i: (b, 0, 0))
    return pl.pallas_call(
        _lat_attn_kernel,
        grid=(DEC_BATCH, LAT_STEPS),
        in_specs=[
            pl.BlockSpec(memory_space=pltpu.SMEM),
            pl.BlockSpec((tq, D_Q), cur),
            edge_spec(prev), cur_spec, edge_spec(nxt),
            edge_spec(prev), cur_spec, edge_spec(nxt),
            ctx_spec, ctx_spec,
        ],
        out_specs=pl.BlockSpec((tq, D_Q), lambda b, i: (b * LAT_STEPS + i, 0)),
        out_shape=jax.ShapeDtypeStruct((N_LATENT_TOK, D_Q), BF16),
        compiler_params=_params(2),
        name="lat_attn",
    )(sink, q, kab, kab, kab, vab, vab, vab, kab_ctx, vab_ctx)


def _proj_kernel(x_ref, ap_ref, al_ref, mod_ref, lng_ref, lnb_ref, wo_ref, o_ref):
    a = _read_split(ap_ref, al_ref)
    gate = mod_ref[5:6, :]
    for r in range(TM // EPILOGUE_ROWS):
        rows = slice(r * EPILOGUE_ROWS, (r + 1) * EPILOGUE_ROWS)
        f = jnp.dot(a[rows], wo_ref[...], preferred_element_type=F32)
        y = DEEPNORM_ALPHA * x_ref[rows, :] + gate * f
        o_ref[rows, :] = _layer_norm(y, lng_ref[...], lnb_ref[...])


def _proj(x, attn_prompt, attn_latent, mods, ln_g, ln_b, w_o):
    return pl.pallas_call(
        _proj_kernel,
        grid=(N_TOK // TM,),
        in_specs=[
            _slab_spec(D_MODEL),
            _prompt_spec(D_Q),
            _latent_spec(D_Q),
            _mod_spec(TM, 0)] + _ln_specs(0, 1) + [
            _resident((D_Q, D_MODEL)),
        ],
        out_specs=_slab_spec(D_MODEL),
        out_shape=jax.ShapeDtypeStruct((N_TOK, D_MODEL), F32),
        compiler_params=_params(),
        name="attn_proj",
    )(x, attn_prompt, attn_latent, mods, ln_g, ln_b, w_o)


PROMPT_POOL_TILES = N_PROMPT_TOK // POOL_TM
LATENT_POOL_TILES = DEC_SEQ // POOL_TM


def _pool_kernel(x_ref, xp_ref, xn_ref, mod_ref, lng_ref, lnb_ref, w_ref, sc_ref, o_ref):
    i = pl.program_id(0)
    in_seq = (i - PROMPT_POOL_TILES) % LATENT_POOL_TILES
    is_start = (i < PROMPT_POOL_TILES) | (in_seq == 0)
    is_end = (i < PROMPT_POOL_TILES) | (in_seq == LATENT_POOL_TILES - 1)

    x = x_ref[...]
    h = _modulate(x, mod_ref, 3)
    h_ext = jnp.concatenate([
        jnp.where(is_start, 0.0, _modulate(xp_ref[...], mod_ref, 3)), h,
        jnp.where(is_end, 0.0, _modulate(xn_ref[...], mod_ref, 3))], axis=0)
    n_ext = POOL_TM + 2 * POOL_HALO

    r8 = lax.broadcasted_iota(jnp.int32, (POOL_HALO, POOL_GROUP_DIM), 0)
    gate = mod_ref[5:6, :]
    outs = []
    for gi, w in enumerate(POOL_WINDOWS):
        half = w // 2
        cols = slice(gi * POOL_GROUP_DIM, (gi + 1) * POOL_GROUP_DIM)
        acc = h_ext[:, cols]
        s = 1
        while s < w:
            acc = acc + pltpu.roll(acc, s, 0)
            s *= 2
        if half > 1:
            acc = pltpu.roll(acc, n_ext - (half - 1), 0)
        total = acc[POOL_HALO:POOL_HALO + POOL_TM]
        cnt_top = w - jnp.where(is_start, jnp.maximum(half - r8, 0), 0)
        cnt_bot = w - jnp.where(is_end, jnp.maximum(r8 + half - POOL_HALO, 0), 0)
        inv_cnt = jnp.concatenate([
            1.0 / cnt_top.astype(F32),
            jnp.full((POOL_TM - 2 * POOL_HALO, POOL_GROUP_DIM), 1.0 / w, F32),
            1.0 / cnt_bot.astype(F32)], axis=0)
        pooled = (total * inv_cnt - h[:, cols]).astype(BF16)
        outs.append(jnp.dot(pooled, w_ref[gi], preferred_element_type=F32))
    y = DEEPNORM_ALPHA * x + (gate * sc_ref[...]) * jnp.concatenate(outs, axis=-1)
    o_ref[...] = _layer_norm(y, lng_ref[...], lnb_ref[...])


def _pool(x, mods, ln_g, ln_b, w_pool, scale):
    halo_per_tile = POOL_TM // POOL_HALO
    last_halo = N_TOK // POOL_HALO - 1
    return pl.pallas_call(
        _pool_kernel,
        grid=(N_TOK // POOL_TM,),
        in_specs=[
            pl.BlockSpec((POOL_TM, D_MODEL), lambda i: (i, 0)),
            pl.BlockSpec((POOL_HALO, D_MODEL), lambda i: (jnp.maximum(i * halo_per_tile - 1, 0), 0)),
            pl.BlockSpec((POOL_HALO, D_MODEL),
                         lambda i: (jnp.minimum((i + 1) * halo_per_tile, last_halo), 0)),
            _mod_spec(POOL_TM, 1)] + _ln_specs(1, 1) + [
            _resident((len(POOL_WINDOWS), POOL_GROUP_DIM, POOL_GROUP_DIM)),
            _resident((1, D_MODEL)),
        ],
        out_specs=pl.BlockSpec((POOL_TM, D_MODEL), lambda i: (i, 0)),
        out_shape=jax.ShapeDtypeStruct((N_TOK, D_MODEL), F32),
        compiler_params=_params(),
        name="pool",
    )(x, x, x, mods, ln_g, ln_b, w_pool, scale.reshape(1, D_MODEL))


def _ab_layout(a):
    lead = a.shape[:-1]
    a = a.reshape(lead + (N_KV_HEADS, 1, HEAD_DIM)).astype(BF16)
    z = jnp.zeros_like(a)
    return jnp.concatenate([a, z, z, a], axis=-2).reshape(lead + (D_KV_AB,))


def _rope_tables():
    n_rows = DEC_SEQ // GRID_W
    rows = jnp.repeat(jnp.arange(n_rows, dtype=F32), GRID_W)
    cols = jnp.tile(jnp.arange(GRID_W, dtype=F32), n_rows)
    inv = jnp.power(ROPE_BASE, -jnp.arange(N_FREQ, dtype=F32) / N_FREQ)
    ang_r = rows[:, None] * inv
    ang_c = cols[:, None] * inv
    ang = jnp.concatenate([ang_r, ang_r, ang_c, ang_c], axis=-1)
    cos = jnp.tile(jnp.cos(ang), (1, LANES // HEAD_DIM))
    sin = jnp.tile(jnp.sin(ang), (1, LANES // HEAD_DIM))
    first_half = (jnp.arange(LANES) % 32) < 16
    s_up = jnp.where(first_half, -sin, 0.0)
    s_dn = jnp.where(first_half, 0.0, sin)
    ident = jnp.ones((TM, LANES), F32)
    zeros = jnp.zeros((TM, LANES), F32)
    return (jnp.concatenate([cos, ident], axis=0),
            jnp.concatenate([s_up, zeros], axis=0),
            jnp.concatenate([s_dn, zeros], axis=0))


def kernel(x_prompt, x_sample, cache_k, cache_v, c, c_ctx, w_mod, b_mod, ln_g, ln_b,
           ffn_w_gate, ffn_w_up, ffn_w_down, attn_w_qkv, attn_w_o, attn_sink,
           pool_w, pool_scale):
    cond = jnp.concatenate(
        [c_ctx[None, :], c, jnp.zeros((COND_ROWS - N_COND, D_MODEL), F32)], axis=0)
    mods = _adaln(cond, w_mod, b_mod)

    ffn_w = (ffn_w_gate, ffn_w_up, ffn_w_down)
    ln_g = ln_g.reshape(DEPTH * 3, 1, D_MODEL)
    ln_b = ln_b.reshape(DEPTH * 3, 1, D_MODEL)
    ffn = functools.partial(_ffn, mods=mods, ln_g=ln_g, ln_b=ln_b)

    x, w_next = ffn((x_prompt.reshape(N_PROMPT_TOK, D_MODEL), x_sample.reshape(N_LATENT_TOK, D_MODEL)),
                    weights=None, layer=0, which=0, next_f32=(ffn_w, (0, 1)))
    cos_t, sup_t, sdn_t = _rope_tables()
    q, kab, vab, k_state, v_state = _qkv(x, mods, attn_w_qkv[0].astype(BF16), cos_t, sup_t, sdn_t)
    sink = attn_sink[0]
    o_ctx = _ctx_attn(sink, q, kab, vab)
    kab_ctx = _ab_layout(cache_k[:, 0].reshape(DEC_BATCH, PAST_LEN, D_KV))
    vab_ctx = _ab_layout(cache_v[:, 0].reshape(DEC_BATCH, PAST_LEN, D_KV))
    o_lat = _lat_attn(sink, q, kab, vab, kab_ctx, vab_ctx)
    x = _proj(x, o_ctx, o_lat, mods, ln_g, ln_b, attn_w_o[0].astype(BF16))
    x, w_next = ffn((x,), weights=w_next, layer=0, which=1, next_f32=(ffn_w, (1, 0)))

    x, w_next = ffn((x,), weights=w_next, layer=1, which=0, next_f32=(ffn_w, (1, 1)))
    x = _pool(x, mods, ln_g, ln_b, pool_w[0].astype(BF16), pool_scale[0])
    (y_prompt, y_sample), _ = ffn((x,), weights=w_next, layer=1, which=1, split_out=True)

    def state(s_t):
        s_t = s_t.reshape(BATCH, N_KV_HEADS, HEAD_DIM, SEQ)
        return jnp.transpose(s_t, (0, 3, 1, 2))[:, None]

    return (y_prompt.reshape(BATCH, SEQ, D_MODEL),
            y_sample.reshape(DEC_BATCH, DEC_SEQ, D_MODEL),
            state(k_state), state(v_state))
```

```python
import functools

import jax
import jax.numpy as jnp
from jax import lax
from jax.experimental import pallas as pl
from jax.experimental.pallas import tpu as pltpu

D_MODEL = 1024
BATCH = 16
SEQ = 256
DEPTH = 2
DEC_BATCH = 2
DEC_SEQ = 4096
PAST_LEN = 256
GRID_W = 64
N_HEADS = 16
N_KV_HEADS = 4
HEAD_DIM = 64
WINDOW = 128
BLOCK = 128
ROPE_BASE = 10000.0
N_FREQ = HEAD_DIM // 4
POOL_WINDOWS = (2, 4, 8, 16)
POOL_GROUP_DIM = D_MODEL // 4
D_FF = 2816
N_MOD = 9
LN_EPS = 1e-5
DEEPNORM_ALPHA = (2.0 * DEPTH) ** 0.25
ATTN_SCALE = HEAD_DIM ** -0.5
LOG2_E = 1.4426950408889634
Q_SCALE = ATTN_SCALE * LOG2_E
NEG_INF = -1e30

N_PROMPT_TOK = BATCH * SEQ
N_LATENT_TOK = DEC_BATCH * DEC_SEQ
N_TOK = N_PROMPT_TOK + N_LATENT_TOK
MOD_GROUP_ROWS = 4096
N_COND = 1 + DEC_BATCH
COND_ROWS = 8

LANES = 128
HALF_LANES = LANES // 2
TM = 1024
N_PROMPT_TILES = N_PROMPT_TOK // TM
FF_CHUNK = 256
EPILOGUE_ROWS = 256
POOL_TM = 256
POOL_HALO = 8
D_Q = N_HEADS * HEAD_DIM
D_KV = N_KV_HEADS * HEAD_DIM
D_KV_AB = N_KV_HEADS * 2 * LANES
VMEM_LIMIT = 40 * 1024 * 1024
FFN_VMEM_LIMIT = 55 * 1024 * 1024

F32 = jnp.float32
BF16 = jnp.bfloat16


def _layer_norm(y, g, b):
    mu = jnp.mean(y, axis=-1, keepdims=True)
    yc = y - mu
    var = jnp.mean(yc * yc, axis=-1, keepdims=True)
    return yc * lax.rsqrt(var + LN_EPS) * g + b


def _modulate(x, mod_ref, row0):
    shift = mod_ref[row0:row0 + 1, :]
    scale = mod_ref[row0 + 1:row0 + 2, :]
    return x * (1.0 + scale) + shift


def _params(n_axes=1, vmem_limit=VMEM_LIMIT):
    return pltpu.CompilerParams(
        dimension_semantics=("arbitrary",) * n_axes,
        vmem_limit_bytes=vmem_limit)


def _resident(shape):
    nd = len(shape)
    return pl.BlockSpec(shape, lambda *_: (0,) * nd, pipeline_mode=pl.Buffered(1))


def _resident_at(index, tail):
    return pl.BlockSpec((None,) * len(index) + tuple(tail),
                        lambda *_: tuple(index) + (0,) * len(tail), pipeline_mode=pl.Buffered(1))


def _mod_spec(tile_rows, layer):
    tiles_per_group = MOD_GROUP_ROWS // tile_rows
    return pl.BlockSpec((None, None, N_MOD, D_MODEL),
                        lambda i: (layer, i // tiles_per_group, 0, 0))


def _ln_specs(layer, which):
    return [_resident_at((layer * 3 + which,), (1, D_MODEL))] * 2


def _slab_spec(cols, tm=TM):
    return pl.BlockSpec((tm, cols), lambda i: (i, 0))


def _prompt_spec(cols, tm=TM):
    return pl.BlockSpec((tm, cols), lambda i: (jnp.minimum(i, N_PROMPT_TOK // tm - 1), 0))


def _latent_spec(cols, tm=TM):
    return pl.BlockSpec((tm, cols), lambda i: (jnp.maximum(i - N_PROMPT_TOK // tm, 0), 0))


def _read_split(p_ref, l_ref, n_prompt_tiles=N_PROMPT_TILES):
    return jnp.where(pl.program_id(0) < n_prompt_tiles, p_ref[...], l_ref[...])


ADALN_NC = 1024


def _adaln_kernel(cond_ref, w_ref, b_ref, o_ref):
    c = cond_ref[...]
    s = (c * (1.0 / (1.0 + jnp.exp(-c)))).astype(BF16)
    o_ref[0] = jnp.dot(s, w_ref[0].astype(BF16), preferred_element_type=F32) + b_ref[0]


def _adaln(cond, w_mod, b_mod):
    n_out = N_MOD * D_MODEL
    out = pl.pallas_call(
        _adaln_kernel,
        grid=(DEPTH, n_out // ADALN_NC),
        in_specs=[
            pl.BlockSpec((COND_ROWS, D_MODEL), lambda l, n: (0, 0)),
            pl.BlockSpec((1, D_MODEL, ADALN_NC), lambda l, n: (l, 0, n)),
            pl.BlockSpec((1, 1, ADALN_NC), lambda l, n: (l, 0, n)),
        ],
        out_specs=pl.BlockSpec((1, COND_ROWS, ADALN_NC), lambda l, n: (l, 0, n)),
        out_shape=jax.ShapeDtypeStruct((DEPTH, COND_ROWS, n_out), F32),
        compiler_params=_params(2),
        name="adaln",
    )(cond, w_mod, b_mod.reshape(DEPTH, 1, n_out))
    return out[:, :N_COND].reshape(DEPTH, N_COND, N_MOD, D_MODEL)


def _ffn_body(x, mod_ref, lng_ref, lnb_ref, wg_ref, wu_ref, wd_ref, a_ref, row0):
    h = _modulate(x, mod_ref, row0).astype(BF16)
    for c in range(D_FF // FF_CHUNK):
        sl = slice(c * FF_CHUNK, (c + 1) * FF_CHUNK)
        g = jnp.dot(h, wg_ref[:, sl], preferred_element_type=F32)
        u = jnp.dot(h, wu_ref[:, sl], preferred_element_type=F32)
        a_ref[:, sl] = (g * (1.0 / (1.0 + jnp.exp(-g))) * u).astype(BF16)
    f = jnp.dot(a_ref[...], wd_ref[...], preferred_element_type=F32)
    gate = mod_ref[row0 + 2:row0 + 3, :]
    y = DEEPNORM_ALPHA * x + (0.5 * gate) * f
    return _layer_norm(y, lng_ref[...], lnb_ref[...])


CAST_STEPS = 8
OWN_CHUNKS = 16
STAGE_SLOTS = 4
FFN_WEIGHT_SHAPES = ((D_MODEL, D_FF), (D_MODEL, D_FF), (D_FF, D_MODEL))


def _weight_chunk_copy(src_ref, lead, stage_ref, sem_ref, chunk, slot):
    rows = stage_ref.shape[1]
    src = src_ref.at[lead[0], lead[1], pl.ds(chunk * rows, rows), :]
    return pltpu.make_async_copy(src, stage_ref.at[slot], sem_ref.at[slot])


def _load_own_weights(lead, srcs, dsts, stages, sems):
    jobs, used = [], {}
    for src, dst in zip(srcs, dsts):
        key = dst.shape[1]
        for chunk in range(OWN_CHUNKS):
            slot = used.get(key, 0) % STAGE_SLOTS
            used[key] = used.get(key, 0) + 1
            jobs.append((_weight_chunk_copy(src, lead, stages[key], sems[key], chunk, slot),
                         stages[key], slot, dst, chunk))
    ahead = STAGE_SLOTS - 1
    for job in jobs[:ahead]:
        job[0].start()
    for n, (copy, stage, slot, dst, chunk) in enumerate(jobs):
        if n + ahead < len(jobs):
            jobs[n + ahead][0].start()
        copy.wait()
        rows = stage.shape[1]
        dst[chunk * rows:(chunk + 1) * rows, :] = stage[slot].astype(BF16)


def _ffn_kernel(*refs, row0, n_prompt_tiles, split_in, split_out, cast_next, own_lead):
    n_x = 2 if split_in else 1
    n_cast = 3 if cast_next else 0
    n_own = 7 if own_lead is not None else 0
    own = refs[len(refs) - n_own:]
    refs = refs[:len(refs) - n_own]
    x_refs, (mod_ref, lng_ref, lnb_ref, wg_ref, wu_ref, wd_ref) = refs[:n_x], refs[n_x:n_x + 6]
    next_f32 = refs[n_x + 6:n_x + 6 + n_cast]
    o_refs, next_bf16, a_ref = refs[n_x + 6 + n_cast:-1 - n_cast], refs[-1 - n_cast:-1], refs[-1]
    step = pl.program_id(0)
    if own_lead is not None:
        own_w, stage_a, stage_b, sem_a, sem_b = own[:3], own[3], own[4], own[5], own[6]
        stages = {stage_a.shape[2]: stage_a, stage_b.shape[2]: stage_b}
        sems = {stage_a.shape[2]: sem_a, stage_b.shape[2]: sem_b}

        @pl.when(step == 0)
        def _():
            _load_own_weights(own_lead, (wg_ref, wu_ref, wd_ref), own_w, stages, sems)

        wg_ref, wu_ref, wd_ref = own_w
    x = _read_split(*x_refs, n_prompt_tiles) if split_in else x_refs[0][...]
    out = _ffn_body(x, mod_ref, lng_ref, lnb_ref, wg_ref, wu_ref, wd_ref, a_ref, row0)
    if split_out:
        @pl.when(step < n_prompt_tiles)
        def _():
            o_refs[0][...] = out

        @pl.when(step >= n_prompt_tiles)
        def _():
            o_refs[1][...] = out
    else:
        o_refs[0][...] = out
    if cast_next:
        @pl.when(step < CAST_STEPS)
        def _():
            for src, dst in zip(next_f32, next_bf16):
                dst[...] = src[...].astype(BF16)


def _cast_block_spec(shape, lead):
    block = (None,) * len(lead) + (shape[0] // CAST_STEPS, shape[1])
    return pl.BlockSpec(block, lambda i: tuple(lead) + (jnp.minimum(i, CAST_STEPS - 1), 0))


def _ffn(xs, mods, ln_g, ln_b, weights, layer, which, next_f32=None, split_out=False):
    split_in = len(xs) == 2
    cast_next = next_f32 is not None
    own_lead = (layer, which) if weights is None else None
    row0 = 6 * which
    ln_row = 2 * which
    tm = TM // 2 if split_in and cast_next else TM
    x_specs = ([_prompt_spec(D_MODEL, tm), _latent_spec(D_MODEL, tm)] if split_in
               else [_slab_spec(D_MODEL, tm)])
    if split_out:
        out_specs = [_prompt_spec(D_MODEL, tm), _latent_spec(D_MODEL, tm)]
        out_shape = [jax.ShapeDtypeStruct((N_PROMPT_TOK, D_MODEL), F32),
                     jax.ShapeDtypeStruct((N_LATENT_TOK, D_MODEL), F32)]
    else:
        out_specs = [_slab_spec(D_MODEL, tm)]
        out_shape = [jax.ShapeDtypeStruct((N_TOK, D_MODEL), F32)]
    n_x_out = len(out_specs)
    cast_in_specs, cast_args = [], ()
    if cast_next:
        cast_args, lead = next_f32
        cast_in_specs = [_cast_block_spec(s, lead) for s in FFN_WEIGHT_SHAPES]
        out_specs = out_specs + [_cast_block_spec(s, ()) for s in FFN_WEIGHT_SHAPES]
        out_shape = out_shape + [jax.ShapeDtypeStruct(s, BF16) for s in FFN_WEIGHT_SHAPES]
    scratch = [pltpu.VMEM((tm, D_FF), BF16)]
    if own_lead is None:
        weight_specs = [_resident(s) for s in FFN_WEIGHT_SHAPES]
    else:
        weights = next_f32[0]
        weight_specs = [pl.BlockSpec(memory_space=pl.ANY)] * 3
        stage_shapes = sorted({(STAGE_SLOTS, s[0] // OWN_CHUNKS, s[1]) for s in FFN_WEIGHT_SHAPES})
        scratch += [pltpu.VMEM(s, BF16) for s in FFN_WEIGHT_SHAPES]
        scratch += [pltpu.VMEM(s, F32) for s in stage_shapes]
        scratch += [pltpu.SemaphoreType.DMA((STAGE_SLOTS,)) for _ in stage_shapes]
    outs = pl.pallas_call(
        functools.partial(_ffn_kernel, row0=row0, n_prompt_tiles=N_PROMPT_TOK // tm,
                          split_in=split_in, split_out=split_out, cast_next=cast_next,
                          own_lead=own_lead),
        grid=(N_TOK // tm,),
        in_specs=x_specs + [_mod_spec(tm, layer)] + _ln_specs(layer, ln_row)
        + weight_specs + cast_in_specs,
        out_specs=out_specs,
        out_shape=out_shape,
        scratch_shapes=scratch,
        compiler_params=_params(vmem_limit=FFN_VMEM_LIMIT),
        name="ffn",
    )(*xs, mods, ln_g, ln_b, *weights, *cast_args)
    x_out = tuple(outs[:n_x_out]) if split_out else outs[0]
    return x_out, (tuple(outs[n_x_out:]) if cast_next else None)


ROPE_TILES = DEC_SEQ // TM


def _write_ab(dst_ref, rows, pair, src):
    lo = lax.broadcasted_iota(jnp.int32, src.shape, 1) < HALF_LANES
    swapped = pltpu.roll(src, HALF_LANES, 1)
    blocks = (jnp.where(lo, src, 0.0), jnp.where(lo, 0.0, swapped),
              jnp.where(lo, swapped, 0.0), jnp.where(lo, 0.0, src))
    for n, blk in enumerate(blocks):
        c0 = (4 * pair + n) * LANES
        dst_ref[rows, c0:c0 + LANES] = blk.astype(BF16)


def _qkv_kernel(x_ref, mod_ref, w_ref, cos_ref, sup_ref, sdn_ref,
                q_ref, kab_ref, vab_ref, ks_ref, vs_ref):
    kv_raw = []
    for r in range(TM // EPILOGUE_ROWS):
        rows = slice(r * EPILOGUE_ROWS, (r + 1) * EPILOGUE_ROWS)
        h = _modulate(x_ref[rows, :], mod_ref, 3).astype(BF16)
        qkv = jnp.dot(h, w_ref[...], preferred_element_type=F32)
        cos = cos_ref[rows, :]
        s_up = sup_ref[rows, :]
        s_dn = sdn_ref[rows, :]

        def rope(blk):
            up = pltpu.roll(blk, LANES - 16, 1)
            dn = pltpu.roll(blk, 16, 1)
            return blk * cos + up * s_up + dn * s_dn

        for j in range(D_Q // LANES):
            cols = slice(j * LANES, (j + 1) * LANES)
            q_ref[rows, cols] = (rope(qkv[:, cols]) * Q_SCALE).astype(BF16)
        for pair in range(D_KV // LANES):
            k_cols = slice(D_Q + pair * LANES, D_Q + (pair + 1) * LANES)
            v_cols = slice(D_Q + D_KV + pair * LANES, D_Q + D_KV + (pair + 1) * LANES)
            _write_ab(kab_ref, rows, pair, rope(qkv[:, k_cols]))
            _write_ab(vab_ref, rows, pair, qkv[:, v_cols])
        kv_raw.append(qkv[:, D_Q:])

    @pl.when(pl.program_id(0) < N_PROMPT_TILES)
    def _():
        kv_t = jnp.concatenate(kv_raw, axis=0).T
        for s in range(TM // SEQ):
            ks_ref[s] = kv_t[:D_KV, s * SEQ:(s + 1) * SEQ]
            vs_ref[s] = kv_t[D_KV:, s * SEQ:(s + 1) * SEQ]


def _rope_index(i):
    return (jnp.where(i < N_PROMPT_TILES, ROPE_TILES, (i - N_PROMPT_TILES) % ROPE_TILES), 0)


def _qkv(x, mods, w, cos_t, sup_t, sdn_t):
    seqs_per_tile = TM // SEQ
    state_spec = pl.BlockSpec((seqs_per_tile, D_KV, SEQ),
                              lambda i: (jnp.minimum(i, N_PROMPT_TILES - 1), 0, 0))
    return pl.pallas_call(
        _qkv_kernel,
        grid=(N_TOK // TM,),
        in_specs=[
            _slab_spec(D_MODEL),
            _mod_spec(TM, 0),
            _resident((D_MODEL, D_Q + 2 * D_KV)),
            pl.BlockSpec((TM, LANES), _rope_index),
            pl.BlockSpec((TM, LANES), _rope_index),
            pl.BlockSpec((TM, LANES), _rope_index),
        ],
        out_specs=[
            _slab_spec(D_Q), _slab_spec(D_KV_AB), _slab_spec(D_KV_AB), state_spec, state_spec,
        ],
        out_shape=[
            jax.ShapeDtypeStruct((N_TOK, D_Q), BF16),
            jax.ShapeDtypeStruct((N_TOK, D_KV_AB), BF16),
            jax.ShapeDtypeStruct((N_TOK, D_KV_AB), BF16),
            jax.ShapeDtypeStruct((BATCH, D_KV, SEQ), F32),
            jax.ShapeDtypeStruct((BATCH, D_KV, SEQ), F32),
        ],
        compiler_params=_params(),
        name="qkv",
    )(x, mods, w, cos_t, sup_t, sdn_t)


def _attend(q_ref, o_ref, sink_ref, k_slabs, v_slabs, masks, tq, row0=0):
    nt = (((1,), (1,)), ((), ()))
    rows = slice(row0, row0 + tq)
    row_hi = lax.broadcasted_iota(jnp.int32, (2 * tq, 1), 0) >= tq
    lane_lo = lax.broadcasted_iota(jnp.int32, (2 * tq, LANES), 1) < HALF_LANES
    for kh in range(N_KV_HEADS):
        ks = k_slabs(kh)
        vs = v_slabs(kh)
        k_cat = jnp.concatenate([s[:, :LANES] for s in ks] + [s[:, LANES:] for s in ks], axis=0)
        v_cat = jnp.concatenate([s[:, :LANES] for s in vs] + [s[:, LANES:] for s in vs], axis=0)
        n_keys = k_cat.shape[0] // 2
        j0 = 2 * kh
        q2 = jnp.concatenate([q_ref[rows, j0 * LANES:(j0 + 1) * LANES],
                              q_ref[rows, (j0 + 1) * LANES:(j0 + 2) * LANES]], axis=0)
        s = lax.dot_general(q2, k_cat, nt, preferred_element_type=F32)
        es, inv_dens = [], []
        for half in range(2):
            segs, off = [], half * n_keys
            for slab, m in zip(ks, masks):
                seg = s[:, off:off + slab.shape[0]]
                segs.append(seg if m is None else jnp.where(m, seg, NEG_INF))
                off += slab.shape[0]
            logits = jnp.concatenate(segs, axis=1)
            sink = LOG2_E * jnp.where(row_hi, sink_ref[4 * kh + 2 + half], sink_ref[4 * kh + half])
            m_row = jnp.maximum(jnp.max(logits, axis=-1, keepdims=True), sink)
            e = jnp.exp2(logits - m_row)
            den = jnp.sum(e, axis=-1, keepdims=True) + jnp.exp2(sink - m_row)
            es.append(e.astype(BF16))
            inv_dens.append(1.0 / den)
        p = jnp.concatenate(es, axis=1)
        o2 = jnp.dot(p, v_cat, preferred_element_type=F32)
        o2 = o2 * jnp.where(lane_lo, inv_dens[0], inv_dens[1])
        o_ref[rows, j0 * LANES:(j0 + 1) * LANES] = o2[:tq].astype(BF16)
        o_ref[rows, (j0 + 1) * LANES:(j0 + 2) * LANES] = o2[tq:].astype(BF16)


def _kv_cols(kh):
    return slice(kh * 2 * LANES, (kh + 1) * 2 * LANES)


def _ctx_attn_kernel(sink_ref, q_ref, k_ref, v_ref, o_ref):
    _attend(q_ref, o_ref, sink_ref,
            lambda kh: [k_ref[:, _kv_cols(kh)]],
            lambda kh: [v_ref[:, _kv_cols(kh)]],
            [None], SEQ)


def _ctx_attn(sink, q, kab, vab):
    return pl.pallas_call(
        _ctx_attn_kernel,
        grid=(BATCH,),
        in_specs=[
            pl.BlockSpec(memory_space=pltpu.SMEM),
            pl.BlockSpec((SEQ, D_Q), lambda b: (b, 0)),
            pl.BlockSpec((SEQ, D_KV_AB), lambda b: (b, 0)),
            pl.BlockSpec((SEQ, D_KV_AB), lambda b: (b, 0)),
        ],
        out_specs=pl.BlockSpec((SEQ, D_Q), lambda b: (b, 0)),
        out_shape=jax.ShapeDtypeStruct((N_PROMPT_TOK, D_Q), BF16),
        compiler_params=_params(),
        name="ctx_attn",
    )(sink, q, kab, vab)


N_QBLK = DEC_SEQ // BLOCK


LAT_SUB = 4
LAT_STEPS = N_QBLK // LAT_SUB


def _lat_attn_kernel(sink_ref, q_ref, kp_ref, kc_ref, kn_ref, vp_ref, vc_ref, vn_ref,
                     kx_ref, vx_ref, o_ref):
    step = pl.program_id(1)
    r = lax.broadcasted_iota(jnp.int32, (2 * BLOCK, BLOCK), 0) & (BLOCK - 1)
    c = lax.broadcasted_iota(jnp.int32, (2 * BLOCK, BLOCK), 1)

    def blocks(p_ref, c_ref, n_ref, kh):
        cols = _kv_cols(kh)
        return ([p_ref[:, cols]]
                + [c_ref[j * BLOCK:(j + 1) * BLOCK, cols] for j in range(LAT_SUB)]
                + [n_ref[:, cols]])

    for sub in range(LAT_SUB):
        no_prev = (step == 0) if sub == 0 else False
        no_next = (step == LAT_STEPS - 1) if sub == LAT_SUB - 1 else False
        m_prev = c >= r + jnp.where(no_prev, BLOCK, 0)
        m_next = c <= r - jnp.where(no_next, BLOCK, 0)

        def slabs(p_ref, c_ref, n_ref, x_ref, sub=sub):
            return lambda kh: blocks(p_ref, c_ref, n_ref, kh)[sub:sub + 3] + [x_ref[0, :, _kv_cols(kh)]]

        _attend(q_ref, o_ref, sink_ref,
                slabs(kp_ref, kc_ref, kn_ref, kx_ref), slabs(vp_ref, vc_ref, vn_ref, vx_ref),
                [m_prev, None, m_next, None], BLOCK, row0=sub * BLOCK)


def _lat_attn(sink, q, kab, vab, kab_ctx, vab_ctx):
    tq = LAT_SUB * BLOCK
    first = N_PROMPT_TOK // BLOCK

    def cur(b, i):
        return (N_PROMPT_TOK // tq + b * LAT_STEPS + i, 0)

    def prev(b, i):
        return (first + b * N_QBLK + jnp.maximum(i * LAT_SUB - 1, 0), 0)

    def nxt(b, i):
        return (first + b * N_QBLK + jnp.minimum((i + 1) * LAT_SUB, N_QBLK - 1), 0)

    edge_spec = lambda f: pl.BlockSpec((BLOCK, D_KV_AB), f)
    cur_spec = pl.BlockSpec((tq, D_KV_AB), cur)
    ctx_spec = pl.BlockSpec((1, PAST_LEN, D_KV_AB), lambda b, i: (b, 0, 0))
    return pl.pallas_call(
        _lat_attn_kernel,
        grid=(DEC_BATCH, LAT_STEPS),
        in_specs=[
            pl.BlockSpec(memory_space=pltpu.SMEM),
            pl.BlockSpec((tq, D_Q), cur),
            edge_spec(prev), cur_spec, edge_spec(nxt),
            edge_spec(prev), cur_spec, edge_spec(nxt),
            ctx_spec, ctx_spec,
        ],
        out_specs=pl.BlockSpec((tq, D_Q), lambda b, i: (b * LAT_STEPS + i, 0)),
        out_shape=jax.ShapeDtypeStruct((N_LATENT_TOK, D_Q), BF16),
        compiler_params=_params(2),
        name="lat_attn",
    )(sink, q, kab, kab, kab, vab, vab, vab, kab_ctx, vab_ctx)


def _proj_kernel(x_ref, ap_ref, al_ref, mod_ref, lng_ref, lnb_ref, wo_ref, o_ref):
    a = _read_split(ap_ref, al_ref)
    gate = mod_ref[5:6, :]
    for r in range(TM // EPILOGUE_ROWS):
        rows = slice(r * EPILOGUE_ROWS, (r + 1) * EPILOGUE_ROWS)
        f = jnp.dot(a[rows], wo_ref[...], preferred_element_type=F32)
        y = DEEPNORM_ALPHA * x_ref[rows, :] + gate * f
        o_ref[rows, :] = _layer_norm(y, lng_ref[...], lnb_ref[...])


def _proj(x, attn_prompt, attn_latent, mods, ln_g, ln_b, w_o):
    return pl.pallas_call(
        _proj_kernel,
        grid=(N_TOK // TM,),
        in_specs=[
            _slab_spec(D_MODEL),
            _prompt_spec(D_Q),
            _latent_spec(D_Q),
            _mod_spec(TM, 0)] + _ln_specs(0, 1) + [
            _resident((D_Q, D_MODEL)),
        ],
        out_specs=_slab_spec(D_MODEL),
        out_shape=jax.ShapeDtypeStruct((N_TOK, D_MODEL), F32),
        compiler_params=_params(),
        name="attn_proj",
    )(x, attn_prompt, attn_latent, mods, ln_g, ln_b, w_o)


POOL_SUB = 2
POOL_STEP_ROWS = POOL_SUB * POOL_TM
PROMPT_POOL_STEPS = N_PROMPT_TOK // POOL_STEP_ROWS
LATENT_POOL_STEPS = DEC_SEQ // POOL_STEP_ROWS


def _pool_kernel(x_ref, xp_ref, xn_ref, mod_ref, lng_ref, lnb_ref, w_ref, sc_ref, o_ref):
    i = pl.program_id(0)
    is_prompt = i < PROMPT_POOL_STEPS
    in_seq = (i - PROMPT_POOL_STEPS) % LATENT_POOL_STEPS
    h_all = _modulate(x_ref[...], mod_ref, 3)
    h_before = _modulate(xp_ref[...], mod_ref, 3)
    h_after = _modulate(xn_ref[...], mod_ref, 3)
    gate_scale = mod_ref[5:6, :] * sc_ref[...]
    for sub in range(POOL_SUB):
        rows = slice(sub * POOL_TM, (sub + 1) * POOL_TM)
        is_start = is_prompt | (in_seq == 0) if sub == 0 else is_prompt
        is_end = is_prompt | (in_seq == LATENT_POOL_STEPS - 1) if sub == POOL_SUB - 1 else is_prompt
        before = h_before if sub == 0 else h_all[rows.start - POOL_HALO:rows.start]
        after = h_after if sub == POOL_SUB - 1 else h_all[rows.stop:rows.stop + POOL_HALO]
        _pool_sub_tile(x_ref[rows, :], h_all[rows], before, after, is_start, is_end, gate_scale,
                       lng_ref, lnb_ref, w_ref, o_ref, rows)


def _pool_sub_tile(x, h, before, after, is_start, is_end, gate_scale, lng_ref, lnb_ref, w_ref,
                   o_ref, rows):
    h_ext = jnp.concatenate([jnp.where(is_start, 0.0, before), h, jnp.where(is_end, 0.0, after)],
                            axis=0)
    n_ext = POOL_TM + 2 * POOL_HALO

    r8 = lax.broadcasted_iota(jnp.int32, (POOL_HALO, POOL_GROUP_DIM), 0)
    outs = []
    for gi, w in enumerate(POOL_WINDOWS):
        half = w // 2
        cols = slice(gi * POOL_GROUP_DIM, (gi + 1) * POOL_GROUP_DIM)
        acc = h_ext[:, cols]
        s = 1
        while s < w:
            acc = acc + pltpu.roll(acc, s, 0)
            s *= 2
        if half > 1:
            acc = pltpu.roll(acc, n_ext - (half - 1), 0)
        total = acc[POOL_HALO:POOL_HALO + POOL_TM]
        cnt_top = w - jnp.where(is_start, jnp.maximum(half - r8, 0), 0)
        cnt_bot = w - jnp.where(is_end, jnp.maximum(r8 + half - POOL_HALO, 0), 0)
        inv_cnt = jnp.concatenate([
            1.0 / cnt_top.astype(F32),
            jnp.full((POOL_TM - 2 * POOL_HALO, POOL_GROUP_DIM), 1.0 / w, F32),
            1.0 / cnt_bot.astype(F32)], axis=0)
        pooled = (total * inv_cnt - h[:, cols]).astype(BF16)
        outs.append(jnp.dot(pooled, w_ref[gi], preferred_element_type=F32))
    y = DEEPNORM_ALPHA * x + gate_scale * jnp.concatenate(outs, axis=-1)
    o_ref[rows, :] = _layer_norm(y, lng_ref[...], lnb_ref[...])


def _pool(x, mods, ln_g, ln_b, w_pool, scale):
    halo_per_tile = POOL_STEP_ROWS // POOL_HALO
    last_halo = N_TOK // POOL_HALO - 1
    return pl.pallas_call(
        _pool_kernel,
        grid=(N_TOK // POOL_STEP_ROWS,),
        in_specs=[
            pl.BlockSpec((POOL_STEP_ROWS, D_MODEL), lambda i: (i, 0)),
            pl.BlockSpec((POOL_HALO, D_MODEL), lambda i: (jnp.maximum(i * halo_per_tile - 1, 0), 0)),
            pl.BlockSpec((POOL_HALO, D_MODEL),
                         lambda i: (jnp.minimum((i + 1) * halo_per_tile, last_halo), 0)),
            _mod_spec(POOL_STEP_ROWS, 1)] + _ln_specs(1, 1) + [
            _resident((len(POOL_WINDOWS), POOL_GROUP_DIM, POOL_GROUP_DIM)),
            _resident((1, D_MODEL)),
        ],
        out_specs=pl.BlockSpec((POOL_STEP_ROWS, D_MODEL), lambda i: (i, 0)),
        out_shape=jax.ShapeDtypeStruct((N_TOK, D_MODEL), F32),
        compiler_params=_params(),
        name="pool",
    )(x, x, x, mods, ln_g, ln_b, w_pool, scale.reshape(1, D_MODEL))


def _ab_layout(a):
    lead = a.shape[:-1]
    a = a.reshape(lead + (N_KV_HEADS, 1, HEAD_DIM)).astype(BF16)
    z = jnp.zeros_like(a)
    return jnp.concatenate([a, z, z, a], axis=-2).reshape(lead + (D_KV_AB,))


def _rope_tables():
    n_rows = DEC_SEQ // GRID_W
    rows = jnp.repeat(jnp.arange(n_rows, dtype=F32), GRID_W)
    cols = jnp.tile(jnp.arange(GRID_W, dtype=F32), n_rows)
    inv = jnp.power(ROPE_BASE, -jnp.arange(N_FREQ, dtype=F32) / N_FREQ)
    ang_r = rows[:, None] * inv
    ang_c = cols[:, None] * inv
    ang = jnp.concatenate([ang_r, ang_r, ang_c, ang_c], axis=-1)
    cos = jnp.tile(jnp.cos(ang), (1, LANES // HEAD_DIM))
    sin = jnp.tile(jnp.sin(ang), (1, LANES // HEAD_DIM))
    first_half = (jnp.arange(LANES) % 32) < 16
    s_up = jnp.where(first_half, -sin, 0.0)
    s_dn = jnp.where(first_half, 0.0, sin)
    ident = jnp.ones((TM, LANES), F32)
    zeros = jnp.zeros((TM, LANES), F32)
    return (jnp.concatenate([cos, ident], axis=0),
            jnp.concatenate([s_up, zeros], axis=0),
            jnp.concatenate([s_dn, zeros], axis=0))


def kernel(x_prompt, x_sample, cache_k, cache_v, c, c_ctx, w_mod, b_mod, ln_g, ln_b,
           ffn_w_gate, ffn_w_up, ffn_w_down, attn_w_qkv, attn_w_o, attn_sink,
           pool_w, pool_scale):
    cond = jnp.concatenate(
        [c_ctx[None, :], c, jnp.zeros((COND_ROWS - N_COND, D_MODEL), F32)], axis=0)
    mods = _adaln(cond, w_mod, b_mod)

    ffn_w = (ffn_w_gate, ffn_w_up, ffn_w_down)
    ln_g = ln_g.reshape(DEPTH * 3, 1, D_MODEL)
    ln_b = ln_b.reshape(DEPTH * 3, 1, D_MODEL)
    ffn = functools.partial(_ffn, mods=mods, ln_g=ln_g, ln_b=ln_b)

    x, w_next = ffn((x_prompt.reshape(N_PROMPT_TOK, D_MODEL), x_sample.reshape(N_LATENT_TOK, D_MODEL)),
                    weights=None, layer=0, which=0, next_f32=(ffn_w, (0, 1)))
    cos_t, sup_t, sdn_t = _rope_tables()
    q, kab, vab, k_state, v_state = _qkv(x, mods, attn_w_qkv[0].astype(BF16), cos_t, sup_t, sdn_t)
    sink = attn_sink[0]
    o_ctx = _ctx_attn(sink, q, kab, vab)
    kab_ctx = _ab_layout(cache_k[:, 0].reshape(DEC_BATCH, PAST_LEN, D_KV))
    vab_ctx = _ab_layout(cache_v[:, 0].reshape(DEC_BATCH, PAST_LEN, D_KV))
    o_lat = _lat_attn(sink, q, kab, vab, kab_ctx, vab_ctx)
    x = _proj(x, o_ctx, o_lat, mods, ln_g, ln_b, attn_w_o[0].astype(BF16))
    x, w_next = ffn((x,), weights=w_next, layer=0, which=1, next_f32=(ffn_w, (1, 0)))

    x, w_next = ffn((x,), weights=w_next, layer=1, which=0, next_f32=(ffn_w, (1, 1)))
    x = _pool(x, mods, ln_g, ln_b, pool_w[0].astype(BF16), pool_scale[0])
    (y_prompt, y_sample), _ = ffn((x,), weights=w_next, layer=1, which=1, split_out=True)

    def state(s_t):
        s_t = s_t.reshape(BATCH, N_KV_HEADS, HEAD_DIM, SEQ)
        return jnp.transpose(s_t, (0, 3, 1, 2))[:, None]

    return (y_prompt.reshape(BATCH, SEQ, D_MODEL),
            y_sample.reshape(DEC_BATCH, DEC_SEQ, D_MODEL),
            state(k_state), state(v_state))
```

```python
import functools

import jax
import jax.numpy as jnp
from jax import lax
from jax.experimental import pallas as pl
from jax.experimental.pallas import tpu as pltpu

D_MODEL = 1024
BATCH = 16
SEQ = 256
DEPTH = 2
DEC_BATCH = 2
DEC_SEQ = 4096
PAST_LEN = 256
GRID_W = 64
N_HEADS = 16
N_KV_HEADS = 4
HEAD_DIM = 64
WINDOW = 128
BLOCK = 128
ROPE_BASE = 10000.0
N_FREQ = HEAD_DIM // 4
POOL_WINDOWS = (2, 4, 8, 16)
POOL_GROUP_DIM = D_MODEL // 4
D_FF = 2816
N_MOD = 9
LN_EPS = 1e-5
DEEPNORM_ALPHA = (2.0 * DEPTH) ** 0.25
ATTN_SCALE = HEAD_DIM ** -0.5
LOG2_E = 1.4426950408889634
Q_SCALE = ATTN_SCALE * LOG2_E
NEG_INF = -1e30

N_PROMPT_TOK = BATCH * SEQ
N_LATENT_TOK = DEC_BATCH * DEC_SEQ
N_TOK = N_PROMPT_TOK + N_LATENT_TOK
MOD_GROUP_ROWS = 4096
N_COND = 1 + DEC_BATCH
COND_ROWS = 8

LANES = 128
HALF_LANES = LANES // 2
TM = 1024
N_PROMPT_TILES = N_PROMPT_TOK // TM
FF_CHUNK = 256
EPILOGUE_ROWS = 256
POOL_TM = 256
POOL_HALO = 8
D_Q = N_HEADS * HEAD_DIM
D_KV = N_KV_HEADS * HEAD_DIM
D_KV_AB = N_KV_HEADS * 2 * LANES
VMEM_LIMIT = 40 * 1024 * 1024
FFN_VMEM_LIMIT = 55 * 1024 * 1024

F32 = jnp.float32
BF16 = jnp.bfloat16


def _layer_norm(y, g, b):
    mu = jnp.mean(y, axis=-1, keepdims=True)
    yc = y - mu
    var = jnp.mean(yc * yc, axis=-1, keepdims=True)
    return yc * lax.rsqrt(var + LN_EPS) * g + b


def _modulate(x, mod_ref, row0):
    shift = mod_ref[row0:row0 + 1, :]
    scale = mod_ref[row0 + 1:row0 + 2, :]
    return x * (1.0 + scale) + shift


def _params(n_axes=1, vmem_limit=VMEM_LIMIT):
    return pltpu.CompilerParams(
        dimension_semantics=("arbitrary",) * n_axes,
        vmem_limit_bytes=vmem_limit)


def _resident(shape):
    nd = len(shape)
    return pl.BlockSpec(shape, lambda *_: (0,) * nd, pipeline_mode=pl.Buffered(1))


def _resident_at(index, tail):
    return pl.BlockSpec((None,) * len(index) + tuple(tail),
                        lambda *_: tuple(index) + (0,) * len(tail), pipeline_mode=pl.Buffered(1))


def _mod_spec(tile_rows, layer):
    tiles_per_group = MOD_GROUP_ROWS // tile_rows
    return pl.BlockSpec((None, None, N_MOD, D_MODEL),
                        lambda i: (layer, i // tiles_per_group, 0, 0))


def _ln_specs(layer, which):
    return [_resident_at((layer * 3 + which,), (1, D_MODEL))] * 2


def _slab_spec(cols, tm=TM):
    return pl.BlockSpec((tm, cols), lambda i: (i, 0))


def _prompt_spec(cols, tm=TM):
    return pl.BlockSpec((tm, cols), lambda i: (jnp.minimum(i, N_PROMPT_TOK // tm - 1), 0))


def _latent_spec(cols, tm=TM):
    return pl.BlockSpec((tm, cols), lambda i: (jnp.maximum(i - N_PROMPT_TOK // tm, 0), 0))


def _read_split(p_ref, l_ref, n_prompt_tiles=N_PROMPT_TILES):
    return jnp.where(pl.program_id(0) < n_prompt_tiles, p_ref[...], l_ref[...])


ADALN_NC = 2304


def _adaln_kernel(cond_ref, w_ref, b_ref, o_ref):
    c = cond_ref[...]
    s = (c * (1.0 / (1.0 + jnp.exp(-c)))).astype(BF16)
    o_ref[0] = jnp.dot(s, w_ref[0].astype(BF16), preferred_element_type=F32) + b_ref[0]


def _adaln(cond, w_mod, b_mod):
    n_out = N_MOD * D_MODEL
    out = pl.pallas_call(
        _adaln_kernel,
        grid=(DEPTH, n_out // ADALN_NC),
        in_specs=[
            pl.BlockSpec((COND_ROWS, D_MODEL), lambda l, n: (0, 0)),
            pl.BlockSpec((1, D_MODEL, ADALN_NC), lambda l, n: (l, 0, n)),
            pl.BlockSpec((1, 1, ADALN_NC), lambda l, n: (l, 0, n)),
        ],
        out_specs=pl.BlockSpec((1, COND_ROWS, ADALN_NC), lambda l, n: (l, 0, n)),
        out_shape=jax.ShapeDtypeStruct((DEPTH, COND_ROWS, n_out), F32),
        compiler_params=_params(2),
        name="adaln",
    )(cond, w_mod, b_mod.reshape(DEPTH, 1, n_out))
    return out[:, :N_COND].reshape(DEPTH, N_COND, N_MOD, D_MODEL)


def _ffn_matmuls(x, mod_ref, wg_ref, wu_ref, wd_ref, a_ref, row0):
    h = _modulate(x, mod_ref, row0).astype(BF16)
    for c in range(D_FF // FF_CHUNK):
        sl = slice(c * FF_CHUNK, (c + 1) * FF_CHUNK)
        g = jnp.dot(h, wg_ref[:, sl], preferred_element_type=F32)
        u = jnp.dot(h, wu_ref[:, sl], preferred_element_type=F32)
        a_ref[:, sl] = (g * (1.0 / (1.0 + jnp.exp(-g))) * u).astype(BF16)
    return jnp.dot(a_ref[...], wd_ref[...], preferred_element_type=F32)


def _ffn_epilogue(x, f, mod_ref, lng_ref, lnb_ref, row0):
    gate = mod_ref[row0 + 2:row0 + 3, :]
    y = DEEPNORM_ALPHA * x + (0.5 * gate) * f
    return _layer_norm(y, lng_ref[...], lnb_ref[...])


CAST_STEPS = 8
OWN_CHUNKS = 16
STAGE_SLOTS = 4
FFN_WEIGHT_SHAPES = ((D_MODEL, D_FF), (D_MODEL, D_FF), (D_FF, D_MODEL))


def _weight_chunk_copy(src_ref, lead, stage_ref, sem_ref, chunk, slot):
    rows = stage_ref.shape[1]
    src = src_ref.at[lead[0], lead[1], pl.ds(chunk * rows, rows), :]
    return pltpu.make_async_copy(src, stage_ref.at[slot], sem_ref.at[slot])


def _load_own_weights(lead, srcs, dsts, stages, sems):
    jobs, used = [], {}
    for src, dst in zip(srcs, dsts):
        key = dst.shape[1]
        for chunk in range(OWN_CHUNKS):
            slot = used.get(key, 0) % STAGE_SLOTS
            used[key] = used.get(key, 0) + 1
            jobs.append((_weight_chunk_copy(src, lead, stages[key], sems[key], chunk, slot),
                         stages[key], slot, dst, chunk))
    ahead = STAGE_SLOTS - 1
    for job in jobs[:ahead]:
        job[0].start()
    for n, (copy, stage, slot, dst, chunk) in enumerate(jobs):
        if n + ahead < len(jobs):
            jobs[n + ahead][0].start()
        copy.wait()
        rows = stage.shape[1]
        dst[chunk * rows:(chunk + 1) * rows, :] = stage[slot].astype(BF16)


def _ffn_kernel(*refs, row0, n_prompt_tiles, split_in, split_out, cast_next, own_lead):
    n_x = 2 if split_in else 1
    n_cast = 3 if cast_next else 0
    n_own = 7 if own_lead is not None else 0
    own = refs[len(refs) - n_own:]
    refs = refs[:len(refs) - n_own]
    x_refs, (mod_ref, lng_ref, lnb_ref, wg_ref, wu_ref, wd_ref) = refs[:n_x], refs[n_x:n_x + 6]
    next_f32 = refs[n_x + 6:n_x + 6 + n_cast]
    o_refs, next_bf16, a_ref = refs[n_x + 6 + n_cast:-1 - n_cast], refs[-1 - n_cast:-1], refs[-1]
    step = pl.program_id(0)
    if own_lead is not None:
        own_w, stage_a, stage_b, sem_a, sem_b = own[:3], own[3], own[4], own[5], own[6]
        stages = {stage_a.shape[2]: stage_a, stage_b.shape[2]: stage_b}
        sems = {stage_a.shape[2]: sem_a, stage_b.shape[2]: sem_b}

        @pl.when(step == 0)
        def _():
            _load_own_weights(own_lead, (wg_ref, wu_ref, wd_ref), own_w, stages, sems)

        wg_ref, wu_ref, wd_ref = own_w
    x = _read_split(*x_refs, n_prompt_tiles) if split_in else x_refs[0][...]
    f = _ffn_matmuls(x, mod_ref, wg_ref, wu_ref, wd_ref, a_ref, row0)
    out = _ffn_epilogue(x, f, mod_ref, lng_ref, lnb_ref, row0)
    if split_out:
        @pl.when(step < n_prompt_tiles)
        def _():
            o_refs[0][...] = out

        @pl.when(step >= n_prompt_tiles)
        def _():
            o_refs[1][...] = out
    else:
        o_refs[0][...] = out
    if cast_next:
        @pl.when(step < CAST_STEPS)
        def _():
            for src, dst in zip(next_f32, next_bf16):
                dst[...] = src[...].astype(BF16)


def _cast_block_spec(shape, lead):
    block = (None,) * len(lead) + (shape[0] // CAST_STEPS, shape[1])
    return pl.BlockSpec(block, lambda i: tuple(lead) + (jnp.minimum(i, CAST_STEPS - 1), 0))


def _ffn(xs, mods, ln_g, ln_b, weights, layer, which, next_f32=None, split_out=False):
    split_in = len(xs) == 2
    cast_next = next_f32 is not None
    own_lead = (layer, which) if weights is None else None
    row0 = 6 * which
    ln_row = 2 * which
    tm = TM // 2 if split_in and cast_next else TM
    x_specs = ([_prompt_spec(D_MODEL, tm), _latent_spec(D_MODEL, tm)] if split_in
               else [_slab_spec(D_MODEL, tm)])
    if split_out:
        out_specs = [_prompt_spec(D_MODEL, tm), _latent_spec(D_MODEL, tm)]
        out_shape = [jax.ShapeDtypeStruct((N_PROMPT_TOK, D_MODEL), F32),
                     jax.ShapeDtypeStruct((N_LATENT_TOK, D_MODEL), F32)]
    else:
        out_specs = [_slab_spec(D_MODEL, tm)]
        out_shape = [jax.ShapeDtypeStruct((N_TOK, D_MODEL), F32)]
    n_x_out = len(out_specs)
    cast_in_specs, cast_args = [], ()
    if cast_next:
        cast_args, lead = next_f32
        cast_in_specs = [_cast_block_spec(s, lead) for s in FFN_WEIGHT_SHAPES]
        out_specs = out_specs + [_cast_block_spec(s, ()) for s in FFN_WEIGHT_SHAPES]
        out_shape = out_shape + [jax.ShapeDtypeStruct(s, BF16) for s in FFN_WEIGHT_SHAPES]
    scratch = [pltpu.VMEM((tm, D_FF), BF16)]
    if own_lead is None:
        weight_specs = [_resident(s) for s in FFN_WEIGHT_SHAPES]
    else:
        weights = next_f32[0]
        weight_specs = [pl.BlockSpec(memory_space=pl.ANY)] * 3
        stage_shapes = sorted({(STAGE_SLOTS, s[0] // OWN_CHUNKS, s[1]) for s in FFN_WEIGHT_SHAPES})
        scratch += [pltpu.VMEM(s, BF16) for s in FFN_WEIGHT_SHAPES]
        scratch += [pltpu.VMEM(s, F32) for s in stage_shapes]
        scratch += [pltpu.SemaphoreType.DMA((STAGE_SLOTS,)) for _ in stage_shapes]
    outs = pl.pallas_call(
        functools.partial(_ffn_kernel, row0=row0, n_prompt_tiles=N_PROMPT_TOK // tm,
                          split_in=split_in, split_out=split_out, cast_next=cast_next,
                          own_lead=own_lead),
        grid=(N_TOK // tm,),
        in_specs=x_specs + [_mod_spec(tm, layer)] + _ln_specs(layer, ln_row)
        + weight_specs + cast_in_specs,
        out_specs=out_specs,
        out_shape=out_shape,
        scratch_shapes=scratch,
        compiler_params=_params(vmem_limit=FFN_VMEM_LIMIT),
        name="ffn",
    )(*xs, mods, ln_g, ln_b, *weights, *cast_args)
    x_out = tuple(outs[:n_x_out]) if split_out else outs[0]
    return x_out, (tuple(outs[n_x_out:]) if cast_next else None)


ROPE_TILES = DEC_SEQ // TM


def _write_ab(dst_ref, rows, pair, src):
    lo = lax.broadcasted_iota(jnp.int32, src.shape, 1) < HALF_LANES
    swapped = pltpu.roll(src, HALF_LANES, 1)
    blocks = (jnp.where(lo, src, 0.0), jnp.where(lo, 0.0, swapped),
              jnp.where(lo, swapped, 0.0), jnp.where(lo, 0.0, src))
    for n, blk in enumerate(blocks):
        c0 = (4 * pair + n) * LANES
        dst_ref[rows, c0:c0 + LANES] = blk.astype(BF16)


def _qkv_kernel(x_ref, mod_ref, w_ref, cos_ref, sup_ref, sdn_ref,
                q_ref, kab_ref, vab_ref, ks_ref, vs_ref):
    kv_raw = []
    for r in range(TM // EPILOGUE_ROWS):
        rows = slice(r * EPILOGUE_ROWS, (r + 1) * EPILOGUE_ROWS)
        h = _modulate(x_ref[rows, :], mod_ref, 3).astype(BF16)
        qkv = jnp.dot(h, w_ref[...], preferred_element_type=F32)
        cos = cos_ref[rows, :]
        s_up = sup_ref[rows, :]
        s_dn = sdn_ref[rows, :]

        def rope(blk):
            up = pltpu.roll(blk, LANES - 16, 1)
            dn = pltpu.roll(blk, 16, 1)
            return blk * cos + up * s_up + dn * s_dn

        for j in range(D_Q // LANES):
            cols = slice(j * LANES, (j + 1) * LANES)
            q_ref[rows, cols] = (rope(qkv[:, cols]) * Q_SCALE).astype(BF16)
        for pair in range(D_KV // LANES):
            k_cols = slice(D_Q + pair * LANES, D_Q + (pair + 1) * LANES)
            v_cols = slice(D_Q + D_KV + pair * LANES, D_Q + D_KV + (pair + 1) * LANES)
            _write_ab(kab_ref, rows, pair, rope(qkv[:, k_cols]))
            _write_ab(vab_ref, rows, pair, qkv[:, v_cols])
        kv_raw.append(qkv[:, D_Q:])

    @pl.when(pl.program_id(0) < N_PROMPT_TILES)
    def _():
        kv_t = jnp.concatenate(kv_raw, axis=0).T
        for s in range(TM // SEQ):
            ks_ref[s] = kv_t[:D_KV, s * SEQ:(s + 1) * SEQ]
            vs_ref[s] = kv_t[D_KV:, s * SEQ:(s + 1) * SEQ]


def _rope_index(i):
    return (jnp.where(i < N_PROMPT_TILES, ROPE_TILES, (i - N_PROMPT_TILES) % ROPE_TILES), 0)


def _qkv(x, mods, w, cos_t, sup_t, sdn_t):
    seqs_per_tile = TM // SEQ
    state_spec = pl.BlockSpec((seqs_per_tile, D_KV, SEQ),
                              lambda i: (jnp.minimum(i, N_PROMPT_TILES - 1), 0, 0))
    return pl.pallas_call(
        _qkv_kernel,
        grid=(N_TOK // TM,),
        in_specs=[
            _slab_spec(D_MODEL),
            _mod_spec(TM, 0),
            _resident((D_MODEL, D_Q + 2 * D_KV)),
            pl.BlockSpec((TM, LANES), _rope_index),
            pl.BlockSpec((TM, LANES), _rope_index),
            pl.BlockSpec((TM, LANES), _rope_index),
        ],
        out_specs=[
            _slab_spec(D_Q), _slab_spec(D_KV_AB), _slab_spec(D_KV_AB), state_spec, state_spec,
        ],
        out_shape=[
            jax.ShapeDtypeStruct((N_TOK, D_Q), BF16),
            jax.ShapeDtypeStruct((N_TOK, D_KV_AB), BF16),
            jax.ShapeDtypeStruct((N_TOK, D_KV_AB), BF16),
            jax.ShapeDtypeStruct((BATCH, D_KV, SEQ), F32),
            jax.ShapeDtypeStruct((BATCH, D_KV, SEQ), F32),
        ],
        compiler_params=_params(),
        name="qkv",
    )(x, mods, w, cos_t, sup_t, sdn_t)


def _attend(q_ref, o_ref, sink_ref, k_slabs, v_slabs, masks, tq, row0=0):
    nt = (((1,), (1,)), ((), ()))
    rows = slice(row0, row0 + tq)
    row_hi = lax.broadcasted_iota(jnp.int32, (2 * tq, 1), 0) >= tq
    lane_lo = lax.broadcasted_iota(jnp.int32, (2 * tq, LANES), 1) < HALF_LANES
    for kh in range(N_KV_HEADS):
        ks = k_slabs(kh)
        vs = v_slabs(kh)
        k_cat = jnp.concatenate([s[:, :LANES] for s in ks] + [s[:, LANES:] for s in ks], axis=0)
        v_cat = jnp.concatenate([s[:, :LANES] for s in vs] + [s[:, LANES:] for s in vs], axis=0)
        n_keys = k_cat.shape[0] // 2
        j0 = 2 * kh
        q2 = jnp.concatenate([q_ref[rows, j0 * LANES:(j0 + 1) * LANES],
                              q_ref[rows, (j0 + 1) * LANES:(j0 + 2) * LANES]], axis=0)
        s = lax.dot_general(q2, k_cat, nt, preferred_element_type=F32)
        es, inv_dens = [], []
        for half in range(2):
            segs, off = [], half * n_keys
            for slab, m in zip(ks, masks):
                seg = s[:, off:off + slab.shape[0]]
                segs.append(seg if m is None else jnp.where(m, seg, NEG_INF))
                off += slab.shape[0]
            logits = jnp.concatenate(segs, axis=1)
            sink = LOG2_E * jnp.where(row_hi, sink_ref[4 * kh + 2 + half], sink_ref[4 * kh + half])
            m_row = jnp.maximum(jnp.max(logits, axis=-1, keepdims=True), sink)
            e = jnp.exp2(logits - m_row)
            den = jnp.sum(e, axis=-1, keepdims=True) + jnp.exp2(sink - m_row)
            es.append(e.astype(BF16))
            inv_dens.append(1.0 / den)
        p = jnp.concatenate(es, axis=1)
        o2 = jnp.dot(p, v_cat, preferred_element_type=F32)
        o2 = o2 * jnp.where(lane_lo, inv_dens[0], inv_dens[1])
        o_ref[rows, j0 * LANES:(j0 + 1) * LANES] = o2[:tq].astype(BF16)
        o_ref[rows, (j0 + 1) * LANES:(j0 + 2) * LANES] = o2[tq:].astype(BF16)


def _kv_cols(kh):
    return slice(kh * 2 * LANES, (kh + 1) * 2 * LANES)


def _ctx_attn_kernel(sink_ref, q_ref, k_ref, v_ref, o_ref):
    _attend(q_ref, o_ref, sink_ref,
            lambda kh: [k_ref[:, _kv_cols(kh)]],
            lambda kh: [v_ref[:, _kv_cols(kh)]],
            [None], SEQ)


def _ctx_attn(sink, q, kab, vab):
    return pl.pallas_call(
        _ctx_attn_kernel,
        grid=(BATCH,),
        in_specs=[
            pl.BlockSpec(memory_space=pltpu.SMEM),
            pl.BlockSpec((SEQ, D_Q), lambda b: (b, 0)),
            pl.BlockSpec((SEQ, D_KV_AB), lambda b: (b, 0)),
            pl.BlockSpec((SEQ, D_KV_AB), lambda b: (b, 0)),
        ],
        out_specs=pl.BlockSpec((SEQ, D_Q), lambda b: (b, 0)),
        out_shape=jax.ShapeDtypeStruct((N_PROMPT_TOK, D_Q), BF16),
        compiler_params=_params(),
        name="ctx_attn",
    )(sink, q, kab, vab)


N_QBLK = DEC_SEQ // BLOCK


LAT_SUB = 4
LAT_STEPS = N_QBLK // LAT_SUB


def _lat_attn_kernel(sink_ref, q_ref, kp_ref, kc_ref, kn_ref, vp_ref, vc_ref, vn_ref,
                     kx_ref, vx_ref, o_ref):
    step = pl.program_id(1)
    r = lax.broadcasted_iota(jnp.int32, (2 * BLOCK, BLOCK), 0) & (BLOCK - 1)
    c = lax.broadcasted_iota(jnp.int32, (2 * BLOCK, BLOCK), 1)

    def blocks(p_ref, c_ref, n_ref, kh):
        cols = _kv_cols(kh)
        return ([p_ref[:, cols]]
                + [c_ref[j * BLOCK:(j + 1) * BLOCK, cols] for j in range(LAT_SUB)]
                + [n_ref[:, cols]])

    for sub in range(LAT_SUB):
        no_prev = (step == 0) if sub == 0 else False
        no_next = (step == LAT_STEPS - 1) if sub == LAT_SUB - 1 else False
        m_prev = c >= r + jnp.where(no_prev, BLOCK, 0)
        m_next = c <= r - jnp.where(no_next, BLOCK, 0)

        def slabs(p_ref, c_ref, n_ref, x_ref, sub=sub):
            return lambda kh: blocks(p_ref, c_ref, n_ref, kh)[sub:sub + 3] + [x_ref[0, :, _kv_cols(kh)]]

        _attend(q_ref, o_ref, sink_ref,
                slabs(kp_ref, kc_ref, kn_ref, kx_ref), slabs(vp_ref, vc_ref, vn_ref, vx_ref),
                [m_prev, None, m_next, None], BLOCK, row0=sub * BLOCK)


def _lat_attn(sink, q, kab, vab, kab_ctx, vab_ctx):
    tq = LAT_SUB * BLOCK
    first = N_PROMPT_TOK // BLOCK

    def cur(b, i):
        return (N_PROMPT_TOK // tq + b * LAT_STEPS + i, 0)

    def prev(b, i):
        return (first + b * N_QBLK + jnp.maximum(i * LAT_SUB - 1, 0), 0)

    def nxt(b, i):
        return (first + b * N_QBLK + jnp.minimum((i + 1) * LAT_SUB, N_QBLK - 1), 0)

    edge_spec = lambda f: pl.BlockSpec((BLOCK, D_KV_AB), f)
    cur_spec = pl.BlockSpec((tq, D_KV_AB), cur)
    ctx_spec = pl.BlockSpec((1, PAST_LEN, D_KV_AB), lambda b, i: (b, 0, 0))
    return pl.pallas_call(
        _lat_attn_kernel,
        grid=(DEC_BATCH, LAT_STEPS),
        in_specs=[
            pl.BlockSpec(memory_space=pltpu.SMEM),
            pl.BlockSpec((tq, D_Q), cur),
            edge_spec(prev), cur_spec, edge_spec(nxt),
            edge_spec(prev), cur_spec, edge_spec(nxt),
            ctx_spec, ctx_spec,
        ],
        out_specs=pl.BlockSpec((tq, D_Q), lambda b, i: (b * LAT_STEPS + i, 0)),
        out_shape=jax.ShapeDtypeStruct((N_LATENT_TOK, D_Q), BF16),
        compiler_params=_params(2),
        name="lat_attn",
    )(sink, q, kab, kab, kab, vab, vab, vab, kab_ctx, vab_ctx)


def _proj_kernel(x_ref, ap_ref, al_ref, mod_ref, lng_ref, lnb_ref, wo_ref, o_ref):
    a = _read_split(ap_ref, al_ref)
    gate = mod_ref[5:6, :]
    for r in range(TM // EPILOGUE_ROWS):
        rows = slice(r * EPILOGUE_ROWS, (r + 1) * EPILOGUE_ROWS)
        f = jnp.dot(a[rows], wo_ref[...], preferred_element_type=F32)
        y = DEEPNORM_ALPHA * x_ref[rows, :] + gate * f
        o_ref[rows, :] = _layer_norm(y, lng_ref[...], lnb_ref[...])


def _proj(x, attn_prompt, attn_latent, mods, ln_g, ln_b, w_o):
    return pl.pallas_call(
        _proj_kernel,
        grid=(N_TOK // TM,),
        in_specs=[
            _slab_spec(D_MODEL),
            _prompt_spec(D_Q),
            _latent_spec(D_Q),
            _mod_spec(TM, 0)] + _ln_specs(0, 1) + [
            _resident((D_Q, D_MODEL)),
        ],
        out_specs=_slab_spec(D_MODEL),
        out_shape=jax.ShapeDtypeStruct((N_TOK, D_MODEL), F32),
        compiler_params=_params(),
        name="attn_proj",
    )(x, attn_prompt, attn_latent, mods, ln_g, ln_b, w_o)


POOL_SUB = 4
POOL_STEP_ROWS = POOL_SUB * POOL_TM
PROMPT_POOL_STEPS = N_PROMPT_TOK // POOL_STEP_ROWS
LATENT_POOL_STEPS = DEC_SEQ // POOL_STEP_ROWS


def _pool_kernel(x_ref, xp_ref, xn_ref, mod_ref, lng_ref, lnb_ref, w_ref, sc_ref, o_ref):
    i = pl.program_id(0)
    is_prompt = i < PROMPT_POOL_STEPS
    in_seq = (i - PROMPT_POOL_STEPS) % LATENT_POOL_STEPS
    h_all = _modulate(x_ref[...], mod_ref, 3)
    h_before = _modulate(xp_ref[...], mod_ref, 3)
    h_after = _modulate(xn_ref[...], mod_ref, 3)
    gate_scale = mod_ref[5:6, :] * sc_ref[...]
    for sub in range(POOL_SUB):
        rows = slice(sub * POOL_TM, (sub + 1) * POOL_TM)
        is_start = is_prompt | (in_seq == 0) if sub == 0 else is_prompt
        is_end = is_prompt | (in_seq == LATENT_POOL_STEPS - 1) if sub == POOL_SUB - 1 else is_prompt
        before = h_before if sub == 0 else h_all[rows.start - POOL_HALO:rows.start]
        after = h_after if sub == POOL_SUB - 1 else h_all[rows.stop:rows.stop + POOL_HALO]
        _pool_sub_tile(x_ref[rows, :], h_all[rows], before, after, is_start, is_end, gate_scale,
                       lng_ref, lnb_ref, w_ref, o_ref, rows)


def _pool_sub_tile(x, h, before, after, is_start, is_end, gate_scale, lng_ref, lnb_ref, w_ref,
                   o_ref, rows):
    h_ext = jnp.concatenate([jnp.where(is_start, 0.0, before), h, jnp.where(is_end, 0.0, after)],
                            axis=0)
    n_ext = POOL_TM + 2 * POOL_HALO

    r8 = lax.broadcasted_iota(jnp.int32, (POOL_HALO, POOL_GROUP_DIM), 0)
    outs = []
    for gi, w in enumerate(POOL_WINDOWS):
        half = w // 2
        cols = slice(gi * POOL_GROUP_DIM, (gi + 1) * POOL_GROUP_DIM)
        acc = h_ext[:, cols]
        s = 1
        while s < w:
            acc = acc + pltpu.roll(acc, s, 0)
            s *= 2
        if half > 1:
            acc = pltpu.roll(acc, n_ext - (half - 1), 0)
        total = acc[POOL_HALO:POOL_HALO + POOL_TM]
        cnt_top = w - jnp.where(is_start, jnp.maximum(half - r8, 0), 0)
        cnt_bot = w - jnp.where(is_end, jnp.maximum(r8 + half - POOL_HALO, 0), 0)
        inv_cnt = jnp.concatenate([
            1.0 / cnt_top.astype(F32),
            jnp.full((POOL_TM - 2 * POOL_HALO, POOL_GROUP_DIM), 1.0 / w, F32),
            1.0 / cnt_bot.astype(F32)], axis=0)
        pooled = (total * inv_cnt - h[:, cols]).astype(BF16)
        outs.append(jnp.dot(pooled, w_ref[gi], preferred_element_type=F32))
    y = DEEPNORM_ALPHA * x + gate_scale * jnp.concatenate(outs, axis=-1)
    o_ref[rows, :] = _layer_norm(y, lng_ref[...], lnb_ref[...])


def _pool(x, mods, ln_g, ln_b, w_pool, scale):
    halo_per_tile = POOL_STEP_ROWS // POOL_HALO
    last_halo = N_TOK // POOL_HALO - 1
    return pl.pallas_call(
        _pool_kernel,
        grid=(N_TOK // POOL_STEP_ROWS,),
        in_specs=[
            pl.BlockSpec((POOL_STEP_ROWS, D_MODEL), lambda i: (i, 0)),
            pl.BlockSpec((POOL_HALO, D_MODEL), lambda i: (jnp.maximum(i * halo_per_tile - 1, 0), 0)),
            pl.BlockSpec((POOL_HALO, D_MODEL),
                         lambda i: (jnp.minimum((i + 1) * halo_per_tile, last_halo), 0)),
            _mod_spec(POOL_STEP_ROWS, 1)] + _ln_specs(1, 1) + [
            _resident((len(POOL_WINDOWS), POOL_GROUP_DIM, POOL_GROUP_DIM)),
            _resident((1, D_MODEL)),
        ],
        out_specs=pl.BlockSpec((POOL_STEP_ROWS, D_MODEL), lambda i: (i, 0)),
        out_shape=jax.ShapeDtypeStruct((N_TOK, D_MODEL), F32),
        compiler_params=_params(),
        name="pool",
    )(x, x, x, mods, ln_g, ln_b, w_pool, scale.reshape(1, D_MODEL))


def _ab_layout(a):
    lead = a.shape[:-1]
    a = a.reshape(lead + (N_KV_HEADS, 1, HEAD_DIM)).astype(BF16)
    z = jnp.zeros_like(a)
    return jnp.concatenate([a, z, z, a], axis=-2).reshape(lead + (D_KV_AB,))


def _rope_tables():
    n_rows = DEC_SEQ // GRID_W
    rows = jnp.repeat(jnp.arange(n_rows, dtype=F32), GRID_W)
    cols = jnp.tile(jnp.arange(GRID_W, dtype=F32), n_rows)
    inv = jnp.power(ROPE_BASE, -jnp.arange(N_FREQ, dtype=F32) / N_FREQ)
    ang_r = rows[:, None] * inv
    ang_c = cols[:, None] * inv
    ang = jnp.concatenate([ang_r, ang_r, ang_c, ang_c], axis=-1)
    cos = jnp.tile(jnp.cos(ang), (1, LANES // HEAD_DIM))
    sin = jnp.tile(jnp.sin(ang), (1, LANES // HEAD_DIM))
    first_half = (jnp.arange(LANES) % 32) < 16
    s_up = jnp.where(first_half, -sin, 0.0)
    s_dn = jnp.where(first_half, 0.0, sin)
    ident = jnp.ones((TM, LANES), F32)
    zeros = jnp.zeros((TM, LANES), F32)
    return (jnp.concatenate([cos, ident], axis=0),
            jnp.concatenate([s_up, zeros], axis=0),
            jnp.concatenate([s_dn, zeros], axis=0))


def kernel(x_prompt, x_sample, cache_k, cache_v, c, c_ctx, w_mod, b_mod, ln_g, ln_b,
           ffn_w_gate, ffn_w_up, ffn_w_down, attn_w_qkv, attn_w_o, attn_sink,
           pool_w, pool_scale):
    cond = jnp.concatenate(
        [c_ctx[None, :], c, jnp.zeros((COND_ROWS - N_COND, D_MODEL), F32)], axis=0)
    mods = _adaln(cond, w_mod, b_mod)

    ffn_w = (ffn_w_gate, ffn_w_up, ffn_w_down)
    ln_g = ln_g.reshape(DEPTH * 3, 1, D_MODEL)
    ln_b = ln_b.reshape(DEPTH * 3, 1, D_MODEL)
    ffn = functools.partial(_ffn, mods=mods, ln_g=ln_g, ln_b=ln_b)

    x, w_next = ffn((x_prompt.reshape(N_PROMPT_TOK, D_MODEL), x_sample.reshape(N_LATENT_TOK, D_MODEL)),
                    weights=None, layer=0, which=0, next_f32=(ffn_w, (0, 1)))
    cos_t, sup_t, sdn_t = _rope_tables()
    q, kab, vab, k_state, v_state = _qkv(x, mods, attn_w_qkv[0].astype(BF16), cos_t, sup_t, sdn_t)
    sink = attn_sink[0]
    o_ctx = _ctx_attn(sink, q, kab, vab)
    kab_ctx = _ab_layout(cache_k[:, 0].reshape(DEC_BATCH, PAST_LEN, D_KV))
    vab_ctx = _ab_layout(cache_v[:, 0].reshape(DEC_BATCH, PAST_LEN, D_KV))
    o_lat = _lat_attn(sink, q, kab, vab, kab_ctx, vab_ctx)
    x = _proj(x, o_ctx, o_lat, mods, ln_g, ln_b, attn_w_o[0].astype(BF16))
    x, w_next = ffn((x,), weights=w_next, layer=0, which=1, next_f32=(ffn_w, (1, 0)))

    x, w_next = ffn((x,), weights=w_next, layer=1, which=0, next_f32=(ffn_w, (1, 1)))
    x = _pool(x, mods, ln_g, ln_b, pool_w[0].astype(BF16), pool_scale[0])
    (y_prompt, y_sample), _ = ffn((x,), weights=w_next, layer=1, which=1, split_out=True)

    def state(s_t):
        s_t = s_t.reshape(BATCH, N_KV_HEADS, HEAD_DIM, SEQ)
        return jnp.transpose(s_t, (0, 3, 1, 2))[:, None]

    return (y_prompt.reshape(BATCH, SEQ, D_MODEL),
            y_sample.reshape(DEC_BATCH, DEC_SEQ, D_MODEL),
            state(k_state), state(v_state))
```

```python
import functools

import jax
import jax.numpy as jnp
from jax import lax
from jax.experimental import pallas as pl
from jax.experimental.pallas import tpu as pltpu

D_MODEL = 1024
BATCH = 16
SEQ = 256
DEPTH = 2
DEC_BATCH = 2
DEC_SEQ = 4096
PAST_LEN = 256
GRID_W = 64
N_HEADS = 16
N_KV_HEADS = 4
HEAD_DIM = 64
WINDOW = 128
BLOCK = 128
ROPE_BASE = 10000.0
N_FREQ = HEAD_DIM // 4
POOL_WINDOWS = (2, 4, 8, 16)
POOL_GROUP_DIM = D_MODEL // 4
D_FF = 2816
N_MOD = 9
LN_EPS = 1e-5
DEEPNORM_ALPHA = (2.0 * DEPTH) ** 0.25
ATTN_SCALE = HEAD_DIM ** -0.5
LOG2_E = 1.4426950408889634
Q_SCALE = ATTN_SCALE * LOG2_E
NEG_INF = -1e30

N_PROMPT_TOK = BATCH * SEQ
N_LATENT_TOK = DEC_BATCH * DEC_SEQ
N_TOK = N_PROMPT_TOK + N_LATENT_TOK
MOD_GROUP_ROWS = 4096
N_COND = 1 + DEC_BATCH
COND_ROWS = 8

LANES = 128
HALF_LANES = LANES // 2
TM = 1024
N_PROMPT_TILES = N_PROMPT_TOK // TM
FF_CHUNK = 256
EPILOGUE_ROWS = 256
POOL_TM = 256
POOL_HALO = 8
D_Q = N_HEADS * HEAD_DIM
D_KV = N_KV_HEADS * HEAD_DIM
D_KV_AB = N_KV_HEADS * 2 * LANES
VMEM_LIMIT = 40 * 1024 * 1024
FFN_VMEM_LIMIT = 55 * 1024 * 1024

F32 = jnp.float32
BF16 = jnp.bfloat16


def _layer_norm(y, g, b):
    mu = jnp.mean(y, axis=-1, keepdims=True)
    yc = y - mu
    var = jnp.mean(yc * yc, axis=-1, keepdims=True)
    return yc * lax.rsqrt(var + LN_EPS) * g + b


def _modulate(x, mod_ref, row0):
    shift = mod_ref[row0:row0 + 1, :]
    scale = mod_ref[row0 + 1:row0 + 2, :]
    return x * (1.0 + scale) + shift


def _params(n_axes=1, vmem_limit=VMEM_LIMIT):
    return pltpu.CompilerParams(
        dimension_semantics=("arbitrary",) * n_axes,
        vmem_limit_bytes=vmem_limit)


def _resident(shape):
    nd = len(shape)
    return pl.BlockSpec(shape, lambda *_: (0,) * nd, pipeline_mode=pl.Buffered(1))


def _resident_at(index, tail):
    return pl.BlockSpec((None,) * len(index) + tuple(tail),
                        lambda *_: tuple(index) + (0,) * len(tail), pipeline_mode=pl.Buffered(1))


def _mod_spec(tile_rows, layer):
    tiles_per_group = MOD_GROUP_ROWS // tile_rows
    return pl.BlockSpec((None, None, N_MOD, D_MODEL),
                        lambda i: (layer, i // tiles_per_group, 0, 0))


def _ln_specs(layer, which):
    return [_resident_at((layer * 3 + which,), (1, D_MODEL))] * 2


def _slab_spec(cols, tm=TM):
    return pl.BlockSpec((tm, cols), lambda i: (i, 0))


def _prompt_spec(cols, tm=TM):
    return pl.BlockSpec((tm, cols), lambda i: (jnp.minimum(i, N_PROMPT_TOK // tm - 1), 0))


def _latent_spec(cols, tm=TM):
    return pl.BlockSpec((tm, cols), lambda i: (jnp.maximum(i - N_PROMPT_TOK // tm, 0), 0))


def _read_split(p_ref, l_ref, n_prompt_tiles=N_PROMPT_TILES):
    return jnp.where(pl.program_id(0) < n_prompt_tiles, p_ref[...], l_ref[...])


ADALN_NC = 2304


def _adaln_kernel(cond_ref, w_ref, b_ref, o_ref):
    c = cond_ref[...]
    s = (c * (1.0 / (1.0 + jnp.exp(-c)))).astype(BF16)
    o_ref[0] = jnp.dot(s, w_ref[0].astype(BF16), preferred_element_type=F32) + b_ref[0]


def _adaln(cond, w_mod, b_mod):
    n_out = N_MOD * D_MODEL
    out = pl.pallas_call(
        _adaln_kernel,
        grid=(DEPTH, n_out // ADALN_NC),
        in_specs=[
            pl.BlockSpec((COND_ROWS, D_MODEL), lambda l, n: (0, 0)),
            pl.BlockSpec((1, D_MODEL, ADALN_NC), lambda l, n: (l, 0, n)),
            pl.BlockSpec((1, 1, ADALN_NC), lambda l, n: (l, 0, n)),
        ],
        out_specs=pl.BlockSpec((1, COND_ROWS, ADALN_NC), lambda l, n: (l, 0, n)),
        out_shape=jax.ShapeDtypeStruct((DEPTH, COND_ROWS, n_out), F32),
        compiler_params=_params(2),
        name="adaln",
    )(cond, w_mod, b_mod.reshape(DEPTH, 1, n_out))
    return out[:, :N_COND].reshape(DEPTH, N_COND, N_MOD, D_MODEL)


def _ffn_matmuls(x, mod_ref, wg_ref, wu_ref, wd_ref, a_ref, row0):
    h = _modulate(x, mod_ref, row0).astype(BF16)
    for c in range(D_FF // FF_CHUNK):
        sl = slice(c * FF_CHUNK, (c + 1) * FF_CHUNK)
        g = jnp.dot(h, wg_ref[:, sl], preferred_element_type=F32)
        u = jnp.dot(h, wu_ref[:, sl], preferred_element_type=F32)
        a_ref[:, sl] = (g * (1.0 / (1.0 + jnp.exp(-g))) * u).astype(BF16)
    return jnp.dot(a_ref[...], wd_ref[...], preferred_element_type=F32)


def _ffn_epilogue(x, f, mod_ref, lng_ref, lnb_ref, row0):
    gate = mod_ref[row0 + 2:row0 + 3, :]
    y = DEEPNORM_ALPHA * x + (0.5 * gate) * f
    return _layer_norm(y, lng_ref[...], lnb_ref[...])


CAST_STEPS = 8
OWN_CHUNKS = 16
STAGE_SLOTS = 4
FFN_WEIGHT_SHAPES = ((D_MODEL, D_FF), (D_MODEL, D_FF), (D_FF, D_MODEL))


def _weight_chunk_copy(src_ref, lead, stage_ref, sem_ref, chunk, slot):
    rows = stage_ref.shape[1]
    src = src_ref.at[lead[0], lead[1], pl.ds(chunk * rows, rows), :]
    return pltpu.make_async_copy(src, stage_ref.at[slot], sem_ref.at[slot])


def _load_own_weights(lead, srcs, dsts, stages, sems):
    jobs, used = [], {}
    for src, dst in zip(srcs, dsts):
        key = dst.shape[1]
        for chunk in range(OWN_CHUNKS):
            slot = used.get(key, 0) % STAGE_SLOTS
            used[key] = used.get(key, 0) + 1
            jobs.append((_weight_chunk_copy(src, lead, stages[key], sems[key], chunk, slot),
                         stages[key], slot, dst, chunk))
    ahead = STAGE_SLOTS - 1
    for job in jobs[:ahead]:
        job[0].start()
    for n, (copy, stage, slot, dst, chunk) in enumerate(jobs):
        if n + ahead < len(jobs):
            jobs[n + ahead][0].start()
        copy.wait()
        rows = stage.shape[1]
        dst[chunk * rows:(chunk + 1) * rows, :] = stage[slot].astype(BF16)


def _ffn_kernel(*refs, row0, n_prompt_tiles, split_in, split_out, cast_next, own_lead):
    n_x = 2 if split_in else 1
    n_cast = 3 if cast_next else 0
    n_own = 7 if own_lead is not None else 0
    own = refs[len(refs) - n_own:]
    refs = refs[:len(refs) - n_own]
    x_refs, (mod_ref, lng_ref, lnb_ref, wg_ref, wu_ref, wd_ref) = refs[:n_x], refs[n_x:n_x + 6]
    next_f32 = refs[n_x + 6:n_x + 6 + n_cast]
    o_refs, next_bf16, a_ref = refs[n_x + 6 + n_cast:-1 - n_cast], refs[-1 - n_cast:-1], refs[-1]
    step = pl.program_id(0)
    if own_lead is not None:
        own_w, stage_a, stage_b, sem_a, sem_b = own[:3], own[3], own[4], own[5], own[6]
        stages = {stage_a.shape[2]: stage_a, stage_b.shape[2]: stage_b}
        sems = {stage_a.shape[2]: sem_a, stage_b.shape[2]: sem_b}

        @pl.when(step == 0)
        def _():
            _load_own_weights(own_lead, (wg_ref, wu_ref, wd_ref), own_w, stages, sems)

        wg_ref, wu_ref, wd_ref = own_w
    x = _read_split(*x_refs, n_prompt_tiles) if split_in else x_refs[0][...]
    f = _ffn_matmuls(x, mod_ref, wg_ref, wu_ref, wd_ref, a_ref, row0)
    out = _ffn_epilogue(x, f, mod_ref, lng_ref, lnb_ref, row0)
    if split_out:
        @pl.when(step < n_prompt_tiles)
        def _():
            o_refs[0][...] = out

        @pl.when(step >= n_prompt_tiles)
        def _():
            o_refs[1][...] = out
    else:
        o_refs[0][...] = out
    if cast_next:
        @pl.when(step < CAST_STEPS)
        def _():
            for src, dst in zip(next_f32, next_bf16):
                dst[...] = src[...].astype(BF16)


def _cast_block_spec(shape, lead):
    block = (None,) * len(lead) + (shape[0] // CAST_STEPS, shape[1])
    return pl.BlockSpec(block, lambda i: tuple(lead) + (jnp.minimum(i, CAST_STEPS - 1), 0))


def _ffn(xs, mods, ln_g, ln_b, weights, layer, which, next_f32=None, split_out=False):
    split_in = len(xs) == 2
    cast_next = next_f32 is not None
    own_lead = (layer, which) if weights is None else None
    row0 = 6 * which
    ln_row = 2 * which
    tm = TM // 2 if split_in and cast_next else TM
    x_specs = ([_prompt_spec(D_MODEL, tm), _latent_spec(D_MODEL, tm)] if split_in
               else [_slab_spec(D_MODEL, tm)])
    if split_out:
        out_specs = [_prompt_spec(D_MODEL, tm), _latent_spec(D_MODEL, tm)]
        out_shape = [jax.ShapeDtypeStruct((N_PROMPT_TOK, D_MODEL), F32),
                     jax.ShapeDtypeStruct((N_LATENT_TOK, D_MODEL), F32)]
    else:
        out_specs = [_slab_spec(D_MODEL, tm)]
        out_shape = [jax.ShapeDtypeStruct((N_TOK, D_MODEL), F32)]
    n_x_out = len(out_specs)
    cast_in_specs, cast_args = [], ()
    if cast_next:
        cast_args, lead = next_f32
        cast_in_specs = [_cast_block_spec(s, lead) for s in FFN_WEIGHT_SHAPES]
        out_specs = out_specs + [_cast_block_spec(s, ()) for s in FFN_WEIGHT_SHAPES]
        out_shape = out_shape + [jax.ShapeDtypeStruct(s, BF16) for s in FFN_WEIGHT_SHAPES]
    scratch = [pltpu.VMEM((tm, D_FF), BF16)]
    if own_lead is None:
        weight_specs = [_resident(s) for s in FFN_WEIGHT_SHAPES]
    else:
        weights = next_f32[0]
        weight_specs = [pl.BlockSpec(memory_space=pl.ANY)] * 3
        stage_shapes = sorted({(STAGE_SLOTS, s[0] // OWN_CHUNKS, s[1]) for s in FFN_WEIGHT_SHAPES})
        scratch += [pltpu.VMEM(s, BF16) for s in FFN_WEIGHT_SHAPES]
        scratch += [pltpu.VMEM(s, F32) for s in stage_shapes]
        scratch += [pltpu.SemaphoreType.DMA((STAGE_SLOTS,)) for _ in stage_shapes]
    outs = pl.pallas_call(
        functools.partial(_ffn_kernel, row0=row0, n_prompt_tiles=N_PROMPT_TOK // tm,
                          split_in=split_in, split_out=split_out, cast_next=cast_next,
                          own_lead=own_lead),
        grid=(N_TOK // tm,),
        in_specs=x_specs + [_mod_spec(tm, layer)] + _ln_specs(layer, ln_row)
        + weight_specs + cast_in_specs,
        out_specs=out_specs,
        out_shape=out_shape,
        scratch_shapes=scratch,
        compiler_params=_params(vmem_limit=FFN_VMEM_LIMIT),
        name="ffn",
    )(*xs, mods, ln_g, ln_b, *weights, *cast_args)
    x_out = tuple(outs[:n_x_out]) if split_out else outs[0]
    return x_out, (tuple(outs[n_x_out:]) if cast_next else None)


ROPE_TILES = DEC_SEQ // TM


def _write_ab(dst_ref, rows, pair, src):
    lo = lax.broadcasted_iota(jnp.int32, src.shape, 1) < HALF_LANES
    swapped = pltpu.roll(src, HALF_LANES, 1)
    blocks = (jnp.where(lo, src, 0.0), jnp.where(lo, 0.0, swapped),
              jnp.where(lo, swapped, 0.0), jnp.where(lo, 0.0, src))
    for n, blk in enumerate(blocks):
        c0 = (4 * pair + n) * LANES
        dst_ref[rows, c0:c0 + LANES] = blk.astype(BF16)


def _qkv_kernel(x_ref, mod_ref, w_ref, cos_ref, sup_ref, sdn_ref,
                q_ref, kab_ref, vab_ref, ks_ref, vs_ref):
    kv_raw = []
    is_prompt = pl.program_id(0) < N_PROMPT_TILES
    for r in range(TM // EPILOGUE_ROWS):
        rows = slice(r * EPILOGUE_ROWS, (r + 1) * EPILOGUE_ROWS)
        h = _modulate(x_ref[rows, :], mod_ref, 3).astype(BF16)
        qkv = jnp.dot(h, w_ref[...], preferred_element_type=F32)
        cos = jnp.where(is_prompt, 1.0, cos_ref[rows, :])
        s_up = jnp.where(is_prompt, 0.0, sup_ref[rows, :])
        s_dn = jnp.where(is_prompt, 0.0, sdn_ref[rows, :])

        def rope(blk):
            up = pltpu.roll(blk, LANES - 16, 1)
            dn = pltpu.roll(blk, 16, 1)
            return blk * cos + up * s_up + dn * s_dn

        for j in range(D_Q // LANES):
            cols = slice(j * LANES, (j + 1) * LANES)
            q_ref[rows, cols] = (rope(qkv[:, cols]) * Q_SCALE).astype(BF16)
        for pair in range(D_KV // LANES):
            k_cols = slice(D_Q + pair * LANES, D_Q + (pair + 1) * LANES)
            v_cols = slice(D_Q + D_KV + pair * LANES, D_Q + D_KV + (pair + 1) * LANES)
            _write_ab(kab_ref, rows, pair, rope(qkv[:, k_cols]))
            _write_ab(vab_ref, rows, pair, qkv[:, v_cols])
        kv_raw.append(qkv[:, D_Q:])

    @pl.when(pl.program_id(0) < N_PROMPT_TILES)
    def _():
        kv_t = jnp.concatenate(kv_raw, axis=0).T
        for s in range(TM // SEQ):
            ks_ref[s] = kv_t[:D_KV, s * SEQ:(s + 1) * SEQ]
            vs_ref[s] = kv_t[D_KV:, s * SEQ:(s + 1) * SEQ]


def _rope_index(i):
    return (jnp.maximum(i - N_PROMPT_TILES, 0) % ROPE_TILES, 0)


def _qkv(x, mods, w, cos_t, sup_t, sdn_t):
    seqs_per_tile = TM // SEQ
    state_spec = pl.BlockSpec((seqs_per_tile, D_KV, SEQ),
                              lambda i: (jnp.minimum(i, N_PROMPT_TILES - 1), 0, 0))
    return pl.pallas_call(
        _qkv_kernel,
        grid=(N_TOK // TM,),
        in_specs=[
            _slab_spec(D_MODEL),
            _mod_spec(TM, 0),
            _resident((D_MODEL, D_Q + 2 * D_KV)),
            pl.BlockSpec((TM, LANES), _rope_index),
            pl.BlockSpec((TM, LANES), _rope_index),
            pl.BlockSpec((TM, LANES), _rope_index),
        ],
        out_specs=[
            _slab_spec(D_Q), _slab_spec(D_KV_AB), _slab_spec(D_KV_AB), state_spec, state_spec,
        ],
        out_shape=[
            jax.ShapeDtypeStruct((N_TOK, D_Q), BF16),
            jax.ShapeDtypeStruct((N_TOK, D_KV_AB), BF16),
            jax.ShapeDtypeStruct((N_TOK, D_KV_AB), BF16),
            jax.ShapeDtypeStruct((BATCH, D_KV, SEQ), F32),
            jax.ShapeDtypeStruct((BATCH, D_KV, SEQ), F32),
        ],
        compiler_params=_params(),
        name="qkv",
    )(x, mods, w, cos_t, sup_t, sdn_t)


def _attend(q_ref, o_ref, sink_ref, k_slabs, v_slabs, masks, tq, row0=0):
    nt = (((1,), (1,)), ((), ()))
    rows = slice(row0, row0 + tq)
    row_hi = lax.broadcasted_iota(jnp.int32, (2 * tq, 1), 0) >= tq
    lane_lo = lax.broadcasted_iota(jnp.int32, (2 * tq, LANES), 1) < HALF_LANES
    for kh in range(N_KV_HEADS):
        ks = k_slabs(kh)
        vs = v_slabs(kh)
        k_cat = jnp.concatenate([s[:, :LANES] for s in ks] + [s[:, LANES:] for s in ks], axis=0)
        v_cat = jnp.concatenate([s[:, :LANES] for s in vs] + [s[:, LANES:] for s in vs], axis=0)
        n_keys = k_cat.shape[0] // 2
        j0 = 2 * kh
        q2 = jnp.concatenate([q_ref[rows, j0 * LANES:(j0 + 1) * LANES],
                              q_ref[rows, (j0 + 1) * LANES:(j0 + 2) * LANES]], axis=0)
        s = lax.dot_general(q2, k_cat, nt, preferred_element_type=F32)
        es, inv_dens = [], []
        for half in range(2):
            segs, off = [], half * n_keys
            for slab, m in zip(ks, masks):
                seg = s[:, off:off + slab.shape[0]]
                segs.append(seg if m is None else jnp.where(m, seg, NEG_INF))
                off += slab.shape[0]
            logits = jnp.concatenate(segs, axis=1)
            sink = LOG2_E * jnp.where(row_hi, sink_ref[4 * kh + 2 + half], sink_ref[4 * kh + half])
            m_row = jnp.maximum(jnp.max(logits, axis=-1, keepdims=True), sink)
            e = jnp.exp2(logits - m_row)
            den = jnp.sum(e, axis=-1, keepdims=True) + jnp.exp2(sink - m_row)
            es.append(e.astype(BF16))
            inv_dens.append(1.0 / den)
        p = jnp.concatenate(es, axis=1)
        o2 = jnp.dot(p, v_cat, preferred_element_type=F32)
        o2 = o2 * jnp.where(lane_lo, inv_dens[0], inv_dens[1])
        o_ref[rows, j0 * LANES:(j0 + 1) * LANES] = o2[:tq].astype(BF16)
        o_ref[rows, (j0 + 1) * LANES:(j0 + 2) * LANES] = o2[tq:].astype(BF16)


def _kv_cols(kh):
    return slice(kh * 2 * LANES, (kh + 1) * 2 * LANES)


CTX_SUB = 2


def _ctx_attn_kernel(sink_ref, q_ref, k_ref, v_ref, o_ref):
    for sub in range(CTX_SUB):
        rows = slice(sub * SEQ, (sub + 1) * SEQ)
        _attend(q_ref, o_ref, sink_ref,
                lambda kh, rows=rows: [k_ref[rows, _kv_cols(kh)]],
                lambda kh, rows=rows: [v_ref[rows, _kv_cols(kh)]],
                [None], SEQ, row0=sub * SEQ)


def _ctx_attn(sink, q, kab, vab):
    step_rows = CTX_SUB * SEQ
    return pl.pallas_call(
        _ctx_attn_kernel,
        grid=(BATCH // CTX_SUB,),
        in_specs=[
            pl.BlockSpec(memory_space=pltpu.SMEM),
            pl.BlockSpec((step_rows, D_Q), lambda b: (b, 0)),
            pl.BlockSpec((step_rows, D_KV_AB), lambda b: (b, 0)),
            pl.BlockSpec((step_rows, D_KV_AB), lambda b: (b, 0)),
        ],
        out_specs=pl.BlockSpec((step_rows, D_Q), lambda b: (b, 0)),
        out_shape=jax.ShapeDtypeStruct((N_PROMPT_TOK, D_Q), BF16),
        compiler_params=_params(),
        name="ctx_attn",
    )(sink, q, kab, vab)


N_QBLK = DEC_SEQ // BLOCK


LAT_SUB = 4
LAT_STEPS = N_QBLK // LAT_SUB


def _lat_attn_kernel(sink_ref, q_ref, kp_ref, kc_ref, kn_ref, vp_ref, vc_ref, vn_ref,
                     kx_ref, vx_ref, o_ref):
    step = pl.program_id(1)
    r = lax.broadcasted_iota(jnp.int32, (2 * BLOCK, BLOCK), 0) & (BLOCK - 1)
    c = lax.broadcasted_iota(jnp.int32, (2 * BLOCK, BLOCK), 1)

    def blocks(p_ref, c_ref, n_ref, kh):
        cols = _kv_cols(kh)
        return ([p_ref[:, cols]]
                + [c_ref[j * BLOCK:(j + 1) * BLOCK, cols] for j in range(LAT_SUB)]
                + [n_ref[:, cols]])

    for sub in range(LAT_SUB):
        no_prev = (step == 0) if sub == 0 else False
        no_next = (step == LAT_STEPS - 1) if sub == LAT_SUB - 1 else False
        m_prev = c >= r + jnp.where(no_prev, BLOCK, 0)
        m_next = c <= r - jnp.where(no_next, BLOCK, 0)

        def slabs(p_ref, c_ref, n_ref, x_ref, sub=sub):
            return lambda kh: blocks(p_ref, c_ref, n_ref, kh)[sub:sub + 3] + [x_ref[0, :, _kv_cols(kh)]]

        _attend(q_ref, o_ref, sink_ref,
                slabs(kp_ref, kc_ref, kn_ref, kx_ref), slabs(vp_ref, vc_ref, vn_ref, vx_ref),
                [m_prev, None, m_next, None], BLOCK, row0=sub * BLOCK)


def _lat_attn(sink, q, kab, vab, kab_ctx, vab_ctx):
    tq = LAT_SUB * BLOCK
    first = N_PROMPT_TOK // BLOCK

    def cur(b, i):
        return (N_PROMPT_TOK // tq + b * LAT_STEPS + i, 0)

    def prev(b, i):
        return (first + b * N_QBLK + jnp.maximum(i * LAT_SUB - 1, 0), 0)

    def nxt(b, i):
        return (first + b * N_QBLK + jnp.minimum((i + 1) * LAT_SUB, N_QBLK - 1), 0)

    edge_spec = lambda f: pl.BlockSpec((BLOCK, D_KV_AB), f)
    cur_spec = pl.BlockSpec((tq, D_KV_AB), cur)
    ctx_spec = pl.BlockSpec((1, PAST_LEN, D_KV_AB), lambda b, i: (b, 0, 0))
    return pl.pallas_call(
        _lat_attn_kernel,
        grid=(DEC_BATCH, LAT_STEPS),
        in_specs=[
            pl.BlockSpec(memory_space=pltpu.SMEM),
            pl.BlockSpec((tq, D_Q), cur),
            edge_spec(prev), cur_spec, edge_spec(nxt),
            edge_spec(prev), cur_spec, edge_spec(nxt),
            ctx_spec, ctx_spec,
        ],
        out_specs=pl.BlockSpec((tq, D_Q), lambda b, i: (b * LAT_STEPS + i, 0)),
        out_shape=jax.ShapeDtypeStruct((N_LATENT_TOK, D_Q), BF16),
        compiler_params=_params(2),
        name="lat_attn",
    )(sink, q, kab, kab, kab, vab, vab, vab, kab_ctx, vab_ctx)


def _proj_kernel(x_ref, ap_ref, al_ref, mod_ref, lng_ref, lnb_ref, wo_ref, o_ref):
    a = _read_split(ap_ref, al_ref)
    gate = mod_ref[5:6, :]
    for r in range(TM // EPILOGUE_ROWS):
        rows = slice(r * EPILOGUE_ROWS, (r + 1) * EPILOGUE_ROWS)
        f = jnp.dot(a[rows], wo_ref[...], preferred_element_type=F32)
        y = DEEPNORM_ALPHA * x_ref[rows, :] + gate * f
        o_ref[rows, :] = _layer_norm(y, lng_ref[...], lnb_ref[...])


def _proj(x, attn_prompt, attn_latent, mods, ln_g, ln_b, w_o):
    return pl.pallas_call(
        _proj_kernel,
        grid=(N_TOK // TM,),
        in_specs=[
            _slab_spec(D_MODEL),
            _prompt_spec(D_Q),
            _latent_spec(D_Q),
            _mod_spec(TM, 0)] + _ln_specs(0, 1) + [
            _resident((D_Q, D_MODEL)),
        ],
        out_specs=_slab_spec(D_MODEL),
        out_shape=jax.ShapeDtypeStruct((N_TOK, D_MODEL), F32),
        compiler_params=_params(),
        name="attn_proj",
    )(x, attn_prompt, attn_latent, mods, ln_g, ln_b, w_o)


POOL_SUB = 4
POOL_STEP_ROWS = POOL_SUB * POOL_TM
PROMPT_POOL_STEPS = N_PROMPT_TOK // POOL_STEP_ROWS
LATENT_POOL_STEPS = DEC_SEQ // POOL_STEP_ROWS


def _pool_kernel(x_ref, xp_ref, xn_ref, mod_ref, lng_ref, lnb_ref, w_ref, sc_ref, o_ref):
    i = pl.program_id(0)
    is_prompt = i < PROMPT_POOL_STEPS
    in_seq = (i - PROMPT_POOL_STEPS) % LATENT_POOL_STEPS
    h_all = _modulate(x_ref[...], mod_ref, 3)
    h_before = _modulate(xp_ref[...], mod_ref, 3)
    h_after = _modulate(xn_ref[...], mod_ref, 3)
    gate_scale = mod_ref[5:6, :] * sc_ref[...]
    for sub in range(POOL_SUB):
        rows = slice(sub * POOL_TM, (sub + 1) * POOL_TM)
        is_start = is_prompt | (in_seq == 0) if sub == 0 else is_prompt
        is_end = is_prompt | (in_seq == LATENT_POOL_STEPS - 1) if sub == POOL_SUB - 1 else is_prompt
        before = h_before if sub == 0 else h_all[rows.start - POOL_HALO:rows.start]
        after = h_after if sub == POOL_SUB - 1 else h_all[rows.stop:rows.stop + POOL_HALO]
        _pool_sub_tile(x_ref[rows, :], h_all[rows], before, after, is_start, is_end, gate_scale,
                       lng_ref, lnb_ref, w_ref, o_ref, rows)


def _pool_sub_tile(x, h, before, after, is_start, is_end, gate_scale, lng_ref, lnb_ref, w_ref,
                   o_ref, rows):
    h_ext = jnp.concatenate([jnp.where(is_start, 0.0, before), h, jnp.where(is_end, 0.0, after)],
                            axis=0)
    n_ext = POOL_TM + 2 * POOL_HALO

    r8 = lax.broadcasted_iota(jnp.int32, (POOL_HALO, POOL_GROUP_DIM), 0)
    outs = []
    for gi, w in enumerate(POOL_WINDOWS):
        half = w // 2
        cols = slice(gi * POOL_GROUP_DIM, (gi + 1) * POOL_GROUP_DIM)
        acc = h_ext[:, cols]
        s = 1
        while s < w:
            acc = acc + pltpu.roll(acc, s, 0)
            s *= 2
        if half > 1:
            acc = pltpu.roll(acc, n_ext - (half - 1), 0)
        total = acc[POOL_HALO:POOL_HALO + POOL_TM]
        cnt_top = w - jnp.where(is_start, jnp.maximum(half - r8, 0), 0)
        cnt_bot = w - jnp.where(is_end, jnp.maximum(r8 + half - POOL_HALO, 0), 0)
        inv_cnt = jnp.concatenate([
            1.0 / cnt_top.astype(F32),
            jnp.full((POOL_TM - 2 * POOL_HALO, POOL_GROUP_DIM), 1.0 / w, F32),
            1.0 / cnt_bot.astype(F32)], axis=0)
        pooled = (total * inv_cnt - h[:, cols]).astype(BF16)
        outs.append(jnp.dot(pooled, w_ref[gi], preferred_element_type=F32))
    y = DEEPNORM_ALPHA * x + gate_scale * jnp.concatenate(outs, axis=-1)
    o_ref[rows, :] = _layer_norm(y, lng_ref[...], lnb_ref[...])


def _pool(x, mods, ln_g, ln_b, w_pool, scale):
    halo_per_tile = POOL_STEP_ROWS // POOL_HALO
    last_halo = N_TOK // POOL_HALO - 1
    return pl.pallas_call(
        _pool_kernel,
        grid=(N_TOK // POOL_STEP_ROWS,),
        in_specs=[
            pl.BlockSpec((POOL_STEP_ROWS, D_MODEL), lambda i: (i, 0)),
            pl.BlockSpec((POOL_HALO, D_MODEL), lambda i: (jnp.maximum(i * halo_per_tile - 1, 0), 0)),
            pl.BlockSpec((POOL_HALO, D_MODEL),
                         lambda i: (jnp.minimum((i + 1) * halo_per_tile, last_halo), 0)),
            _mod_spec(POOL_STEP_ROWS, 1)] + _ln_specs(1, 1) + [
            _resident((len(POOL_WINDOWS), POOL_GROUP_DIM, POOL_GROUP_DIM)),
            _resident((1, D_MODEL)),
        ],
        out_specs=pl.BlockSpec((POOL_STEP_ROWS, D_MODEL), lambda i: (i, 0)),
        out_shape=jax.ShapeDtypeStruct((N_TOK, D_MODEL), F32),
        compiler_params=_params(),
        name="pool",
    )(x, x, x, mods, ln_g, ln_b, w_pool, scale.reshape(1, D_MODEL))


def _ab_layout(a):
    lead = a.shape[:-1]
    a = a.reshape(lead + (N_KV_HEADS, 1, HEAD_DIM)).astype(BF16)
    z = jnp.zeros_like(a)
    return jnp.concatenate([a, z, z, a], axis=-2).reshape(lead + (D_KV_AB,))


def _rope_tables():
    n_rows = DEC_SEQ // GRID_W
    inv = jnp.power(ROPE_BASE, -jnp.arange(N_FREQ, dtype=F32) / N_FREQ)

    def expand(fn):
        by_row = fn(jnp.arange(n_rows, dtype=F32)[:, None] * inv)
        by_col = fn(jnp.arange(GRID_W, dtype=F32)[:, None] * inv)
        r = jnp.broadcast_to(by_row[:, None, :], (n_rows, GRID_W, N_FREQ))
        c = jnp.broadcast_to(by_col[None, :, :], (n_rows, GRID_W, N_FREQ))
        per_head = jnp.concatenate([r, r, c, c], axis=-1).reshape(DEC_SEQ, HEAD_DIM)
        return jnp.concatenate([per_head] * (LANES // HEAD_DIM), axis=-1)

    cos, sin = expand(jnp.cos), expand(jnp.sin)
    first_half = (jnp.arange(LANES) % 32) < 16
    s_up = jnp.where(first_half, -sin, 0.0)
    s_dn = jnp.where(first_half, 0.0, sin)
    return cos, s_up, s_dn


def kernel(x_prompt, x_sample, cache_k, cache_v, c, c_ctx, w_mod, b_mod, ln_g, ln_b,
           ffn_w_gate, ffn_w_up, ffn_w_down, attn_w_qkv, attn_w_o, attn_sink,
           pool_w, pool_scale):
    cond = jnp.concatenate(
        [c_ctx[None, :], c, jnp.zeros((COND_ROWS - N_COND, D_MODEL), F32)], axis=0)
    mods = _adaln(cond, w_mod, b_mod)

    ffn_w = (ffn_w_gate, ffn_w_up, ffn_w_down)
    ln_g = ln_g.reshape(DEPTH * 3, 1, D_MODEL)
    ln_b = ln_b.reshape(DEPTH * 3, 1, D_MODEL)
    ffn = functools.partial(_ffn, mods=mods, ln_g=ln_g, ln_b=ln_b)

    x, w_next = ffn((x_prompt.reshape(N_PROMPT_TOK, D_MODEL), x_sample.reshape(N_LATENT_TOK, D_MODEL)),
                    weights=None, layer=0, which=0, next_f32=(ffn_w, (0, 1)))
    cos_t, sup_t, sdn_t = _rope_tables()
    q, kab, vab, k_state, v_state = _qkv(x, mods, attn_w_qkv[0].astype(BF16), cos_t, sup_t, sdn_t)
    sink = attn_sink[0]
    o_ctx = _ctx_attn(sink, q, kab, vab)
    kab_ctx = _ab_layout(cache_k[:, 0].reshape(DEC_BATCH, PAST_LEN, D_KV))
    vab_ctx = _ab_layout(cache_v[:, 0].reshape(DEC_BATCH, PAST_LEN, D_KV))
    o_lat = _lat_attn(sink, q, kab, vab, kab_ctx, vab_ctx)
    x = _proj(x, o_ctx, o_lat, mods, ln_g, ln_b, attn_w_o[0].astype(BF16))
    x, w_next = ffn((x,), weights=w_next, layer=0, which=1, next_f32=(ffn_w, (1, 0)))

    x, w_next = ffn((x,), weights=w_next, layer=1, which=0, next_f32=(ffn_w, (1, 1)))
    x = _pool(x, mods, ln_g, ln_b, pool_w[0].astype(BF16), pool_scale[0])
    (y_prompt, y_sample), _ = ffn((x,), weights=w_next, layer=1, which=1, split_out=True)

    def state(s_t):
        s_t = s_t.reshape(BATCH, N_KV_HEADS, HEAD_DIM, SEQ)
        return jnp.transpose(s_t, (0, 3, 1, 2))[:, None]

    return (y_prompt.reshape(BATCH, SEQ, D_MODEL),
            y_sample.reshape(DEC_BATCH, DEC_SEQ, D_MODEL),
            state(k_state), state(v_state))
```

```python
import functools

import jax
import jax.numpy as jnp
from jax import lax
from jax.experimental import pallas as pl
from jax.experimental.pallas import tpu as pltpu

D_MODEL = 1024
BATCH = 16
SEQ = 256
DEPTH = 2
DEC_BATCH = 2
DEC_SEQ = 4096
PAST_LEN = 256
GRID_W = 64
N_HEADS = 16
N_KV_HEADS = 4
HEAD_DIM = 64
WINDOW = 128
BLOCK = 128
ROPE_BASE = 10000.0
N_FREQ = HEAD_DIM // 4
POOL_WINDOWS = (2, 4, 8, 16)
POOL_GROUP_DIM = D_MODEL // 4
D_FF = 2816
N_MOD = 9
LN_EPS = 1e-5
DEEPNORM_ALPHA = (2.0 * DEPTH) ** 0.25
ATTN_SCALE = HEAD_DIM ** -0.5
LOG2_E = 1.4426950408889634
Q_SCALE = ATTN_SCALE * LOG2_E
NEG_INF = -1e30

N_PROMPT_TOK = BATCH * SEQ
N_LATENT_TOK = DEC_BATCH * DEC_SEQ
N_TOK = N_PROMPT_TOK + N_LATENT_TOK
MOD_GROUP_ROWS = 4096
N_COND = 1 + DEC_BATCH
COND_ROWS = 8

LANES = 128
HALF_LANES = LANES // 2
TM = 1024
N_PROMPT_TILES = N_PROMPT_TOK // TM
FF_CHUNK = 256
EPILOGUE_ROWS = 256
POOL_TM = 256
POOL_HALO = 8
D_Q = N_HEADS * HEAD_DIM
D_KV = N_KV_HEADS * HEAD_DIM
D_KV_AB = N_KV_HEADS * 2 * LANES
VMEM_LIMIT = 40 * 1024 * 1024
FFN_VMEM_LIMIT = 55 * 1024 * 1024

F32 = jnp.float32
BF16 = jnp.bfloat16


def _layer_norm(y, g, b):
    mu = jnp.mean(y, axis=-1, keepdims=True)
    yc = y - mu
    var = jnp.mean(yc * yc, axis=-1, keepdims=True)
    return yc * lax.rsqrt(var + LN_EPS) * g + b


def _modulate(x, mod_ref, row0):
    shift = mod_ref[row0:row0 + 1, :]
    scale = mod_ref[row0 + 1:row0 + 2, :]
    return x * (1.0 + scale) + shift


def _params(n_axes=1, vmem_limit=VMEM_LIMIT):
    return pltpu.CompilerParams(
        dimension_semantics=("arbitrary",) * n_axes,
        vmem_limit_bytes=vmem_limit)


def _resident(shape):
    nd = len(shape)
    return pl.BlockSpec(shape, lambda *_: (0,) * nd, pipeline_mode=pl.Buffered(1))


def _resident_at(index, tail):
    return pl.BlockSpec((None,) * len(index) + tuple(tail),
                        lambda *_: tuple(index) + (0,) * len(tail), pipeline_mode=pl.Buffered(1))


def _mod_spec(tile_rows, layer):
    tiles_per_group = MOD_GROUP_ROWS // tile_rows
    return pl.BlockSpec((None, None, N_MOD, D_MODEL),
                        lambda i: (layer, i // tiles_per_group, 0, 0))


def _ln_specs(layer, which):
    return [_resident_at((layer * 3 + which,), (1, D_MODEL))] * 2


def _slab_spec(cols, tm=TM):
    return pl.BlockSpec((tm, cols), lambda i: (i, 0))


def _prompt_spec(cols, tm=TM):
    return pl.BlockSpec((tm, cols), lambda i: (jnp.minimum(i, N_PROMPT_TOK // tm - 1), 0))


def _latent_spec(cols, tm=TM):
    return pl.BlockSpec((tm, cols), lambda i: (jnp.maximum(i - N_PROMPT_TOK // tm, 0), 0))


def _read_split(p_ref, l_ref, n_prompt_tiles=N_PROMPT_TILES):
    return jnp.where(pl.program_id(0) < n_prompt_tiles, p_ref[...], l_ref[...])


ADALN_NC = 2304


def _adaln_kernel(cond_ref, w_ref, b_ref, o_ref):
    c = cond_ref[...]
    s = (c * (1.0 / (1.0 + jnp.exp(-c)))).astype(BF16)
    o_ref[0] = jnp.dot(s, w_ref[0].astype(BF16), preferred_element_type=F32) + b_ref[0]


def _adaln(cond, w_mod, b_mod):
    n_out = N_MOD * D_MODEL
    out = pl.pallas_call(
        _adaln_kernel,
        grid=(DEPTH, n_out // ADALN_NC),
        in_specs=[
            pl.BlockSpec((COND_ROWS, D_MODEL), lambda l, n: (0, 0)),
            pl.BlockSpec((1, D_MODEL, ADALN_NC), lambda l, n: (l, 0, n)),
            pl.BlockSpec((1, 1, ADALN_NC), lambda l, n: (l, 0, n)),
        ],
        out_specs=pl.BlockSpec((1, COND_ROWS, ADALN_NC), lambda l, n: (l, 0, n)),
        out_shape=jax.ShapeDtypeStruct((DEPTH, COND_ROWS, n_out), F32),
        compiler_params=_params(2),
        name="adaln",
    )(cond, w_mod, b_mod.reshape(DEPTH, 1, n_out))
    return out[:, :N_COND].reshape(DEPTH, N_COND, N_MOD, D_MODEL)


def _ffn_matmuls(x, mod_ref, wg_ref, wu_ref, wd_ref, a_ref, row0):
    h = _modulate(x, mod_ref, row0).astype(BF16)
    for c0 in range(0, D_FF, FF_CHUNK):
        sl = slice(c0, min(c0 + FF_CHUNK, D_FF))
        g = jnp.dot(h, wg_ref[:, sl], preferred_element_type=F32)
        u = jnp.dot(h, wu_ref[:, sl], preferred_element_type=F32)
        a_ref[:, sl] = (g * (1.0 / (1.0 + jnp.exp(-g))) * u).astype(BF16)
    return jnp.dot(a_ref[...], wd_ref[...], preferred_element_type=F32)


def _ffn_epilogue(x, f, mod_ref, lng_ref, lnb_ref, row0):
    gate = mod_ref[row0 + 2:row0 + 3, :]
    y = DEEPNORM_ALPHA * x + (0.5 * gate) * f
    return _layer_norm(y, lng_ref[...], lnb_ref[...])


CAST_STEPS = 8
OWN_CHUNKS = 16
STAGE_SLOTS = 4
FFN_WEIGHT_SHAPES = ((D_MODEL, D_FF), (D_MODEL, D_FF), (D_FF, D_MODEL))


def _weight_chunk_copy(src_ref, lead, stage_ref, sem_ref, chunk, slot):
    rows = stage_ref.shape[1]
    src = src_ref.at[lead[0], lead[1], pl.ds(chunk * rows, rows), :]
    return pltpu.make_async_copy(src, stage_ref.at[slot], sem_ref.at[slot])


def _load_own_weights(lead, srcs, dsts, stages, sems):
    jobs, used = [], {}
    for src, dst in zip(srcs, dsts):
        key = dst.shape[1]
        for chunk in range(OWN_CHUNKS):
            slot = used.get(key, 0) % STAGE_SLOTS
            used[key] = used.get(key, 0) + 1
            jobs.append((_weight_chunk_copy(src, lead, stages[key], sems[key], chunk, slot),
                         stages[key], slot, dst, chunk))
    ahead = STAGE_SLOTS - 1
    for job in jobs[:ahead]:
        job[0].start()
    for n, (copy, stage, slot, dst, chunk) in enumerate(jobs):
        if n + ahead < len(jobs):
            jobs[n + ahead][0].start()
        copy.wait()
        rows = stage.shape[1]
        dst[chunk * rows:(chunk + 1) * rows, :] = stage[slot].astype(BF16)


def _ffn_kernel(*refs, row0, n_prompt_tiles, split_in, split_out, cast_next, own_lead):
    n_x = 2 if split_in else 1
    n_cast = 3 if cast_next else 0
    n_own = 7 if own_lead is not None else 0
    own = refs[len(refs) - n_own:]
    refs = refs[:len(refs) - n_own]
    x_refs, (mod_ref, lng_ref, lnb_ref, wg_ref, wu_ref, wd_ref) = refs[:n_x], refs[n_x:n_x + 6]
    next_f32 = refs[n_x + 6:n_x + 6 + n_cast]
    o_refs, next_bf16, a_ref = refs[n_x + 6 + n_cast:-1 - n_cast], refs[-1 - n_cast:-1], refs[-1]
    step = pl.program_id(0)
    if own_lead is not None:
        own_w, stage_a, stage_b, sem_a, sem_b = own[:3], own[3], own[4], own[5], own[6]
        stages = {stage_a.shape[2]: stage_a, stage_b.shape[2]: stage_b}
        sems = {stage_a.shape[2]: sem_a, stage_b.shape[2]: sem_b}

        @pl.when(step == 0)
        def _():
            _load_own_weights(own_lead, (wg_ref, wu_ref, wd_ref), own_w, stages, sems)

        wg_ref, wu_ref, wd_ref = own_w
    x = _read_split(*x_refs, n_prompt_tiles) if split_in else x_refs[0][...]
    f = _ffn_matmuls(x, mod_ref, wg_ref, wu_ref, wd_ref, a_ref, row0)
    out = _ffn_epilogue(x, f, mod_ref, lng_ref, lnb_ref, row0)
    if split_out:
        @pl.when(step < n_prompt_tiles)
        def _():
            o_refs[0][...] = out

        @pl.when(step >= n_prompt_tiles)
        def _():
            o_refs[1][...] = out
    else:
        o_refs[0][...] = out
    if cast_next:
        @pl.when(step < CAST_STEPS)
        def _():
            for src, dst in zip(next_f32, next_bf16):
                dst[...] = src[...].astype(BF16)


def _cast_block_spec(shape, lead):
    block = (None,) * len(lead) + (shape[0] // CAST_STEPS, shape[1])
    return pl.BlockSpec(block, lambda i: tuple(lead) + (jnp.minimum(i, CAST_STEPS - 1), 0))


def _ffn(xs, mods, ln_g, ln_b, weights, layer, which, next_f32=None, split_out=False):
    split_in = len(xs) == 2
    cast_next = next_f32 is not None
    own_lead = (layer, which) if weights is None else None
    row0 = 6 * which
    ln_row = 2 * which
    tm = TM // 2 if split_in and cast_next else TM
    x_specs = ([_prompt_spec(D_MODEL, tm), _latent_spec(D_MODEL, tm)] if split_in
               else [_slab_spec(D_MODEL, tm)])
    if split_out:
        out_specs = [_prompt_spec(D_MODEL, tm), _latent_spec(D_MODEL, tm)]
        out_shape = [jax.ShapeDtypeStruct((N_PROMPT_TOK, D_MODEL), F32),
                     jax.ShapeDtypeStruct((N_LATENT_TOK, D_MODEL), F32)]
    else:
        out_specs = [_slab_spec(D_MODEL, tm)]
        out_shape = [jax.ShapeDtypeStruct((N_TOK, D_MODEL), F32)]
    n_x_out = len(out_specs)
    cast_in_specs, cast_args = [], ()
    if cast_next:
        cast_args, lead = next_f32
        cast_in_specs = [_cast_block_spec(s, lead) for s in FFN_WEIGHT_SHAPES]
        out_specs = out_specs + [_cast_block_spec(s, ()) for s in FFN_WEIGHT_SHAPES]
        out_shape = out_shape + [jax.ShapeDtypeStruct(s, BF16) for s in FFN_WEIGHT_SHAPES]
    scratch = [pltpu.VMEM((tm, D_FF), BF16)]
    if own_lead is None:
        weight_specs = [_resident(s) for s in FFN_WEIGHT_SHAPES]
    else:
        weights = next_f32[0]
        weight_specs = [pl.BlockSpec(memory_space=pl.ANY)] * 3
        stage_shapes = sorted({(STAGE_SLOTS, s[0] // OWN_CHUNKS, s[1]) for s in FFN_WEIGHT_SHAPES})
        scratch += [pltpu.VMEM(s, BF16) for s in FFN_WEIGHT_SHAPES]
        scratch += [pltpu.VMEM(s, F32) for s in stage_shapes]
        scratch += [pltpu.SemaphoreType.DMA((STAGE_SLOTS,)) for _ in stage_shapes]
    outs = pl.pallas_call(
        functools.partial(_ffn_kernel, row0=row0, n_prompt_tiles=N_PROMPT_TOK // tm,
                          split_in=split_in, split_out=split_out, cast_next=cast_next,
                          own_lead=own_lead),
        grid=(N_TOK // tm,),
        in_specs=x_specs + [_mod_spec(tm, layer)] + _ln_specs(layer, ln_row)
        + weight_specs + cast_in_specs,
        out_specs=out_specs,
        out_shape=out_shape,
        scratch_shapes=scratch,
        compiler_params=_params(vmem_limit=FFN_VMEM_LIMIT),
        name="ffn",
    )(*xs, mods, ln_g, ln_b, *weights, *cast_args)
    x_out = tuple(outs[:n_x_out]) if split_out else outs[0]
    return x_out, (tuple(outs[n_x_out:]) if cast_next else None)


ROPE_TILES = DEC_SEQ // TM


def _write_ab(dst_ref, rows, pair, src):
    lo = lax.broadcasted_iota(jnp.int32, src.shape, 1) < HALF_LANES
    swapped = pltpu.roll(src, HALF_LANES, 1)
    blocks = (jnp.where(lo, src, 0.0), jnp.where(lo, 0.0, swapped),
              jnp.where(lo, swapped, 0.0), jnp.where(lo, 0.0, src))
    for n, blk in enumerate(blocks):
        c0 = (4 * pair + n) * LANES
        dst_ref[rows, c0:c0 + LANES] = blk.astype(BF16)


def _qkv_kernel(x_ref, mod_ref, w_ref, cos_ref, sin_ref, q_ref, kab_ref, vab_ref, ks_ref, vs_ref):
    kv_raw = []
    is_prompt = pl.program_id(0) < N_PROMPT_TILES
    first_half = (lax.broadcasted_iota(jnp.int32, (EPILOGUE_ROWS, LANES), 1) & 16) == 0
    for r in range(TM // EPILOGUE_ROWS):
        rows = slice(r * EPILOGUE_ROWS, (r + 1) * EPILOGUE_ROWS)
        h = _modulate(x_ref[rows, :], mod_ref, 3).astype(BF16)
        qkv = jnp.dot(h, w_ref[...], preferred_element_type=F32)
        cos = jnp.where(is_prompt, 1.0, cos_ref[rows, :])
        sin = jnp.where(is_prompt, 0.0, sin_ref[rows, :])

        def rope(blk, c, s):
            up = pltpu.roll(blk, LANES - 16, 1)
            dn = pltpu.roll(blk, 16, 1)
            return blk * c + jnp.where(first_half, up, dn) * s

        cos_q, sin_q = cos * Q_SCALE, sin * Q_SCALE
        for j in range(D_Q // LANES):
            cols = slice(j * LANES, (j + 1) * LANES)
            q_ref[rows, cols] = rope(qkv[:, cols], cos_q, sin_q).astype(BF16)
        for pair in range(D_KV // LANES):
            k_cols = slice(D_Q + pair * LANES, D_Q + (pair + 1) * LANES)
            v_cols = slice(D_Q + D_KV + pair * LANES, D_Q + D_KV + (pair + 1) * LANES)
            _write_ab(kab_ref, rows, pair, rope(qkv[:, k_cols], cos, sin))
            _write_ab(vab_ref, rows, pair, qkv[:, v_cols])
        kv_raw.append(qkv[:, D_Q:])

    @pl.when(pl.program_id(0) < N_PROMPT_TILES)
    def _():
        kv_t = jnp.concatenate(kv_raw, axis=0).T
        for s in range(TM // SEQ):
            ks_ref[s] = kv_t[:D_KV, s * SEQ:(s + 1) * SEQ]
            vs_ref[s] = kv_t[D_KV:, s * SEQ:(s + 1) * SEQ]


def _rope_index(i):
    return (jnp.maximum(i - N_PROMPT_TILES, 0) % ROPE_TILES, 0)


def _qkv(x, mods, w, cos_t, sin_t):
    seqs_per_tile = TM // SEQ
    state_spec = pl.BlockSpec((seqs_per_tile, D_KV, SEQ),
                              lambda i: (jnp.minimum(i, N_PROMPT_TILES - 1), 0, 0))
    return pl.pallas_call(
        _qkv_kernel,
        grid=(N_TOK // TM,),
        in_specs=[
            _slab_spec(D_MODEL),
            _mod_spec(TM, 0),
            _resident((D_MODEL, D_Q + 2 * D_KV)),
            pl.BlockSpec((TM, LANES), _rope_index),
            pl.BlockSpec((TM, LANES), _rope_index),
        ],
        out_specs=[
            _slab_spec(D_Q), _slab_spec(D_KV_AB), _slab_spec(D_KV_AB), state_spec, state_spec,
        ],
        out_shape=[
            jax.ShapeDtypeStruct((N_TOK, D_Q), BF16),
            jax.ShapeDtypeStruct((N_TOK, D_KV_AB), BF16),
            jax.ShapeDtypeStruct((N_TOK, D_KV_AB), BF16),
            jax.ShapeDtypeStruct((BATCH, D_KV, SEQ), F32),
            jax.ShapeDtypeStruct((BATCH, D_KV, SEQ), F32),
        ],
        compiler_params=_params(),
        name="qkv",
    )(x, mods, w, cos_t, sin_t)


def _attend(q_ref, o_ref, sink_ref, k_slabs, v_slabs, masks, tq, row0=0):
    nt = (((1,), (1,)), ((), ()))
    rows = slice(row0, row0 + tq)
    row_hi = lax.broadcasted_iota(jnp.int32, (2 * tq, 1), 0) >= tq
    lane_lo = lax.broadcasted_iota(jnp.int32, (2 * tq, LANES), 1) < HALF_LANES
    for kh in range(N_KV_HEADS):
        ks = k_slabs(kh)
        vs = v_slabs(kh)
        k_cat = jnp.concatenate([s[:, :LANES] for s in ks] + [s[:, LANES:] for s in ks], axis=0)
        v_cat = jnp.concatenate([s[:, :LANES] for s in vs] + [s[:, LANES:] for s in vs], axis=0)
        n_keys = k_cat.shape[0] // 2
        j0 = 2 * kh
        q2 = jnp.concatenate([q_ref[rows, j0 * LANES:(j0 + 1) * LANES],
                              q_ref[rows, (j0 + 1) * LANES:(j0 + 2) * LANES]], axis=0)
        s = lax.dot_general(q2, k_cat, nt, preferred_element_type=F32)
        es, inv_dens = [], []
        for half in range(2):
            segs, off = [], half * n_keys
            for slab, m in zip(ks, masks):
                seg = s[:, off:off + slab.shape[0]]
                segs.append(seg if m is None else jnp.where(m, seg, NEG_INF))
                off += slab.shape[0]
            logits = jnp.concatenate(segs, axis=1)
            sink = LOG2_E * jnp.where(row_hi, sink_ref[4 * kh + 2 + half], sink_ref[4 * kh + half])
            m_row = jnp.maximum(jnp.max(logits, axis=-1, keepdims=True), sink)
            e = jnp.exp2(logits - m_row)
            den = jnp.sum(e, axis=-1, keepdims=True) + jnp.exp2(sink - m_row)
            es.append(e.astype(BF16))
            inv_dens.append(1.0 / den)
        p = jnp.concatenate(es, axis=1)
        o2 = jnp.dot(p, v_cat, preferred_element_type=F32)
        o2 = o2 * jnp.where(lane_lo, inv_dens[0], inv_dens[1])
        o_ref[rows, j0 * LANES:(j0 + 1) * LANES] = o2[:tq].astype(BF16)
        o_ref[rows, (j0 + 1) * LANES:(j0 + 2) * LANES] = o2[tq:].astype(BF16)


def _kv_cols(kh):
    return slice(kh * 2 * LANES, (kh + 1) * 2 * LANES)


CTX_SUB = 4


def _ctx_attn_kernel(sink_ref, q_ref, k_ref, v_ref, o_ref):
    for sub in range(CTX_SUB):
        rows = slice(sub * SEQ, (sub + 1) * SEQ)
        _attend(q_ref, o_ref, sink_ref,
                lambda kh, rows=rows: [k_ref[rows, _kv_cols(kh)]],
                lambda kh, rows=rows: [v_ref[rows, _kv_cols(kh)]],
                [None], SEQ, row0=sub * SEQ)


def _ctx_attn(sink, q, kab, vab):
    step_rows = CTX_SUB * SEQ
    return pl.pallas_call(
        _ctx_attn_kernel,
        grid=(BATCH // CTX_SUB,),
        in_specs=[
            pl.BlockSpec(memory_space=pltpu.SMEM),
            pl.BlockSpec((step_rows, D_Q), lambda b: (b, 0)),
            pl.BlockSpec((step_rows, D_KV_AB), lambda b: (b, 0)),
            pl.BlockSpec((step_rows, D_KV_AB), lambda b: (b, 0)),
        ],
        out_specs=pl.BlockSpec((step_rows, D_Q), lambda b: (b, 0)),
        out_shape=jax.ShapeDtypeStruct((N_PROMPT_TOK, D_Q), BF16),
        compiler_params=_params(),
        name="ctx_attn",
    )(sink, q, kab, vab)


N_QBLK = DEC_SEQ // BLOCK


LAT_SUB = 4
LAT_STEPS = N_QBLK // LAT_SUB


def _lat_attn_kernel(sink_ref, q_ref, kp_ref, kc_ref, kn_ref, vp_ref, vc_ref, vn_ref,
                     kx_ref, vx_ref, o_ref):
    step = pl.program_id(1)
    r = lax.broadcasted_iota(jnp.int32, (2 * BLOCK, BLOCK), 0) & (BLOCK - 1)
    c = lax.broadcasted_iota(jnp.int32, (2 * BLOCK, BLOCK), 1)

    def blocks(p_ref, c_ref, n_ref, kh):
        cols = _kv_cols(kh)
        return ([p_ref[:, cols]]
                + [c_ref[j * BLOCK:(j + 1) * BLOCK, cols] for j in range(LAT_SUB)]
                + [n_ref[:, cols]])

    for sub in range(LAT_SUB):
        no_prev = (step == 0) if sub == 0 else False
        no_next = (step == LAT_STEPS - 1) if sub == LAT_SUB - 1 else False
        m_prev = c >= r + jnp.where(no_prev, BLOCK, 0)
        m_next = c <= r - jnp.where(no_next, BLOCK, 0)

        def slabs(p_ref, c_ref, n_ref, x_ref, sub=sub):
            return lambda kh: blocks(p_ref, c_ref, n_ref, kh)[sub:sub + 3] + [x_ref[0, :, _kv_cols(kh)]]

        _attend(q_ref, o_ref, sink_ref,
                slabs(kp_ref, kc_ref, kn_ref, kx_ref), slabs(vp_ref, vc_ref, vn_ref, vx_ref),
                [m_prev, None, m_next, None], BLOCK, row0=sub * BLOCK)


def _lat_attn(sink, q, kab, vab, kab_ctx, vab_ctx):
    tq = LAT_SUB * BLOCK
    first = N_PROMPT_TOK // BLOCK

    def cur(b, i):
        return (N_PROMPT_TOK // tq + b * LAT_STEPS + i, 0)

    def prev(b, i):
        return (first + b * N_QBLK + jnp.maximum(i * LAT_SUB - 1, 0), 0)

    def nxt(b, i):
        return (first + b * N_QBLK + jnp.minimum((i + 1) * LAT_SUB, N_QBLK - 1), 0)

    edge_spec = lambda f: pl.BlockSpec((BLOCK, D_KV_AB), f)
    cur_spec = pl.BlockSpec((tq, D_KV_AB), cur)
    ctx_spec = pl.BlockSpec((1, PAST_LEN, D_KV_AB), lambda b, i: (b, 0, 0))
    return pl.pallas_call(
        _lat_attn_kernel,
        grid=(DEC_BATCH, LAT_STEPS),
        in_specs=[
            pl.BlockSpec(memory_space=pltpu.SMEM),
            pl.BlockSpec((tq, D_Q), cur),
            edge_spec(prev), cur_spec, edge_spec(nxt),
            edge_spec(prev), cur_spec, edge_spec(nxt),
            ctx_spec, ctx_spec,
        ],
        out_specs=pl.BlockSpec((tq, D_Q), lambda b, i: (b * LAT_STEPS + i, 0)),
        out_shape=jax.ShapeDtypeStruct((N_LATENT_TOK, D_Q), BF16),
        compiler_params=_params(2),
        name="lat_attn",
    )(sink, q, kab, kab, kab, vab, vab, vab, kab_ctx, vab_ctx)


def _proj_kernel(x_ref, ap_ref, al_ref, mod_ref, lng_ref, lnb_ref, wo_ref, o_ref):
    a = _read_split(ap_ref, al_ref)
    gate = mod_ref[5:6, :]
    for r in range(TM // EPILOGUE_ROWS):
        rows = slice(r * EPILOGUE_ROWS, (r + 1) * EPILOGUE_ROWS)
        f = jnp.dot(a[rows], wo_ref[...], preferred_element_type=F32)
        y = DEEPNORM_ALPHA * x_ref[rows, :] + gate * f
        o_ref[rows, :] = _layer_norm(y, lng_ref[...], lnb_ref[...])


def _proj(x, attn_prompt, attn_latent, mods, ln_g, ln_b, w_o):
    return pl.pallas_call(
        _proj_kernel,
        grid=(N_TOK // TM,),
        in_specs=[
            _slab_spec(D_MODEL),
            _prompt_spec(D_Q),
            _latent_spec(D_Q),
            _mod_spec(TM, 0)] + _ln_specs(0, 1) + [
            _resident((D_Q, D_MODEL)),
        ],
        out_specs=_slab_spec(D_MODEL),
        out_shape=jax.ShapeDtypeStruct((N_TOK, D_MODEL), F32),
        compiler_params=_params(),
        name="attn_proj",
    )(x, attn_prompt, attn_latent, mods, ln_g, ln_b, w_o)


POOL_SUB = 4
POOL_STEP_ROWS = POOL_SUB * POOL_TM
PROMPT_POOL_STEPS = N_PROMPT_TOK // POOL_STEP_ROWS
LATENT_POOL_STEPS = DEC_SEQ // POOL_STEP_ROWS


def _pool_kernel(x_ref, xp_ref, xn_ref, mod_ref, lng_ref, lnb_ref, w_ref, sc_ref, o_ref):
    i = pl.program_id(0)
    is_prompt = i < PROMPT_POOL_STEPS
    in_seq = (i - PROMPT_POOL_STEPS) % LATENT_POOL_STEPS
    h_all = _modulate(x_ref[...], mod_ref, 3)
    h_before = _modulate(xp_ref[...], mod_ref, 3)
    h_after = _modulate(xn_ref[...], mod_ref, 3)
    gate_scale = mod_ref[5:6, :] * sc_ref[...]
    for sub in range(POOL_SUB):
        rows = slice(sub * POOL_TM, (sub + 1) * POOL_TM)
        is_start = is_prompt | (in_seq == 0) if sub == 0 else is_prompt
        is_end = is_prompt | (in_seq == LATENT_POOL_STEPS - 1) if sub == POOL_SUB - 1 else is_prompt
        before = h_before if sub == 0 else h_all[rows.start - POOL_HALO:rows.start]
        after = h_after if sub == POOL_SUB - 1 else h_all[rows.stop:rows.stop + POOL_HALO]
        _pool_sub_tile(x_ref[rows, :], h_all[rows], before, after, is_start, is_end, gate_scale,
                       lng_ref, lnb_ref, w_ref, o_ref, rows)


def _pool_sub_tile(x, h, before, after, is_start, is_end, gate_scale, lng_ref, lnb_ref, w_ref,
                   o_ref, rows):
    h_ext = jnp.concatenate([jnp.where(is_start, 0.0, before), h, jnp.where(is_end, 0.0, after)],
                            axis=0)
    n_ext = POOL_TM + 2 * POOL_HALO

    r8 = lax.broadcasted_iota(jnp.int32, (POOL_HALO, POOL_GROUP_DIM), 0)
    outs = []
    for gi, w in enumerate(POOL_WINDOWS):
        half = w // 2
        cols = slice(gi * POOL_GROUP_DIM, (gi + 1) * POOL_GROUP_DIM)
        acc = h_ext[:, cols]
        s = 1
        while s < w:
            acc = acc + pltpu.roll(acc, s, 0)
            s *= 2
        if half > 1:
            acc = pltpu.roll(acc, n_ext - (half - 1), 0)
        total = acc[POOL_HALO:POOL_HALO + POOL_TM]
        cnt_top = w - jnp.where(is_start, jnp.maximum(half - r8, 0), 0)
        cnt_bot = w - jnp.where(is_end, jnp.maximum(r8 + half - POOL_HALO, 0), 0)
        inv_cnt = jnp.concatenate([
            1.0 / cnt_top.astype(F32),
            jnp.full((POOL_TM - 2 * POOL_HALO, POOL_GROUP_DIM), 1.0 / w, F32),
            1.0 / cnt_bot.astype(F32)], axis=0)
        pooled = (total * inv_cnt - h[:, cols]).astype(BF16)
        outs.append(jnp.dot(pooled, w_ref[gi], preferred_element_type=F32))
    y = DEEPNORM_ALPHA * x + gate_scale * jnp.concatenate(outs, axis=-1)
    o_ref[rows, :] = _layer_norm(y, lng_ref[...], lnb_ref[...])


def _pool(x, mods, ln_g, ln_b, w_pool, scale):
    halo_per_tile = POOL_STEP_ROWS // POOL_HALO
    last_halo = N_TOK // POOL_HALO - 1
    return pl.pallas_call(
        _pool_kernel,
        grid=(N_TOK // POOL_STEP_ROWS,),
        in_specs=[
            pl.BlockSpec((POOL_STEP_ROWS, D_MODEL), lambda i: (i, 0)),
            pl.BlockSpec((POOL_HALO, D_MODEL), lambda i: (jnp.maximum(i * halo_per_tile - 1, 0), 0)),
            pl.BlockSpec((POOL_HALO, D_MODEL),
                         lambda i: (jnp.minimum((i + 1) * halo_per_tile, last_halo), 0)),
            _mod_spec(POOL_STEP_ROWS, 1)] + _ln_specs(1, 1) + [
            _resident((len(POOL_WINDOWS), POOL_GROUP_DIM, POOL_GROUP_DIM)),
            _resident((1, D_MODEL)),
        ],
        out_specs=pl.BlockSpec((POOL_STEP_ROWS, D_MODEL), lambda i: (i, 0)),
        out_shape=jax.ShapeDtypeStruct((N_TOK, D_MODEL), F32),
        compiler_params=_params(),
        name="pool",
    )(x, x, x, mods, ln_g, ln_b, w_pool, scale.reshape(1, D_MODEL))


def _ab_layout(a):
    lead = a.shape[:-1]
    a = a.reshape(lead + (N_KV_HEADS, 1, HEAD_DIM)).astype(BF16)
    z = jnp.zeros_like(a)
    return jnp.concatenate([a, z, z, a], axis=-2).reshape(lead + (D_KV_AB,))


def _rope_tables():
    n_rows = DEC_SEQ // GRID_W
    inv = jnp.power(ROPE_BASE, -jnp.arange(N_FREQ, dtype=F32) / N_FREQ)

    def expand(fn):
        by_row = fn(jnp.arange(n_rows, dtype=F32)[:, None] * inv)
        by_col = fn(jnp.arange(GRID_W, dtype=F32)[:, None] * inv)
        r = jnp.broadcast_to(by_row[:, None, :], (n_rows, GRID_W, N_FREQ))
        c = jnp.broadcast_to(by_col[None, :, :], (n_rows, GRID_W, N_FREQ))
        per_head = jnp.concatenate([r, r, c, c], axis=-1).reshape(DEC_SEQ, HEAD_DIM)
        return jnp.concatenate([per_head] * (LANES // HEAD_DIM), axis=-1)

    cos, sin = expand(jnp.cos), expand(jnp.sin)
    first_half = (jnp.arange(LANES) % 32) < 16
    return cos, jnp.where(first_half, -sin, sin)


def kernel(x_prompt, x_sample, cache_k, cache_v, c, c_ctx, w_mod, b_mod, ln_g, ln_b,
           ffn_w_gate, ffn_w_up, ffn_w_down, attn_w_qkv, attn_w_o, attn_sink,
           pool_w, pool_scale):
    cond = jnp.concatenate(
        [c_ctx[None, :], c, jnp.zeros((COND_ROWS - N_COND, D_MODEL), F32)], axis=0)
    mods = _adaln(cond, w_mod, b_mod)

    ffn_w = (ffn_w_gate, ffn_w_up, ffn_w_down)
    ln_g = ln_g.reshape(DEPTH * 3, 1, D_MODEL)
    ln_b = ln_b.reshape(DEPTH * 3, 1, D_MODEL)
    ffn = functools.partial(_ffn, mods=mods, ln_g=ln_g, ln_b=ln_b)

    x, w_next = ffn((x_prompt.reshape(N_PROMPT_TOK, D_MODEL), x_sample.reshape(N_LATENT_TOK, D_MODEL)),
                    weights=None, layer=0, which=0, next_f32=(ffn_w, (0, 1)))
    cos_t, sin_t = _rope_tables()
    q, kab, vab, k_state, v_state = _qkv(x, mods, attn_w_qkv[0].astype(BF16), cos_t, sin_t)
    sink = attn_sink[0]
    o_ctx = _ctx_attn(sink, q, kab, vab)
    kab_ctx = _ab_layout(cache_k[:, 0].reshape(DEC_BATCH, PAST_LEN, D_KV))
    vab_ctx = _ab_layout(cache_v[:, 0].reshape(DEC_BATCH, PAST_LEN, D_KV))
    o_lat = _lat_attn(sink, q, kab, vab, kab_ctx, vab_ctx)
    x = _proj(x, o_ctx, o_lat, mods, ln_g, ln_b, attn_w_o[0].astype(BF16))
    x, w_next = ffn((x,), weights=w_next, layer=0, which=1, next_f32=(ffn_w, (1, 0)))

    x, w_next = ffn((x,), weights=w_next, layer=1, which=0, next_f32=(ffn_w, (1, 1)))
    x = _pool(x, mods, ln_g, ln_b, pool_w[0].astype(BF16), pool_scale[0])
    (y_prompt, y_sample), _ = ffn((x,), weights=w_next, layer=1, which=1, split_out=True)

    def state(s_t):
        s_t = s_t.reshape(BATCH, N_KV_HEADS, HEAD_DIM, SEQ)
        return jnp.transpose(s_t, (0, 3, 1, 2))[:, None]

    return (y_prompt.reshape(BATCH, SEQ, D_MODEL),
            y_sample.reshape(DEC_BATCH, DEC_SEQ, D_MODEL),
            state(k_state), state(v_state))
```

```python
import functools

import jax
import jax.numpy as jnp
from jax import lax
from jax.experimental import pallas as pl
from jax.experimental.pallas import tpu as pltpu

D_MODEL = 1024
BATCH = 16
SEQ = 256
DEPTH = 2
DEC_BATCH = 2
DEC_SEQ = 4096
PAST_LEN = 256
GRID_W = 64
N_HEADS = 16
N_KV_HEADS = 4
HEAD_DIM = 64
WINDOW = 128
BLOCK = 128
ROPE_BASE = 10000.0
N_FREQ = HEAD_DIM // 4
POOL_WINDOWS = (2, 4, 8, 16)
POOL_GROUP_DIM = D_MODEL // 4
D_FF = 2816
N_MOD = 9
LN_EPS = 1e-5
DEEPNORM_ALPHA = (2.0 * DEPTH) ** 0.25
ATTN_SCALE = HEAD_DIM ** -0.5
LOG2_E = 1.4426950408889634
Q_SCALE = ATTN_SCALE * LOG2_E
NEG_INF = -1e30

N_PROMPT_TOK = BATCH * SEQ
N_LATENT_TOK = DEC_BATCH * DEC_SEQ
N_TOK = N_PROMPT_TOK + N_LATENT_TOK
MOD_GROUP_ROWS = 4096
N_COND = 1 + DEC_BATCH
COND_ROWS = 8

LANES = 128
HALF_LANES = LANES // 2
TM = 1024
N_PROMPT_TILES = N_PROMPT_TOK // TM
FF_CHUNK = 256
EPILOGUE_ROWS = 256
POOL_TM = 256
POOL_HALO = 8
D_Q = N_HEADS * HEAD_DIM
D_KV = N_KV_HEADS * HEAD_DIM
D_KV_AB = N_KV_HEADS * 2 * LANES
VMEM_LIMIT = 40 * 1024 * 1024
FFN_VMEM_LIMIT = 55 * 1024 * 1024

F32 = jnp.float32
BF16 = jnp.bfloat16


def _layer_norm(y, g, b):
    mu = jnp.mean(y, axis=-1, keepdims=True)
    yc = y - mu
    var = jnp.mean(yc * yc, axis=-1, keepdims=True)
    return yc * lax.rsqrt(var + LN_EPS) * g + b


def _modulate(x, mod_ref, row0):
    shift = mod_ref[row0:row0 + 1, :]
    scale = mod_ref[row0 + 1:row0 + 2, :]
    return x * (1.0 + scale) + shift


def _params(n_axes=1, vmem_limit=VMEM_LIMIT):
    return pltpu.CompilerParams(
        dimension_semantics=("arbitrary",) * n_axes,
        vmem_limit_bytes=vmem_limit)


def _resident(shape):
    nd = len(shape)
    return pl.BlockSpec(shape, lambda *_: (0,) * nd, pipeline_mode=pl.Buffered(1))


def _resident_at(index, tail):
    return pl.BlockSpec((None,) * len(index) + tuple(tail),
                        lambda *_: tuple(index) + (0,) * len(tail), pipeline_mode=pl.Buffered(1))


def _mod_spec(tile_rows, layer):
    tiles_per_group = MOD_GROUP_ROWS // tile_rows
    return pl.BlockSpec((None, None, N_MOD, D_MODEL),
                        lambda i: (layer, i // tiles_per_group, 0, 0))


def _ln_specs(layer, which):
    return [_resident_at((layer * 3 + which,), (1, D_MODEL))] * 2


def _slab_spec(cols, tm=TM):
    return pl.BlockSpec((tm, cols), lambda i: (i, 0))


def _prompt_spec(cols, tm=TM):
    return pl.BlockSpec((tm, cols), lambda i: (jnp.minimum(i, N_PROMPT_TOK // tm - 1), 0))


def _latent_spec(cols, tm=TM):
    return pl.BlockSpec((tm, cols), lambda i: (jnp.maximum(i - N_PROMPT_TOK // tm, 0), 0))


def _read_split(p_ref, l_ref, n_prompt_tiles=N_PROMPT_TILES):
    return jnp.where(pl.program_id(0) < n_prompt_tiles, p_ref[...], l_ref[...])


ADALN_NC = 2304


def _adaln_kernel(cond_ref, w_ref, b_ref, o_ref):
    c = cond_ref[...]
    s = (c * (1.0 / (1.0 + jnp.exp(-c)))).astype(BF16)
    o_ref[0] = jnp.dot(s, w_ref[0].astype(BF16), preferred_element_type=F32) + b_ref[0]


def _adaln(cond, w_mod, b_mod):
    n_out = N_MOD * D_MODEL
    out = pl.pallas_call(
        _adaln_kernel,
        grid=(DEPTH, n_out // ADALN_NC),
        in_specs=[
            pl.BlockSpec((COND_ROWS, D_MODEL), lambda l, n: (0, 0)),
            pl.BlockSpec((1, D_MODEL, ADALN_NC), lambda l, n: (l, 0, n)),
            pl.BlockSpec((1, 1, ADALN_NC), lambda l, n: (l, 0, n)),
        ],
        out_specs=pl.BlockSpec((1, COND_ROWS, ADALN_NC), lambda l, n: (l, 0, n)),
        out_shape=jax.ShapeDtypeStruct((DEPTH, COND_ROWS, n_out), F32),
        compiler_params=_params(2),
        name="adaln",
    )(cond, w_mod, b_mod.reshape(DEPTH, 1, n_out))
    return out[:, :N_COND].reshape(DEPTH, N_COND, N_MOD, D_MODEL)


def _ffn_gate_up(x, mod_ref, wg_ref, wu_ref, a_ref, row0):
    h = _modulate(x, mod_ref, row0).astype(BF16)
    for c0 in range(0, D_FF, FF_CHUNK):
        sl = slice(c0, min(c0 + FF_CHUNK, D_FF))
        g = jnp.dot(h, wg_ref[:, sl], preferred_element_type=F32)
        u = jnp.dot(h, wu_ref[:, sl], preferred_element_type=F32)
        a_ref[:, sl] = (g * (1.0 / (1.0 + jnp.exp(-g))) * u).astype(BF16)


def _ffn_epilogue(x, f, mod_ref, lng_ref, lnb_ref, row0):
    gate = mod_ref[row0 + 2:row0 + 3, :]
    y = DEEPNORM_ALPHA * x + (0.5 * gate) * f
    return _layer_norm(y, lng_ref[...], lnb_ref[...])


CAST_STEPS = 8
OWN_CHUNKS = 16
STAGE_SLOTS = 4
FFN_WEIGHT_SHAPES = ((D_MODEL, D_FF), (D_MODEL, D_FF), (D_FF, D_MODEL))


def _weight_chunk_copy(src_ref, lead, stage_ref, sem_ref, chunk, slot):
    rows = stage_ref.shape[1]
    src = src_ref.at[lead[0], lead[1], pl.ds(chunk * rows, rows), :]
    return pltpu.make_async_copy(src, stage_ref.at[slot], sem_ref.at[slot])


def _load_own_weights(lead, srcs, dsts, stages, sems):
    jobs, used = [], {}
    for src, dst in zip(srcs, dsts):
        key = dst.shape[1]
        for chunk in range(OWN_CHUNKS):
            slot = used.get(key, 0) % STAGE_SLOTS
            used[key] = used.get(key, 0) + 1
            jobs.append((_weight_chunk_copy(src, lead, stages[key], sems[key], chunk, slot),
                         stages[key], slot, dst, chunk))
    ahead = STAGE_SLOTS - 1
    for job in jobs[:ahead]:
        job[0].start()
    for n, (copy, stage, slot, dst, chunk) in enumerate(jobs):
        if n + ahead < len(jobs):
            jobs[n + ahead][0].start()
        copy.wait()
        rows = stage.shape[1]
        dst[chunk * rows:(chunk + 1) * rows, :] = stage[slot].astype(BF16)


def _ffn_kernel(*refs, row0, n_prompt_tiles, split_in, split_out, cast_next, own_lead, down_rows):
    n_x = 2 if split_in else 1
    n_cast = 3 if cast_next else 0
    n_own = 7 if own_lead is not None else 0
    own = refs[len(refs) - n_own:]
    refs = refs[:len(refs) - n_own]
    x_refs, (mod_ref, lng_ref, lnb_ref, wg_ref, wu_ref, wd_ref) = refs[:n_x], refs[n_x:n_x + 6]
    next_f32 = refs[n_x + 6:n_x + 6 + n_cast]
    o_refs, next_bf16, a_ref = refs[n_x + 6 + n_cast:-1 - n_cast], refs[-1 - n_cast:-1], refs[-1]
    step = pl.program_id(0)
    if own_lead is not None:
        own_w, stage_a, stage_b, sem_a, sem_b = own[:3], own[3], own[4], own[5], own[6]
        stages = {stage_a.shape[2]: stage_a, stage_b.shape[2]: stage_b}
        sems = {stage_a.shape[2]: sem_a, stage_b.shape[2]: sem_b}

        @pl.when(step == 0)
        def _():
            _load_own_weights(own_lead, (wg_ref, wu_ref, wd_ref), own_w, stages, sems)

        wg_ref, wu_ref, wd_ref = own_w
    x = _read_split(*x_refs, n_prompt_tiles) if split_in else x_refs[0][...]
    _ffn_gate_up(x, mod_ref, wg_ref, wu_ref, a_ref, row0)
    if split_out:
        f = jnp.dot(a_ref[...], wd_ref[...], preferred_element_type=F32)
        out = _ffn_epilogue(x, f, mod_ref, lng_ref, lnb_ref, row0)

        @pl.when(step < n_prompt_tiles)
        def _():
            o_refs[0][...] = out

        @pl.when(step >= n_prompt_tiles)
        def _():
            o_refs[1][...] = out
    else:
        n_rows = x.shape[0]
        for r0 in range(0, n_rows, down_rows or n_rows):
            rows = slice(r0, r0 + (down_rows or n_rows))
            f = jnp.dot(a_ref[rows, :], wd_ref[...], preferred_element_type=F32)
            o_refs[0][rows, :] = _ffn_epilogue(x[rows], f, mod_ref, lng_ref, lnb_ref, row0)
    if cast_next:
        @pl.when(step < CAST_STEPS)
        def _():
            for src, dst in zip(next_f32, next_bf16):
                dst[...] = src[...].astype(BF16)


def _cast_block_spec(shape, lead):
    block = (None,) * len(lead) + (shape[0] // CAST_STEPS, shape[1])
    return pl.BlockSpec(block, lambda i: tuple(lead) + (jnp.minimum(i, CAST_STEPS - 1), 0))


def _ffn(xs, mods, ln_g, ln_b, weights, layer, which, next_f32=None, split_out=False,
         down_rows=None):
    split_in = len(xs) == 2
    cast_next = next_f32 is not None
    own_lead = (layer, which) if weights is None else None
    row0 = 6 * which
    ln_row = 2 * which
    tm = TM // 2 if split_in and cast_next else TM
    x_specs = ([_prompt_spec(D_MODEL, tm), _latent_spec(D_MODEL, tm)] if split_in
               else [_slab_spec(D_MODEL, tm)])
    if split_out:
        out_specs = [_prompt_spec(D_MODEL, tm), _latent_spec(D_MODEL, tm)]
        out_shape = [jax.ShapeDtypeStruct((N_PROMPT_TOK, D_MODEL), F32),
                     jax.ShapeDtypeStruct((N_LATENT_TOK, D_MODEL), F32)]
    else:
        out_specs = [_slab_spec(D_MODEL, tm)]
        out_shape = [jax.ShapeDtypeStruct((N_TOK, D_MODEL), F32)]
    n_x_out = len(out_specs)
    cast_in_specs, cast_args = [], ()
    if cast_next:
        cast_args, lead = next_f32
        cast_in_specs = [_cast_block_spec(s, lead) for s in FFN_WEIGHT_SHAPES]
        out_specs = out_specs + [_cast_block_spec(s, ()) for s in FFN_WEIGHT_SHAPES]
        out_shape = out_shape + [jax.ShapeDtypeStruct(s, BF16) for s in FFN_WEIGHT_SHAPES]
    scratch = [pltpu.VMEM((tm, D_FF), BF16)]
    if own_lead is None:
        weight_specs = [_resident(s) for s in FFN_WEIGHT_SHAPES]
    else:
        weights = next_f32[0]
        weight_specs = [pl.BlockSpec(memory_space=pl.ANY)] * 3
        stage_shapes = sorted({(STAGE_SLOTS, s[0] // OWN_CHUNKS, s[1]) for s in FFN_WEIGHT_SHAPES})
        scratch += [pltpu.VMEM(s, BF16) for s in FFN_WEIGHT_SHAPES]
        scratch += [pltpu.VMEM(s, F32) for s in stage_shapes]
        scratch += [pltpu.SemaphoreType.DMA((STAGE_SLOTS,)) for _ in stage_shapes]
    outs = pl.pallas_call(
        functools.partial(_ffn_kernel, row0=row0, n_prompt_tiles=N_PROMPT_TOK // tm,
                          split_in=split_in, split_out=split_out, cast_next=cast_next,
                          own_lead=own_lead, down_rows=down_rows),
        grid=(N_TOK // tm,),
        in_specs=x_specs + [_mod_spec(tm, layer)] + _ln_specs(layer, ln_row)
        + weight_specs + cast_in_specs,
        out_specs=out_specs,
        out_shape=out_shape,
        scratch_shapes=scratch,
        compiler_params=_params(vmem_limit=FFN_VMEM_LIMIT),
        name="ffn",
    )(*xs, mods, ln_g, ln_b, *weights, *cast_args)
    x_out = tuple(outs[:n_x_out]) if split_out else outs[0]
    return x_out, (tuple(outs[n_x_out:]) if cast_next else None)


ROPE_TILES = DEC_SEQ // TM


def _write_ab(dst_ref, rows, pair, src):
    lo = lax.broadcasted_iota(jnp.int32, src.shape, 1) < HALF_LANES
    swapped = pltpu.roll(src, HALF_LANES, 1)
    blocks = (jnp.where(lo, src, 0.0), jnp.where(lo, 0.0, swapped),
              jnp.where(lo, swapped, 0.0), jnp.where(lo, 0.0, src))
    for n, blk in enumerate(blocks):
        c0 = (4 * pair + n) * LANES
        dst_ref[rows, c0:c0 + LANES] = blk.astype(BF16)


def _qkv_kernel(x_ref, mod_ref, w_ref, cos_ref, sin_ref, q_ref, kab_ref, vab_ref, ks_ref, vs_ref):
    kv_raw = []
    is_prompt = pl.program_id(0) < N_PROMPT_TILES
    first_half = (lax.broadcasted_iota(jnp.int32, (EPILOGUE_ROWS, LANES), 1) & 16) == 0
    for r in range(TM // EPILOGUE_ROWS):
        rows = slice(r * EPILOGUE_ROWS, (r + 1) * EPILOGUE_ROWS)
        h = _modulate(x_ref[rows, :], mod_ref, 3).astype(BF16)
        qkv = jnp.dot(h, w_ref[...], preferred_element_type=F32)
        cos = jnp.where(is_prompt, 1.0, cos_ref[rows, :])
        sin = jnp.where(is_prompt, 0.0, sin_ref[rows, :])

        def rope(blk, c, s):
            up = pltpu.roll(blk, LANES - 16, 1)
            dn = pltpu.roll(blk, 16, 1)
            return blk * c + jnp.where(first_half, up, dn) * s

        cos_q, sin_q = cos * Q_SCALE, sin * Q_SCALE
        for j in range(D_Q // LANES):
            cols = slice(j * LANES, (j + 1) * LANES)
            q_ref[rows, cols] = rope(qkv[:, cols], cos_q, sin_q).astype(BF16)
        for pair in range(D_KV // LANES):
            k_cols = slice(D_Q + pair * LANES, D_Q + (pair + 1) * LANES)
            v_cols = slice(D_Q + D_KV + pair * LANES, D_Q + D_KV + (pair + 1) * LANES)
            _write_ab(kab_ref, rows, pair, rope(qkv[:, k_cols], cos, sin))
            _write_ab(vab_ref, rows, pair, qkv[:, v_cols])
        kv_raw.append(qkv[:, D_Q:])

    @pl.when(pl.program_id(0) < N_PROMPT_TILES)
    def _():
        kv_t = jnp.concatenate(kv_raw, axis=0).T
        for s in range(TM // SEQ):
            ks_ref[s] = kv_t[:D_KV, s * SEQ:(s + 1) * SEQ]
            vs_ref[s] = kv_t[D_KV:, s * SEQ:(s + 1) * SEQ]


def _rope_index(i):
    return (jnp.maximum(i - N_PROMPT_TILES, 0) % ROPE_TILES, 0)


def _qkv(x, mods, w, cos_t, sin_t):
    seqs_per_tile = TM // SEQ
    state_spec = pl.BlockSpec((seqs_per_tile, D_KV, SEQ),
                              lambda i: (jnp.minimum(i, N_PROMPT_TILES - 1), 0, 0))
    return pl.pallas_call(
        _qkv_kernel,
        grid=(N_TOK // TM,),
        in_specs=[
            _slab_spec(D_MODEL),
            _mod_spec(TM, 0),
            _resident((D_MODEL, D_Q + 2 * D_KV)),
            pl.BlockSpec((TM, LANES), _rope_index),
            pl.BlockSpec((TM, LANES), _rope_index),
        ],
        out_specs=[
            _slab_spec(D_Q), _slab_spec(D_KV_AB), _slab_spec(D_KV_AB), state_spec, state_spec,
        ],
        out_shape=[
            jax.ShapeDtypeStruct((N_TOK, D_Q), BF16),
            jax.ShapeDtypeStruct((N_TOK, D_KV_AB), BF16),
            jax.ShapeDtypeStruct((N_TOK, D_KV_AB), BF16),
            jax.ShapeDtypeStruct((BATCH, D_KV, SEQ), F32),
            jax.ShapeDtypeStruct((BATCH, D_KV, SEQ), F32),
        ],
        compiler_params=_params(),
        name="qkv",
    )(x, mods, w, cos_t, sin_t)


def _attend(q_ref, o_ref, sink_ref, k_slabs, v_slabs, masks, tq, row0=0):
    nt = (((1,), (1,)), ((), ()))
    rows = slice(row0, row0 + tq)
    row_hi = lax.broadcasted_iota(jnp.int32, (2 * tq, 1), 0) >= tq
    lane_lo = lax.broadcasted_iota(jnp.int32, (2 * tq, LANES), 1) < HALF_LANES
    for kh in range(N_KV_HEADS):
        ks = k_slabs(kh)
        vs = v_slabs(kh)
        k_cat = jnp.concatenate([s[:, :LANES] for s in ks] + [s[:, LANES:] for s in ks], axis=0)
        v_cat = jnp.concatenate([s[:, :LANES] for s in vs] + [s[:, LANES:] for s in vs], axis=0)
        n_keys = k_cat.shape[0] // 2
        j0 = 2 * kh
        q2 = jnp.concatenate([q_ref[rows, j0 * LANES:(j0 + 1) * LANES],
                              q_ref[rows, (j0 + 1) * LANES:(j0 + 2) * LANES]], axis=0)
        s = lax.dot_general(q2, k_cat, nt, preferred_element_type=F32)
        es, inv_dens = [], []
        for half in range(2):
            segs, off = [], half * n_keys
            for slab, m in zip(ks, masks):
                seg = s[:, off:off + slab.shape[0]]
                segs.append(seg if m is None else jnp.where(m, seg, NEG_INF))
                off += slab.shape[0]
            logits = jnp.concatenate(segs, axis=1)
            sink = LOG2_E * jnp.where(row_hi, sink_ref[4 * kh + 2 + half], sink_ref[4 * kh + half])
            m_row = jnp.maximum(jnp.max(logits, axis=-1, keepdims=True), sink)
            e = jnp.exp2(logits - m_row)
            den = jnp.sum(e, axis=-1, keepdims=True) + jnp.exp2(sink - m_row)
            es.append(e.astype(BF16))
            inv_dens.append(1.0 / den)
        p = jnp.concatenate(es, axis=1)
        o2 = jnp.dot(p, v_cat, preferred_element_type=F32)
        o2 = o2 * jnp.where(lane_lo, inv_dens[0], inv_dens[1])
        o_ref[rows, j0 * LANES:(j0 + 1) * LANES] = o2[:tq].astype(BF16)
        o_ref[rows, (j0 + 1) * LANES:(j0 + 2) * LANES] = o2[tq:].astype(BF16)


def _kv_cols(kh):
    return slice(kh * 2 * LANES, (kh + 1) * 2 * LANES)


CTX_SUB = 4


def _ctx_attn_kernel(sink_ref, q_ref, k_ref, v_ref, o_ref):
    for sub in range(CTX_SUB):
        rows = slice(sub * SEQ, (sub + 1) * SEQ)
        _attend(q_ref, o_ref, sink_ref,
                lambda kh, rows=rows: [k_ref[rows, _kv_cols(kh)]],
                lambda kh, rows=rows: [v_ref[rows, _kv_cols(kh)]],
                [None], SEQ, row0=sub * SEQ)


def _ctx_attn(sink, q, kab, vab):
    step_rows = CTX_SUB * SEQ
    return pl.pallas_call(
        _ctx_attn_kernel,
        grid=(BATCH // CTX_SUB,),
        in_specs=[
            pl.BlockSpec(memory_space=pltpu.SMEM),
            pl.BlockSpec((step_rows, D_Q), lambda b: (b, 0)),
            pl.BlockSpec((step_rows, D_KV_AB), lambda b: (b, 0)),
            pl.BlockSpec((step_rows, D_KV_AB), lambda b: (b, 0)),
        ],
        out_specs=pl.BlockSpec((step_rows, D_Q), lambda b: (b, 0)),
        out_shape=jax.ShapeDtypeStruct((N_PROMPT_TOK, D_Q), BF16),
        compiler_params=_params(),
        name="ctx_attn",
    )(sink, q, kab, vab)


N_QBLK = DEC_SEQ // BLOCK


LAT_SUB = 4
LAT_STEPS = N_QBLK // LAT_SUB


def _lat_attn_kernel(sink_ref, q_ref, kp_ref, kc_ref, kn_ref, vp_ref, vc_ref, vn_ref,
                     kx_ref, vx_ref, o_ref):
    step = pl.program_id(1)
    r = lax.broadcasted_iota(jnp.int32, (2 * BLOCK, BLOCK), 0) & (BLOCK - 1)
    c = lax.broadcasted_iota(jnp.int32, (2 * BLOCK, BLOCK), 1)

    def blocks(p_ref, c_ref, n_ref, kh):
        cols = _kv_cols(kh)
        return ([p_ref[:, cols]]
                + [c_ref[j * BLOCK:(j + 1) * BLOCK, cols] for j in range(LAT_SUB)]
                + [n_ref[:, cols]])

    for sub in range(LAT_SUB):
        no_prev = (step == 0) if sub == 0 else False
        no_next = (step == LAT_STEPS - 1) if sub == LAT_SUB - 1 else False
        m_prev = c >= r + jnp.where(no_prev, BLOCK, 0)
        m_next = c <= r - jnp.where(no_next, BLOCK, 0)

        def slabs(p_ref, c_ref, n_ref, x_ref, sub=sub):
            return lambda kh: blocks(p_ref, c_ref, n_ref, kh)[sub:sub + 3] + [x_ref[0, :, _kv_cols(kh)]]

        _attend(q_ref, o_ref, sink_ref,
                slabs(kp_ref, kc_ref, kn_ref, kx_ref), slabs(vp_ref, vc_ref, vn_ref, vx_ref),
                [m_prev, None, m_next, None], BLOCK, row0=sub * BLOCK)


def _lat_attn(sink, q, kab, vab, kab_ctx, vab_ctx):
    tq = LAT_SUB * BLOCK
    first = N_PROMPT_TOK // BLOCK

    def cur(b, i):
        return (N_PROMPT_TOK // tq + b * LAT_STEPS + i, 0)

    def prev(b, i):
        return (first + b * N_QBLK + jnp.maximum(i * LAT_SUB - 1, 0), 0)

    def nxt(b, i):
        return (first + b * N_QBLK + jnp.minimum((i + 1) * LAT_SUB, N_QBLK - 1), 0)

    edge_spec = lambda f: pl.BlockSpec((BLOCK, D_KV_AB), f)
    cur_spec = pl.BlockSpec((tq, D_KV_AB), cur)
    ctx_spec = pl.BlockSpec((1, PAST_LEN, D_KV_AB), lambda b, i: (b, 0, 0))
    return pl.pallas_call(
        _lat_attn_kernel,
        grid=(DEC_BATCH, LAT_STEPS),
        in_specs=[
            pl.BlockSpec(memory_space=pltpu.SMEM),
            pl.BlockSpec((tq, D_Q), cur),
            edge_spec(prev), cur_spec, edge_spec(nxt),
            edge_spec(prev), cur_spec, edge_spec(nxt),
            ctx_spec, ctx_spec,
        ],
        out_specs=pl.BlockSpec((tq, D_Q), lambda b, i: (b * LAT_STEPS + i, 0)),
        out_shape=jax.ShapeDtypeStruct((N_LATENT_TOK, D_Q), BF16),
        compiler_params=_params(2),
        name="lat_attn",
    )(sink, q, kab, kab, kab, vab, vab, vab, kab_ctx, vab_ctx)


def _proj_kernel(x_ref, ap_ref, al_ref, mod_ref, lng_ref, lnb_ref, wo_ref, o_ref):
    a = _read_split(ap_ref, al_ref)
    gate = mod_ref[5:6, :]
    for r in range(TM // EPILOGUE_ROWS):
        rows = slice(r * EPILOGUE_ROWS, (r + 1) * EPILOGUE_ROWS)
        f = jnp.dot(a[rows], wo_ref[...], preferred_element_type=F32)
        y = DEEPNORM_ALPHA * x_ref[rows, :] + gate * f
        o_ref[rows, :] = _layer_norm(y, lng_ref[...], lnb_ref[...])


def _proj(x, attn_prompt, attn_latent, mods, ln_g, ln_b, w_o):
    return pl.pallas_call(
        _proj_kernel,
        grid=(N_TOK // TM,),
        in_specs=[
            _slab_spec(D_MODEL),
            _prompt_spec(D_Q),
            _latent_spec(D_Q),
            _mod_spec(TM, 0)] + _ln_specs(0, 1) + [
            _resident((D_Q, D_MODEL)),
        ],
        out_specs=_slab_spec(D_MODEL),
        out_shape=jax.ShapeDtypeStruct((N_TOK, D_MODEL), F32),
        compiler_params=_params(),
        name="attn_proj",
    )(x, attn_prompt, attn_latent, mods, ln_g, ln_b, w_o)


POOL_SUB = 4
POOL_STEP_ROWS = POOL_SUB * POOL_TM
PROMPT_POOL_STEPS = N_PROMPT_TOK // POOL_STEP_ROWS
LATENT_POOL_STEPS = DEC_SEQ // POOL_STEP_ROWS


def _pool_kernel(x_ref, xp_ref, xn_ref, mod_ref, lng_ref, lnb_ref, w_ref, sc_ref, o_ref):
    i = pl.program_id(0)
    is_prompt = i < PROMPT_POOL_STEPS
    in_seq = (i - PROMPT_POOL_STEPS) % LATENT_POOL_STEPS
    h_all = _modulate(x_ref[...], mod_ref, 3)
    h_before = _modulate(xp_ref[...], mod_ref, 3)
    h_after = _modulate(xn_ref[...], mod_ref, 3)
    gate_scale = mod_ref[5:6, :] * sc_ref[...]
    for sub in range(POOL_SUB):
        rows = slice(sub * POOL_TM, (sub + 1) * POOL_TM)
        is_start = is_prompt | (in_seq == 0) if sub == 0 else is_prompt
        is_end = is_prompt | (in_seq == LATENT_POOL_STEPS - 1) if sub == POOL_SUB - 1 else is_prompt
        before = h_before if sub == 0 else h_all[rows.start - POOL_HALO:rows.start]
        after = h_after if sub == POOL_SUB - 1 else h_all[rows.stop:rows.stop + POOL_HALO]
        _pool_sub_tile(x_ref[rows, :], h_all[rows], before, after, is_start, is_end, gate_scale,
                       lng_ref, lnb_ref, w_ref, o_ref, rows)


def _pool_sub_tile(x, h, before, after, is_start, is_end, gate_scale, lng_ref, lnb_ref, w_ref,
                   o_ref, rows):
    h_ext = jnp.concatenate([jnp.where(is_start, 0.0, before), h, jnp.where(is_end, 0.0, after)],
                            axis=0)
    n_ext = POOL_TM + 2 * POOL_HALO

    r8 = lax.broadcasted_iota(jnp.int32, (POOL_HALO, POOL_GROUP_DIM), 0)
    outs = []
    for gi, w in enumerate(POOL_WINDOWS):
        half = w // 2
        cols = slice(gi * POOL_GROUP_DIM, (gi + 1) * POOL_GROUP_DIM)
        acc = h_ext[:, cols]
        s = 1
        while s < w:
            acc = acc + pltpu.roll(acc, s, 0)
            s *= 2
        if half > 1:
            acc = pltpu.roll(acc, n_ext - (half - 1), 0)
        total = acc[POOL_HALO:POOL_HALO + POOL_TM]
        cnt_top = w - jnp.where(is_start, jnp.maximum(half - r8, 0), 0)
        cnt_bot = w - jnp.where(is_end, jnp.maximum(r8 + half - POOL_HALO, 0), 0)
        inv_cnt = jnp.concatenate([
            1.0 / cnt_top.astype(F32),
            jnp.full((POOL_TM - 2 * POOL_HALO, POOL_GROUP_DIM), 1.0 / w, F32),
            1.0 / cnt_bot.astype(F32)], axis=0)
        pooled = (total * inv_cnt - h[:, cols]).astype(BF16)
        outs.append(jnp.dot(pooled, w_ref[gi], preferred_element_type=F32))
    y = DEEPNORM_ALPHA * x + gate_scale * jnp.concatenate(outs, axis=-1)
    o_ref[rows, :] = _layer_norm(y, lng_ref[...], lnb_ref[...])


def _pool(x, mods, ln_g, ln_b, w_pool, scale):
    halo_per_tile = POOL_STEP_ROWS // POOL_HALO
    last_halo = N_TOK // POOL_HALO - 1
    return pl.pallas_call(
        _pool_kernel,
        grid=(N_TOK // POOL_STEP_ROWS,),
        in_specs=[
            pl.BlockSpec((POOL_STEP_ROWS, D_MODEL), lambda i: (i, 0)),
            pl.BlockSpec((POOL_HALO, D_MODEL), lambda i: (jnp.maximum(i * halo_per_tile - 1, 0), 0)),
            pl.BlockSpec((POOL_HALO, D_MODEL),
                         lambda i: (jnp.minimum((i + 1) * halo_per_tile, last_halo), 0)),
            _mod_spec(POOL_STEP_ROWS, 1)] + _ln_specs(1, 1) + [
            _resident((len(POOL_WINDOWS), POOL_GROUP_DIM, POOL_GROUP_DIM)),
            _resident((1, D_MODEL)),
        ],
        out_specs=pl.BlockSpec((POOL_STEP_ROWS, D_MODEL), lambda i: (i, 0)),
        out_shape=jax.ShapeDtypeStruct((N_TOK, D_MODEL), F32),
        compiler_params=_params(),
        name="pool",
    )(x, x, x, mods, ln_g, ln_b, w_pool, scale.reshape(1, D_MODEL))


def _ab_layout(a):
    lead = a.shape[:-1]
    a = a.reshape(lead + (N_KV_HEADS, 1, HEAD_DIM)).astype(BF16)
    z = jnp.zeros_like(a)
    return jnp.concatenate([a, z, z, a], axis=-2).reshape(lead + (D_KV_AB,))


def _rope_tables():
    n_rows = DEC_SEQ // GRID_W
    inv = jnp.power(ROPE_BASE, -jnp.arange(N_FREQ, dtype=F32) / N_FREQ)

    def expand(fn):
        by_row = fn(jnp.arange(n_rows, dtype=F32)[:, None] * inv)
        by_col = fn(jnp.arange(GRID_W, dtype=F32)[:, None] * inv)
        r = jnp.broadcast_to(by_row[:, None, :], (n_rows, GRID_W, N_FREQ))
        c = jnp.broadcast_to(by_col[None, :, :], (n_rows, GRID_W, N_FREQ))
        per_head = jnp.concatenate([r, r, c, c], axis=-1).reshape(DEC_SEQ, HEAD_DIM)
        return jnp.concatenate([per_head] * (LANES // HEAD_DIM), axis=-1)

    cos, sin = expand(jnp.cos), expand(jnp.sin)
    first_half = (jnp.arange(LANES) % 32) < 16
    return cos, jnp.where(first_half, -sin, sin)


def kernel(x_prompt, x_sample, cache_k, cache_v, c, c_ctx, w_mod, b_mod, ln_g, ln_b,
           ffn_w_gate, ffn_w_up, ffn_w_down, attn_w_qkv, attn_w_o, attn_sink,
           pool_w, pool_scale):
    cond = jnp.concatenate(
        [c_ctx[None, :], c, jnp.zeros((COND_ROWS - N_COND, D_MODEL), F32)], axis=0)
    mods = _adaln(cond, w_mod, b_mod)

    ffn_w = (ffn_w_gate, ffn_w_up, ffn_w_down)
    ln_g = ln_g.reshape(DEPTH * 3, 1, D_MODEL)
    ln_b = ln_b.reshape(DEPTH * 3, 1, D_MODEL)
    ffn = functools.partial(_ffn, mods=mods, ln_g=ln_g, ln_b=ln_b)

    x, w_next = ffn((x_prompt.reshape(N_PROMPT_TOK, D_MODEL), x_sample.reshape(N_LATENT_TOK, D_MODEL)),
                    weights=None, layer=0, which=0, next_f32=(ffn_w, (0, 1)))
    cos_t, sin_t = _rope_tables()
    q, kab, vab, k_state, v_state = _qkv(x, mods, attn_w_qkv[0].astype(BF16), cos_t, sin_t)
    sink = attn_sink[0]
    o_ctx = _ctx_attn(sink, q, kab, vab)
    kab_ctx = _ab_layout(cache_k[:, 0].reshape(DEC_BATCH, PAST_LEN, D_KV))
    vab_ctx = _ab_layout(cache_v[:, 0].reshape(DEC_BATCH, PAST_LEN, D_KV))
    o_lat = _lat_attn(sink, q, kab, vab, kab_ctx, vab_ctx)
    x = _proj(x, o_ctx, o_lat, mods, ln_g, ln_b, attn_w_o[0].astype(BF16))
    x, w_next = ffn((x,), weights=w_next, layer=0, which=1, next_f32=(ffn_w, (1, 0)))

    x, w_next = ffn((x,), weights=w_next, layer=1, which=0, next_f32=(ffn_w, (1, 1)),
                    down_rows=256)
    x = _pool(x, mods, ln_g, ln_b, pool_w[0].astype(BF16), pool_scale[0])
    (y_prompt, y_sample), _ = ffn((x,), weights=w_next, layer=1, which=1, split_out=True)

    def state(s_t):
        s_t = s_t.reshape(BATCH, N_KV_HEADS, HEAD_DIM, SEQ)
        return jnp.transpose(s_t, (0, 3, 1, 2))[:, None]

    return (y_prompt.reshape(BATCH, SEQ, D_MODEL),
            y_sample.reshape(DEC_BATCH, DEC_SEQ, D_MODEL),
            state(k_state), state(v_state))
```

```python
import functools

import jax
import jax.numpy as jnp
from jax import lax
from jax.experimental import pallas as pl
from jax.experimental.pallas import tpu as pltpu

D_MODEL = 1024
BATCH = 16
SEQ = 256
DEPTH = 2
DEC_BATCH = 2
DEC_SEQ = 4096
PAST_LEN = 256
GRID_W = 64
N_HEADS = 16
N_KV_HEADS = 4
HEAD_DIM = 64
WINDOW = 128
BLOCK = 128
ROPE_BASE = 10000.0
N_FREQ = HEAD_DIM // 4
POOL_WINDOWS = (2, 4, 8, 16)
POOL_GROUP_DIM = D_MODEL // 4
D_FF = 2816
N_MOD = 9
LN_EPS = 1e-5
DEEPNORM_ALPHA = (2.0 * DEPTH) ** 0.25
ATTN_SCALE = HEAD_DIM ** -0.5
LOG2_E = 1.4426950408889634
Q_SCALE = ATTN_SCALE * LOG2_E
NEG_INF = -1e30

N_PROMPT_TOK = BATCH * SEQ
N_LATENT_TOK = DEC_BATCH * DEC_SEQ
N_TOK = N_PROMPT_TOK + N_LATENT_TOK
MOD_GROUP_ROWS = 4096
N_COND = 1 + DEC_BATCH
COND_ROWS = 8

LANES = 128
HALF_LANES = LANES // 2
TM = 1024
N_PROMPT_TILES = N_PROMPT_TOK // TM
FF_CHUNK = 256
EPILOGUE_ROWS = 256
POOL_TM = 256
POOL_HALO = 8
D_Q = N_HEADS * HEAD_DIM
D_KV = N_KV_HEADS * HEAD_DIM
D_KV_AB = N_KV_HEADS * 2 * LANES
VMEM_LIMIT = 40 * 1024 * 1024
FFN_VMEM_LIMIT = 55 * 1024 * 1024

F32 = jnp.float32
BF16 = jnp.bfloat16


def _layer_norm(y, g, b):
    mu = jnp.mean(y, axis=-1, keepdims=True)
    yc = y - mu
    var = jnp.mean(yc * yc, axis=-1, keepdims=True)
    return yc * lax.rsqrt(var + LN_EPS) * g + b


def _modulate(x, mod_ref, row0):
    shift = mod_ref[row0:row0 + 1, :]
    scale = mod_ref[row0 + 1:row0 + 2, :]
    return x * (1.0 + scale) + shift


def _params(n_axes=1, vmem_limit=VMEM_LIMIT):
    return pltpu.CompilerParams(
        dimension_semantics=("arbitrary",) * n_axes,
        vmem_limit_bytes=vmem_limit)


def _resident(shape):
    nd = len(shape)
    return pl.BlockSpec(shape, lambda *_: (0,) * nd, pipeline_mode=pl.Buffered(1))


def _resident_at(index, tail):
    return pl.BlockSpec((None,) * len(index) + tuple(tail),
                        lambda *_: tuple(index) + (0,) * len(tail), pipeline_mode=pl.Buffered(1))


def _mod_spec(tile_rows, layer):
    tiles_per_group = MOD_GROUP_ROWS // tile_rows
    return pl.BlockSpec((None, None, N_MOD, D_MODEL),
                        lambda i: (layer, i // tiles_per_group, 0, 0))


def _ln_specs(layer, which):
    return [_resident_at((layer * 3 + which,), (1, D_MODEL))] * 2


def _slab_spec(cols, tm=TM):
    return pl.BlockSpec((tm, cols), lambda i: (i, 0))


def _prompt_spec(cols, tm=TM):
    return pl.BlockSpec((tm, cols), lambda i: (jnp.minimum(i, N_PROMPT_TOK // tm - 1), 0))


def _latent_spec(cols, tm=TM):
    return pl.BlockSpec((tm, cols), lambda i: (jnp.maximum(i - N_PROMPT_TOK // tm, 0), 0))


def _read_split(p_ref, l_ref, n_prompt_tiles=N_PROMPT_TILES):
    return jnp.where(pl.program_id(0) < n_prompt_tiles, p_ref[...], l_ref[...])


ADALN_NC = 2304


def _adaln_kernel(cond_ref, w_ref, b_ref, o_ref):
    c = cond_ref[...]
    s = (c * (1.0 / (1.0 + jnp.exp(-c)))).astype(BF16)
    o_ref[0] = jnp.dot(s, w_ref[0].astype(BF16), preferred_element_type=F32) + b_ref[0]


def _adaln(cond, w_mod, b_mod):
    n_out = N_MOD * D_MODEL
    out = pl.pallas_call(
        _adaln_kernel,
        grid=(DEPTH, n_out // ADALN_NC),
        in_specs=[
            pl.BlockSpec((COND_ROWS, D_MODEL), lambda l, n: (0, 0)),
            pl.BlockSpec((1, D_MODEL, ADALN_NC), lambda l, n: (l, 0, n)),
            pl.BlockSpec((1, 1, ADALN_NC), lambda l, n: (l, 0, n)),
        ],
        out_specs=pl.BlockSpec((1, COND_ROWS, ADALN_NC), lambda l, n: (l, 0, n)),
        out_shape=jax.ShapeDtypeStruct((DEPTH, COND_ROWS, n_out), F32),
        compiler_params=_params(2),
        name="adaln",
    )(cond, w_mod, b_mod.reshape(DEPTH, 1, n_out))
    return out[:, :N_COND].reshape(DEPTH, N_COND, N_MOD, D_MODEL)


def _ffn_gate_up(x, mod_ref, wg_ref, wu_ref, a_ref, row0):
    h = _modulate(x, mod_ref, row0).astype(BF16)
    for c0 in range(0, D_FF, FF_CHUNK):
        sl = slice(c0, min(c0 + FF_CHUNK, D_FF))
        g = jnp.dot(h, wg_ref[:, sl], preferred_element_type=F32)
        u = jnp.dot(h, wu_ref[:, sl], preferred_element_type=F32)
        a_ref[:, sl] = (g * (1.0 / (1.0 + jnp.exp(-g))) * u).astype(BF16)


def _ffn_epilogue(x, f, mod_ref, lng_ref, lnb_ref, row0):
    gate = mod_ref[row0 + 2:row0 + 3, :]
    y = DEEPNORM_ALPHA * x + (0.5 * gate) * f
    return _layer_norm(y, lng_ref[...], lnb_ref[...])


CAST_STEPS = 8
OWN_CHUNKS = 16
STAGE_SLOTS = 4
FFN_WEIGHT_SHAPES = ((D_MODEL, D_FF), (D_MODEL, D_FF), (D_FF, D_MODEL))


def _weight_chunk_copy(src_ref, lead, stage_ref, sem_ref, chunk, slot):
    rows = stage_ref.shape[1]
    src = src_ref.at[lead[0], lead[1], pl.ds(chunk * rows, rows), :]
    return pltpu.make_async_copy(src, stage_ref.at[slot], sem_ref.at[slot])


def _load_own_weights(lead, srcs, dsts, stages, sems):
    jobs, used = [], {}
    for src, dst in zip(srcs, dsts):
        key = dst.shape[1]
        for chunk in range(OWN_CHUNKS):
            slot = used.get(key, 0) % STAGE_SLOTS
            used[key] = used.get(key, 0) + 1
            jobs.append((_weight_chunk_copy(src, lead, stages[key], sems[key], chunk, slot),
                         stages[key], slot, dst, chunk))
    ahead = STAGE_SLOTS - 1
    for job in jobs[:ahead]:
        job[0].start()
    for n, (copy, stage, slot, dst, chunk) in enumerate(jobs):
        if n + ahead < len(jobs):
            jobs[n + ahead][0].start()
        copy.wait()
        rows = stage.shape[1]
        dst[chunk * rows:(chunk + 1) * rows, :] = stage[slot].astype(BF16)


def _ffn_kernel(*refs, row0, n_prompt_tiles, split_in, split_out, cast_next, own_lead):
    n_x = 2 if split_in else 1
    n_cast = 3 if cast_next else 0
    n_own = 7 if own_lead is not None else 0
    own = refs[len(refs) - n_own:]
    refs = refs[:len(refs) - n_own]
    x_refs, (mod_ref, lng_ref, lnb_ref, wg_ref, wu_ref, wd_ref) = refs[:n_x], refs[n_x:n_x + 6]
    next_f32 = refs[n_x + 6:n_x + 6 + n_cast]
    o_refs, next_bf16, a_ref = refs[n_x + 6 + n_cast:-1 - n_cast], refs[-1 - n_cast:-1], refs[-1]
    step = pl.program_id(0)
    if own_lead is not None:
        own_w, stage_a, stage_b, sem_a, sem_b = own[:3], own[3], own[4], own[5], own[6]
        stages = {stage_a.shape[2]: stage_a, stage_b.shape[2]: stage_b}
        sems = {stage_a.shape[2]: sem_a, stage_b.shape[2]: sem_b}

        @pl.when(step == 0)
        def _():
            _load_own_weights(own_lead, (wg_ref, wu_ref, wd_ref), own_w, stages, sems)

        wg_ref, wu_ref, wd_ref = own_w
    x = _read_split(*x_refs, n_prompt_tiles) if split_in else x_refs[0][...]
    _ffn_gate_up(x, mod_ref, wg_ref, wu_ref, a_ref, row0)
    chunks = []
    for r0 in range(0, x.shape[0], EPILOGUE_ROWS):
        rows = slice(r0, r0 + EPILOGUE_ROWS)
        f = jnp.dot(a_ref[rows, :], wd_ref[...], preferred_element_type=F32)
        chunk = _ffn_epilogue(x[rows], f, mod_ref, lng_ref, lnb_ref, row0)
        if split_out:
            chunks.append(chunk)
        else:
            o_refs[0][rows, :] = chunk
    if split_out:
        out = jnp.concatenate(chunks, axis=0)

        @pl.when(step < n_prompt_tiles)
        def _():
            o_refs[0][...] = out

        @pl.when(step >= n_prompt_tiles)
        def _():
            o_refs[1][...] = out
    if cast_next:
        @pl.when(step < CAST_STEPS)
        def _():
            for src, dst in zip(next_f32, next_bf16):
                dst[...] = src[...].astype(BF16)


def _cast_block_spec(shape, lead):
    block = (None,) * len(lead) + (shape[0] // CAST_STEPS, shape[1])
    return pl.BlockSpec(block, lambda i: tuple(lead) + (jnp.minimum(i, CAST_STEPS - 1), 0))


def _ffn(xs, mods, ln_g, ln_b, weights, layer, which, next_f32=None, split_out=False):
    split_in = len(xs) == 2
    cast_next = next_f32 is not None
    own_lead = (layer, which) if weights is None else None
    row0 = 6 * which
    ln_row = 2 * which
    tm = TM // 2 if split_in and cast_next else TM
    x_specs = ([_prompt_spec(D_MODEL, tm), _latent_spec(D_MODEL, tm)] if split_in
               else [_slab_spec(D_MODEL, tm)])
    if split_out:
        out_specs = [_prompt_spec(D_MODEL, tm), _latent_spec(D_MODEL, tm)]
        out_shape = [jax.ShapeDtypeStruct((N_PROMPT_TOK, D_MODEL), F32),
                     jax.ShapeDtypeStruct((N_LATENT_TOK, D_MODEL), F32)]
    else:
        out_specs = [_slab_spec(D_MODEL, tm)]
        out_shape = [jax.ShapeDtypeStruct((N_TOK, D_MODEL), F32)]
    n_x_out = len(out_specs)
    cast_in_specs, cast_args = [], ()
    if cast_next:
        cast_args, lead = next_f32
        cast_in_specs = [_cast_block_spec(s, lead) for s in FFN_WEIGHT_SHAPES]
        out_specs = out_specs + [_cast_block_spec(s, ()) for s in FFN_WEIGHT_SHAPES]
        out_shape = out_shape + [jax.ShapeDtypeStruct(s, BF16) for s in FFN_WEIGHT_SHAPES]
    scratch = [pltpu.VMEM((tm, D_FF), BF16)]
    if own_lead is None:
        weight_specs = [_resident(s) for s in FFN_WEIGHT_SHAPES]
    else:
        weights = next_f32[0]
        weight_specs = [pl.BlockSpec(memory_space=pl.ANY)] * 3
        stage_shapes = sorted({(STAGE_SLOTS, s[0] // OWN_CHUNKS, s[1]) for s in FFN_WEIGHT_SHAPES})
        scratch += [pltpu.VMEM(s, BF16) for s in FFN_WEIGHT_SHAPES]
        scratch += [pltpu.VMEM(s, F32) for s in stage_shapes]
        scratch += [pltpu.SemaphoreType.DMA((STAGE_SLOTS,)) for _ in stage_shapes]
    outs = pl.pallas_call(
        functools.partial(_ffn_kernel, row0=row0, n_prompt_tiles=N_PROMPT_TOK // tm,
                          split_in=split_in, split_out=split_out, cast_next=cast_next,
                          own_lead=own_lead),
        grid=(N_TOK // tm,),
        in_specs=x_specs + [_mod_spec(tm, layer)] + _ln_specs(layer, ln_row)
        + weight_specs + cast_in_specs,
        out_specs=out_specs,
        out_shape=out_shape,
        scratch_shapes=scratch,
        compiler_params=_params(vmem_limit=FFN_VMEM_LIMIT),
        name="ffn",
    )(*xs, mods, ln_g, ln_b, *weights, *cast_args)
    x_out = tuple(outs[:n_x_out]) if split_out else outs[0]
    return x_out, (tuple(outs[n_x_out:]) if cast_next else None)


ROPE_TILES = DEC_SEQ // TM


def _write_ab(dst_ref, rows, pair, src):
    lo = lax.broadcasted_iota(jnp.int32, src.shape, 1) < HALF_LANES
    swapped = pltpu.roll(src, HALF_LANES, 1)
    blocks = (jnp.where(lo, src, 0.0), jnp.where(lo, 0.0, swapped),
              jnp.where(lo, swapped, 0.0), jnp.where(lo, 0.0, src))
    for n, blk in enumerate(blocks):
        c0 = (4 * pair + n) * LANES
        dst_ref[rows, c0:c0 + LANES] = blk.astype(BF16)


def _qkv_kernel(x_ref, mod_ref, w_ref, cos_ref, sin_ref, q_ref, kab_ref, vab_ref, ks_ref, vs_ref):
    kv_raw = []
    is_prompt = pl.program_id(0) < N_PROMPT_TILES
    first_half = (lax.broadcasted_iota(jnp.int32, (EPILOGUE_ROWS, LANES), 1) & 16) == 0
    for r in range(TM // EPILOGUE_ROWS):
        rows = slice(r * EPILOGUE_ROWS, (r + 1) * EPILOGUE_ROWS)
        h = _modulate(x_ref[rows, :], mod_ref, 3).astype(BF16)
        qkv = jnp.dot(h, w_ref[...], preferred_element_type=F32)
        cos = jnp.where(is_prompt, 1.0, cos_ref[rows, :])
        sin = jnp.where(is_prompt, 0.0, sin_ref[rows, :])

        def rope(blk, c, s):
            up = pltpu.roll(blk, LANES - 16, 1)
            dn = pltpu.roll(blk, 16, 1)
            return blk * c + jnp.where(first_half, up, dn) * s

        cos_q, sin_q = cos * Q_SCALE, sin * Q_SCALE
        for j in range(D_Q // LANES):
            cols = slice(j * LANES, (j + 1) * LANES)
            q_ref[rows, cols] = rope(qkv[:, cols], cos_q, sin_q).astype(BF16)
        for pair in range(D_KV // LANES):
            k_cols = slice(D_Q + pair * LANES, D_Q + (pair + 1) * LANES)
            v_cols = slice(D_Q + D_KV + pair * LANES, D_Q + D_KV + (pair + 1) * LANES)
            _write_ab(kab_ref, rows, pair, rope(qkv[:, k_cols], cos, sin))
            _write_ab(vab_ref, rows, pair, qkv[:, v_cols])
        kv_raw.append(qkv[:, D_Q:])

    @pl.when(pl.program_id(0) < N_PROMPT_TILES)
    def _():
        kv_t = jnp.concatenate(kv_raw, axis=0).T
        for s in range(TM // SEQ):
            ks_ref[s] = kv_t[:D_KV, s * SEQ:(s + 1) * SEQ]
            vs_ref[s] = kv_t[D_KV:, s * SEQ:(s + 1) * SEQ]


def _rope_index(i):
    return (jnp.maximum(i - N_PROMPT_TILES, 0) % ROPE_TILES, 0)


def _qkv(x, mods, w, cos_t, sin_t):
    seqs_per_tile = TM // SEQ
    state_spec = pl.BlockSpec((seqs_per_tile, D_KV, SEQ),
                              lambda i: (jnp.minimum(i, N_PROMPT_TILES - 1), 0, 0))
    return pl.pallas_call(
        _qkv_kernel,
        grid=(N_TOK // TM,),
        in_specs=[
            _slab_spec(D_MODEL),
            _mod_spec(TM, 0),
            _resident((D_MODEL, D_Q + 2 * D_KV)),
            pl.BlockSpec((TM, LANES), _rope_index),
            pl.BlockSpec((TM, LANES), _rope_index),
        ],
        out_specs=[
            _slab_spec(D_Q), _slab_spec(D_KV_AB), _slab_spec(D_KV_AB), state_spec, state_spec,
        ],
        out_shape=[
            jax.ShapeDtypeStruct((N_TOK, D_Q), BF16),
            jax.ShapeDtypeStruct((N_TOK, D_KV_AB), BF16),
            jax.ShapeDtypeStruct((N_TOK, D_KV_AB), BF16),
            jax.ShapeDtypeStruct((BATCH, D_KV, SEQ), F32),
            jax.ShapeDtypeStruct((BATCH, D_KV, SEQ), F32),
        ],
        compiler_params=_params(),
        name="qkv",
    )(x, mods, w, cos_t, sin_t)


def _attend(q_ref, o_ref, sink_ref, k_slabs, v_slabs, masks, tq, row0=0):
    nt = (((1,), (1,)), ((), ()))
    rows = slice(row0, row0 + tq)
    row_hi = lax.broadcasted_iota(jnp.int32, (2 * tq, 1), 0) >= tq
    lane_lo = lax.broadcasted_iota(jnp.int32, (2 * tq, LANES), 1) < HALF_LANES
    for kh in range(N_KV_HEADS):
        ks = k_slabs(kh)
        vs = v_slabs(kh)
        k_cat = jnp.concatenate([s[:, :LANES] for s in ks] + [s[:, LANES:] for s in ks], axis=0)
        v_cat = jnp.concatenate([s[:, :LANES] for s in vs] + [s[:, LANES:] for s in vs], axis=0)
        n_keys = k_cat.shape[0] // 2
        j0 = 2 * kh
        q2 = jnp.concatenate([q_ref[rows, j0 * LANES:(j0 + 1) * LANES],
                              q_ref[rows, (j0 + 1) * LANES:(j0 + 2) * LANES]], axis=0)
        s = lax.dot_general(q2, k_cat, nt, preferred_element_type=F32)
        es, inv_dens = [], []
        for half in range(2):
            segs, off = [], half * n_keys
            for slab, m in zip(ks, masks):
                seg = s[:, off:off + slab.shape[0]]
                segs.append(seg if m is None else jnp.where(m, seg, NEG_INF))
                off += slab.shape[0]
            logits = jnp.concatenate(segs, axis=1)
            sink = LOG2_E * jnp.where(row_hi, sink_ref[4 * kh + 2 + half], sink_ref[4 * kh + half])
            m_row = jnp.maximum(jnp.max(logits, axis=-1, keepdims=True), sink)
            e = jnp.exp2(logits - m_row)
            den = jnp.sum(e, axis=-1, keepdims=True) + jnp.exp2(sink - m_row)
            es.append(e.astype(BF16))
            inv_dens.append(1.0 / den)
        p = jnp.concatenate(es, axis=1)
        o2 = jnp.dot(p, v_cat, preferred_element_type=F32)
        o2 = o2 * jnp.where(lane_lo, inv_dens[0], inv_dens[1])
        o_ref[rows, j0 * LANES:(j0 + 1) * LANES] = o2[:tq].astype(BF16)
        o_ref[rows, (j0 + 1) * LANES:(j0 + 2) * LANES] = o2[tq:].astype(BF16)


def _kv_cols(kh):
    return slice(kh * 2 * LANES, (kh + 1) * 2 * LANES)


CTX_SUB = 4


def _ctx_attn_kernel(sink_ref, q_ref, k_ref, v_ref, o_ref):
    for sub in range(CTX_SUB):
        rows = slice(sub * SEQ, (sub + 1) * SEQ)
        _attend(q_ref, o_ref, sink_ref,
                lambda kh, rows=rows: [k_ref[rows, _kv_cols(kh)]],
                lambda kh, rows=rows: [v_ref[rows, _kv_cols(kh)]],
                [None], SEQ, row0=sub * SEQ)


def _ctx_attn(sink, q, kab, vab):
    step_rows = CTX_SUB * SEQ
    return pl.pallas_call(
        _ctx_attn_kernel,
        grid=(BATCH // CTX_SUB,),
        in_specs=[
            pl.BlockSpec(memory_space=pltpu.SMEM),
            pl.BlockSpec((step_rows, D_Q), lambda b: (b, 0)),
            pl.BlockSpec((step_rows, D_KV_AB), lambda b: (b, 0)),
            pl.BlockSpec((step_rows, D_KV_AB), lambda b: (b, 0)),
        ],
        out_specs=pl.BlockSpec((step_rows, D_Q), lambda b: (b, 0)),
        out_shape=jax.ShapeDtypeStruct((N_PROMPT_TOK, D_Q), BF16),
        compiler_params=_params(),
        name="ctx_attn",
    )(sink, q, kab, vab)


N_QBLK = DEC_SEQ // BLOCK


LAT_SUB = 4
LAT_STEPS = N_QBLK // LAT_SUB


def _lat_attn_kernel(sink_ref, q_ref, kp_ref, kc_ref, kn_ref, vp_ref, vc_ref, vn_ref,
                     kx_ref, vx_ref, o_ref):
    step = pl.program_id(1)
    r = lax.broadcasted_iota(jnp.int32, (2 * BLOCK, BLOCK), 0) & (BLOCK - 1)
    c = lax.broadcasted_iota(jnp.int32, (2 * BLOCK, BLOCK), 1)

    def blocks(p_ref, c_ref, n_ref, kh):
        cols = _kv_cols(kh)
        return ([p_ref[:, cols]]
                + [c_ref[j * BLOCK:(j + 1) * BLOCK, cols] for j in range(LAT_SUB)]
                + [n_ref[:, cols]])

    for sub in range(LAT_SUB):
        no_prev = (step == 0) if sub == 0 else False
        no_next = (step == LAT_STEPS - 1) if sub == LAT_SUB - 1 else False
        m_prev = c >= r + jnp.where(no_prev, BLOCK, 0)
        m_next = c <= r - jnp.where(no_next, BLOCK, 0)

        def slabs(p_ref, c_ref, n_ref, x_ref, sub=sub):
            return lambda kh: blocks(p_ref, c_ref, n_ref, kh)[sub:sub + 3] + [x_ref[0, :, _kv_cols(kh)]]

        _attend(q_ref, o_ref, sink_ref,
                slabs(kp_ref, kc_ref, kn_ref, kx_ref), slabs(vp_ref, vc_ref, vn_ref, vx_ref),
                [m_prev, None, m_next, None], BLOCK, row0=sub * BLOCK)


def _lat_attn(sink, q, kab, vab, kab_ctx, vab_ctx):
    tq = LAT_SUB * BLOCK
    first = N_PROMPT_TOK // BLOCK

    def cur(b, i):
        return (N_PROMPT_TOK // tq + b * LAT_STEPS + i, 0)

    def prev(b, i):
        return (first + b * N_QBLK + jnp.maximum(i * LAT_SUB - 1, 0), 0)

    def nxt(b, i):
        return (first + b * N_QBLK + jnp.minimum((i + 1) * LAT_SUB, N_QBLK - 1), 0)

    edge_spec = lambda f: pl.BlockSpec((BLOCK, D_KV_AB), f)
    cur_spec = pl.BlockSpec((tq, D_KV_AB), cur)
    ctx_spec = pl.BlockSpec((1, PAST_LEN, D_KV_AB), lambda b, i: (b, 0, 0))
    return pl.pallas_call(
        _lat_attn_kernel,
        grid=(DEC_BATCH, LAT_STEPS),
        in_specs=[
            pl.BlockSpec(memory_space=pltpu.SMEM),
            pl.BlockSpec((tq, D_Q), cur),
            edge_spec(prev), cur_spec, edge_spec(nxt),
            edge_spec(prev), cur_spec, edge_spec(nxt),
            ctx_spec, ctx_spec,
        ],
        out_specs=pl.BlockSpec((tq, D_Q), lambda b, i: (b * LAT_STEPS + i, 0)),
        out_shape=jax.ShapeDtypeStruct((N_LATENT_TOK, D_Q), BF16),
        compiler_params=_params(2),
        name="lat_attn",
    )(sink, q, kab, kab, kab, vab, vab, vab, kab_ctx, vab_ctx)


def _proj_kernel(x_ref, ap_ref, al_ref, mod_ref, lng_ref, lnb_ref, wo_ref, o_ref):
    a = _read_split(ap_ref, al_ref)
    gate = mod_ref[5:6, :]
    for r in range(TM // EPILOGUE_ROWS):
        rows = slice(r * EPILOGUE_ROWS, (r + 1) * EPILOGUE_ROWS)
        f = jnp.dot(a[rows], wo_ref[...], preferred_element_type=F32)
        y = DEEPNORM_ALPHA * x_ref[rows, :] + gate * f
        o_ref[rows, :] = _layer_norm(y, lng_ref[...], lnb_ref[...])


def _proj(x, attn_prompt, attn_latent, mods, ln_g, ln_b, w_o):
    return pl.pallas_call(
        _proj_kernel,
        grid=(N_TOK // TM,),
        in_specs=[
            _slab_spec(D_MODEL),
            _prompt_spec(D_Q),
            _latent_spec(D_Q),
            _mod_spec(TM, 0)] + _ln_specs(0, 1) + [
            _resident((D_Q, D_MODEL)),
        ],
        out_specs=_slab_spec(D_MODEL),
        out_shape=jax.ShapeDtypeStruct((N_TOK, D_MODEL), F32),
        compiler_params=_params(),
        name="attn_proj",
    )(x, attn_prompt, attn_latent, mods, ln_g, ln_b, w_o)


POOL_SUB = 4
POOL_STEP_ROWS = POOL_SUB * POOL_TM
PROMPT_POOL_STEPS = N_PROMPT_TOK // POOL_STEP_ROWS
LATENT_POOL_STEPS = DEC_SEQ // POOL_STEP_ROWS


def _pool_kernel(x_ref, xp_ref, xn_ref, mod_ref, lng_ref, lnb_ref, w_ref, sc_ref, o_ref):
    i = pl.program_id(0)
    is_prompt = i < PROMPT_POOL_STEPS
    in_seq = (i - PROMPT_POOL_STEPS) % LATENT_POOL_STEPS
    h_all = _modulate(x_ref[...], mod_ref, 3)
    h_before = _modulate(xp_ref[...], mod_ref, 3)
    h_after = _modulate(xn_ref[...], mod_ref, 3)
    gate_scale = mod_ref[5:6, :] * sc_ref[...]
    for sub in range(POOL_SUB):
        rows = slice(sub * POOL_TM, (sub + 1) * POOL_TM)
        is_start = is_prompt | (in_seq == 0) if sub == 0 else is_prompt
        is_end = is_prompt | (in_seq == LATENT_POOL_STEPS - 1) if sub == POOL_SUB - 1 else is_prompt
        before = h_before if sub == 0 else h_all[rows.start - POOL_HALO:rows.start]
        after = h_after if sub == POOL_SUB - 1 else h_all[rows.stop:rows.stop + POOL_HALO]
        _pool_sub_tile(x_ref[rows, :], h_all[rows], before, after, is_start, is_end, gate_scale,
                       lng_ref, lnb_ref, w_ref, o_ref, rows)


def _pool_sub_tile(x, h, before, after, is_start, is_end, gate_scale, lng_ref, lnb_ref, w_ref,
                   o_ref, rows):
    h_ext = jnp.concatenate([jnp.where(is_start, 0.0, before), h, jnp.where(is_end, 0.0, after)],
                            axis=0)
    n_ext = POOL_TM + 2 * POOL_HALO

    r8 = lax.broadcasted_iota(jnp.int32, (POOL_HALO, POOL_GROUP_DIM), 0)
    outs = []
    for gi, w in enumerate(POOL_WINDOWS):
        half = w // 2
        cols = slice(gi * POOL_GROUP_DIM, (gi + 1) * POOL_GROUP_DIM)
        acc = h_ext[:, cols]
        s = 1
        while s < w:
            acc = acc + pltpu.roll(acc, s, 0)
            s *= 2
        if half > 1:
            acc = pltpu.roll(acc, n_ext - (half - 1), 0)
        total = acc[POOL_HALO:POOL_HALO + POOL_TM]
        cnt_top = w - jnp.where(is_start, jnp.maximum(half - r8, 0), 0)
        cnt_bot = w - jnp.where(is_end, jnp.maximum(r8 + half - POOL_HALO, 0), 0)
        inv_cnt = jnp.concatenate([
            1.0 / cnt_top.astype(F32),
            jnp.full((POOL_TM - 2 * POOL_HALO, POOL_GROUP_DIM), 1.0 / w, F32),
            1.0 / cnt_bot.astype(F32)], axis=0)
        pooled = (total * inv_cnt - h[:, cols]).astype(BF16)
        outs.append(jnp.dot(pooled, w_ref[gi], preferred_element_type=F32))
    y = DEEPNORM_ALPHA * x + gate_scale * jnp.concatenate(outs, axis=-1)
    o_ref[rows, :] = _layer_norm(y, lng_ref[...], lnb_ref[...])


def _pool(x, mods, ln_g, ln_b, w_pool, scale):
    halo_per_tile = POOL_STEP_ROWS // POOL_HALO
    last_halo = N_TOK // POOL_HALO - 1
    return pl.pallas_call(
        _pool_kernel,
        grid=(N_TOK // POOL_STEP_ROWS,),
        in_specs=[
            pl.BlockSpec((POOL_STEP_ROWS, D_MODEL), lambda i: (i, 0)),
            pl.BlockSpec((POOL_HALO, D_MODEL), lambda i: (jnp.maximum(i * halo_per_tile - 1, 0), 0)),
            pl.BlockSpec((POOL_HALO, D_MODEL),
                         lambda i: (jnp.minimum((i + 1) * halo_per_tile, last_halo), 0)),
            _mod_spec(POOL_STEP_ROWS, 1)] + _ln_specs(1, 1) + [
            _resident((len(POOL_WINDOWS), POOL_GROUP_DIM, POOL_GROUP_DIM)),
            _resident((1, D_MODEL)),
        ],
        out_specs=pl.BlockSpec((POOL_STEP_ROWS, D_MODEL), lambda i: (i, 0)),
        out_shape=jax.ShapeDtypeStruct((N_TOK, D_MODEL), F32),
        compiler_params=_params(),
        name="pool",
    )(x, x, x, mods, ln_g, ln_b, w_pool, scale.reshape(1, D_MODEL))


def _ab_layout(a):
    lead = a.shape[:-1]
    a = a.reshape(lead + (N_KV_HEADS, 1, HEAD_DIM)).astype(BF16)
    z = jnp.zeros_like(a)
    return jnp.concatenate([a, z, z, a], axis=-2).reshape(lead + (D_KV_AB,))


def _rope_tables():
    n_rows = DEC_SEQ // GRID_W
    inv = jnp.power(ROPE_BASE, -jnp.arange(N_FREQ, dtype=F32) / N_FREQ)

    def expand(fn):
        by_row = fn(jnp.arange(n_rows, dtype=F32)[:, None] * inv)
        by_col = fn(jnp.arange(GRID_W, dtype=F32)[:, None] * inv)
        r = jnp.broadcast_to(by_row[:, None, :], (n_rows, GRID_W, N_FREQ))
        c = jnp.broadcast_to(by_col[None, :, :], (n_rows, GRID_W, N_FREQ))
        per_head = jnp.concatenate([r, r, c, c], axis=-1).reshape(DEC_SEQ, HEAD_DIM)
        return jnp.concatenate([per_head] * (LANES // HEAD_DIM), axis=-1)

    cos, sin = expand(jnp.cos), expand(jnp.sin)
    first_half = (jnp.arange(LANES) % 32) < 16
    return cos, jnp.where(first_half, -sin, sin)


def kernel(x_prompt, x_sample, cache_k, cache_v, c, c_ctx, w_mod, b_mod, ln_g, ln_b,
           ffn_w_gate, ffn_w_up, ffn_w_down, attn_w_qkv, attn_w_o, attn_sink,
           pool_w, pool_scale):
    cond = jnp.concatenate(
        [c_ctx[None, :], c, jnp.zeros((COND_ROWS - N_COND, D_MODEL), F32)], axis=0)
    mods = _adaln(cond, w_mod, b_mod)

    ffn_w = (ffn_w_gate, ffn_w_up, ffn_w_down)
    ln_g = ln_g.reshape(DEPTH * 3, 1, D_MODEL)
    ln_b = ln_b.reshape(DEPTH * 3, 1, D_MODEL)
    ffn = functools.partial(_ffn, mods=mods, ln_g=ln_g, ln_b=ln_b)

    x, w_next = ffn((x_prompt.reshape(N_PROMPT_TOK, D_MODEL), x_sample.reshape(N_LATENT_TOK, D_MODEL)),
                    weights=None, layer=0, which=0, next_f32=(ffn_w, (0, 1)))
    cos_t, sin_t = _rope_tables()
    q, kab, vab, k_state, v_state = _qkv(x, mods, attn_w_qkv[0].astype(BF16), cos_t, sin_t)
    sink = attn_sink[0]
    o_ctx = _ctx_attn(sink, q, kab, vab)
    kab_ctx = _ab_layout(cache_k[:, 0].reshape(DEC_BATCH, PAST_LEN, D_KV))
    vab_ctx = _ab_layout(cache_v[:, 0].reshape(DEC_BATCH, PAST_LEN, D_KV))
    o_lat = _lat_attn(sink, q, kab, vab, kab_ctx, vab_ctx)
    x = _proj(x, o_ctx, o_lat, mods, ln_g, ln_b, attn_w_o[0].astype(BF16))
    x, w_next = ffn((x,), weights=w_next, layer=0, which=1, next_f32=(ffn_w, (1, 0)))

    x, w_next = ffn((x,), weights=w_next, layer=1, which=0, next_f32=(ffn_w, (1, 1)))
    x = _pool(x, mods, ln_g, ln_b, pool_w[0].astype(BF16), pool_scale[0])
    (y_prompt, y_sample), _ = ffn((x,), weights=w_next, layer=1, which=1, split_out=True)

    def state(s_t):
        s_t = s_t.reshape(BATCH, N_KV_HEADS, HEAD_DIM, SEQ)
        return jnp.transpose(s_t, (0, 3, 1, 2))[:, None]

    return (y_prompt.reshape(BATCH, SEQ, D_MODEL),
            y_sample.reshape(DEC_BATCH, DEC_SEQ, D_MODEL),
            state(k_state), state(v_state))
```

```python
import functools

import jax
import jax.numpy as jnp
from jax import lax
from jax.experimental import pallas as pl
from jax.experimental.pallas import tpu as pltpu

D_MODEL = 1024
BATCH = 16
SEQ = 256
DEPTH = 2
DEC_BATCH = 2
DEC_SEQ = 4096
PAST_LEN = 256
GRID_W = 64
N_HEADS = 16
N_KV_HEADS = 4
HEAD_DIM = 64
WINDOW = 128
BLOCK = 128
ROPE_BASE = 10000.0
N_FREQ = HEAD_DIM // 4
POOL_WINDOWS = (2, 4, 8, 16)
POOL_GROUP_DIM = D_MODEL // 4
D_FF = 2816
N_MOD = 9
LN_EPS = 1e-5
DEEPNORM_ALPHA = (2.0 * DEPTH) ** 0.25
ATTN_SCALE = HEAD_DIM ** -0.5
LOG2_E = 1.4426950408889634
Q_SCALE = ATTN_SCALE * LOG2_E
NEG_INF = -1e30

N_PROMPT_TOK = BATCH * SEQ
N_LATENT_TOK = DEC_BATCH * DEC_SEQ
N_TOK = N_PROMPT_TOK + N_LATENT_TOK
MOD_GROUP_ROWS = 4096
N_COND = 1 + DEC_BATCH
COND_ROWS = 8

LANES = 128
HALF_LANES = LANES // 2
TM = 1024
N_PROMPT_TILES = N_PROMPT_TOK // TM
FF_CHUNK = 256
EPILOGUE_ROWS = 256
POOL_TM = 256
POOL_HALO = 8
D_Q = N_HEADS * HEAD_DIM
D_KV = N_KV_HEADS * HEAD_DIM
D_KV_AB = N_KV_HEADS * 2 * LANES
VMEM_LIMIT = 40 * 1024 * 1024
FFN_VMEM_LIMIT = 55 * 1024 * 1024

F32 = jnp.float32
BF16 = jnp.bfloat16


def _layer_norm(y, g, b):
    mu = jnp.mean(y, axis=-1, keepdims=True)
    yc = y - mu
    var = jnp.mean(yc * yc, axis=-1, keepdims=True)
    return yc * lax.rsqrt(var + LN_EPS) * g + b


def _modulate(x, mod_ref, row0):
    shift = mod_ref[row0:row0 + 1, :]
    scale = mod_ref[row0 + 1:row0 + 2, :]
    return x * (1.0 + scale) + shift


def _params(n_axes=1, vmem_limit=VMEM_LIMIT):
    return pltpu.CompilerParams(
        dimension_semantics=("arbitrary",) * n_axes,
        vmem_limit_bytes=vmem_limit)


def _resident(shape):
    nd = len(shape)
    return pl.BlockSpec(shape, lambda *_: (0,) * nd, pipeline_mode=pl.Buffered(1))


def _resident_at(index, tail):
    return pl.BlockSpec((None,) * len(index) + tuple(tail),
                        lambda *_: tuple(index) + (0,) * len(tail), pipeline_mode=pl.Buffered(1))


def _mod_spec(tile_rows, layer):
    tiles_per_group = MOD_GROUP_ROWS // tile_rows
    return pl.BlockSpec((None, None, N_MOD, D_MODEL),
                        lambda i: (layer, i // tiles_per_group, 0, 0))


def _ln_specs(layer, which):
    return [_resident_at((layer * 3 + which,), (1, D_MODEL))] * 2


def _slab_spec(cols, tm=TM):
    return pl.BlockSpec((tm, cols), lambda i: (i, 0))


def _prompt_spec(cols, tm=TM):
    return pl.BlockSpec((tm, cols), lambda i: (jnp.minimum(i, N_PROMPT_TOK // tm - 1), 0))


def _latent_spec(cols, tm=TM):
    return pl.BlockSpec((tm, cols), lambda i: (jnp.maximum(i - N_PROMPT_TOK // tm, 0), 0))


def _read_split(p_ref, l_ref, n_prompt_tiles=N_PROMPT_TILES):
    return jnp.where(pl.program_id(0) < n_prompt_tiles, p_ref[...], l_ref[...])


ADALN_NC = 2304


def _adaln_kernel(cond_ref, w_ref, b_ref, o_ref):
    c = cond_ref[...]
    s = (c * (1.0 / (1.0 + jnp.exp(-c)))).astype(BF16)
    o_ref[0] = jnp.dot(s, w_ref[0].astype(BF16), preferred_element_type=F32) + b_ref[0]


def _adaln(cond, w_mod, b_mod):
    n_out = N_MOD * D_MODEL
    out = pl.pallas_call(
        _adaln_kernel,
        grid=(DEPTH, n_out // ADALN_NC),
        in_specs=[
            pl.BlockSpec((COND_ROWS, D_MODEL), lambda l, n: (0, 0)),
            pl.BlockSpec((1, D_MODEL, ADALN_NC), lambda l, n: (l, 0, n)),
            pl.BlockSpec((1, 1, ADALN_NC), lambda l, n: (l, 0, n)),
        ],
        out_specs=pl.BlockSpec((1, COND_ROWS, ADALN_NC), lambda l, n: (l, 0, n)),
        out_shape=jax.ShapeDtypeStruct((DEPTH, COND_ROWS, n_out), F32),
        compiler_params=_params(2),
        name="adaln",
    )(cond, w_mod, b_mod.reshape(DEPTH, 1, n_out))
    return out[:, :N_COND].reshape(DEPTH, N_COND, N_MOD, D_MODEL)


def _ffn_gate_up(x, mod_ref, wg_ref, wu_ref, a_ref, row0):
    h = _modulate(x, mod_ref, row0).astype(BF16)
    for c0 in range(0, D_FF, FF_CHUNK):
        sl = slice(c0, min(c0 + FF_CHUNK, D_FF))
        g = jnp.dot(h, wg_ref[:, sl], preferred_element_type=F32)
        u = jnp.dot(h, wu_ref[:, sl], preferred_element_type=F32)
        a_ref[:, sl] = (g * (1.0 / (1.0 + jnp.exp(-g))) * u).astype(BF16)


def _ffn_epilogue(x, f, mod_ref, lng_ref, lnb_ref, row0):
    gate = mod_ref[row0 + 2:row0 + 3, :]
    y = DEEPNORM_ALPHA * x + (0.5 * gate) * f
    return _layer_norm(y, lng_ref[...], lnb_ref[...])


CAST_STEPS = 8
OWN_CHUNKS = 16
STAGE_SLOTS = 4
FFN_WEIGHT_SHAPES = ((D_MODEL, D_FF), (D_MODEL, D_FF), (D_FF, D_MODEL))


def _weight_chunk_copy(src_ref, lead, stage_ref, sem_ref, chunk, slot):
    rows = stage_ref.shape[1]
    src = src_ref.at[lead[0], lead[1], pl.ds(chunk * rows, rows), :]
    return pltpu.make_async_copy(src, stage_ref.at[slot], sem_ref.at[slot])


def _load_own_weights(lead, srcs, dsts, stages, sems):
    jobs, used = [], {}
    for src, dst in zip(srcs, dsts):
        key = dst.shape[1]
        for chunk in range(OWN_CHUNKS):
            slot = used.get(key, 0) % STAGE_SLOTS
            used[key] = used.get(key, 0) + 1
            jobs.append((_weight_chunk_copy(src, lead, stages[key], sems[key], chunk, slot),
                         stages[key], slot, dst, chunk))
    ahead = STAGE_SLOTS - 1
    for job in jobs[:ahead]:
        job[0].start()
    for n, (copy, stage, slot, dst, chunk) in enumerate(jobs):
        if n + ahead < len(jobs):
            jobs[n + ahead][0].start()
        copy.wait()
        rows = stage.shape[1]
        dst[chunk * rows:(chunk + 1) * rows, :] = stage[slot].astype(BF16)


def _ffn_kernel(*refs, row0, n_prompt_tiles, split_in, split_out, cast_next, own_lead):
    n_x = 2 if split_in else 1
    n_cast = 3 if cast_next else 0
    n_own = 7 if own_lead is not None else 0
    own = refs[len(refs) - n_own:]
    refs = refs[:len(refs) - n_own]
    x_refs, (mod_ref, lng_ref, lnb_ref, wg_ref, wu_ref, wd_ref) = refs[:n_x], refs[n_x:n_x + 6]
    next_f32 = refs[n_x + 6:n_x + 6 + n_cast]
    o_refs, next_bf16, a_ref = refs[n_x + 6 + n_cast:-1 - n_cast], refs[-1 - n_cast:-1], refs[-1]
    step = pl.program_id(0)
    if own_lead is not None:
        own_w, stage_a, stage_b, sem_a, sem_b = own[:3], own[3], own[4], own[5], own[6]
        stages = {stage_a.shape[2]: stage_a, stage_b.shape[2]: stage_b}
        sems = {stage_a.shape[2]: sem_a, stage_b.shape[2]: sem_b}

        @pl.when(step == 0)
        def _():
            _load_own_weights(own_lead, (wg_ref, wu_ref, wd_ref), own_w, stages, sems)

        wg_ref, wu_ref, wd_ref = own_w
    x = _read_split(*x_refs, n_prompt_tiles) if split_in else x_refs[0][...]
    _ffn_gate_up(x, mod_ref, wg_ref, wu_ref, a_ref, row0)
    chunks = []
    for r0 in range(0, x.shape[0], EPILOGUE_ROWS):
        rows = slice(r0, r0 + EPILOGUE_ROWS)
        f = jnp.dot(a_ref[rows, :], wd_ref[...], preferred_element_type=F32)
        chunk = _ffn_epilogue(x[rows], f, mod_ref, lng_ref, lnb_ref, row0)
        if split_out:
            chunks.append(chunk)
        else:
            o_refs[0][rows, :] = chunk
    if split_out:
        out = jnp.concatenate(chunks, axis=0)

        @pl.when(step < n_prompt_tiles)
        def _():
            o_refs[0][...] = out

        @pl.when(step >= n_prompt_tiles)
        def _():
            o_refs[1][...] = out
    if cast_next:
        @pl.when(step < CAST_STEPS)
        def _():
            for src, dst in zip(next_f32, next_bf16):
                dst[...] = src[...].astype(BF16)


def _cast_block_spec(shape, lead):
    block = (None,) * len(lead) + (shape[0] // CAST_STEPS, shape[1])
    return pl.BlockSpec(block, lambda i: tuple(lead) + (jnp.minimum(i, CAST_STEPS - 1), 0))


def _ffn(xs, mods, ln_g, ln_b, weights, layer, which, next_f32=None, split_out=False):
    split_in = len(xs) == 2
    cast_next = next_f32 is not None
    own_lead = (layer, which) if weights is None else None
    row0 = 6 * which
    ln_row = 2 * which
    tm = TM // 2 if split_in and cast_next else TM
    x_specs = ([_prompt_spec(D_MODEL, tm), _latent_spec(D_MODEL, tm)] if split_in
               else [_slab_spec(D_MODEL, tm)])
    if split_out:
        out_specs = [_prompt_spec(D_MODEL, tm), _latent_spec(D_MODEL, tm)]
        out_shape = [jax.ShapeDtypeStruct((N_PROMPT_TOK, D_MODEL), F32),
                     jax.ShapeDtypeStruct((N_LATENT_TOK, D_MODEL), F32)]
    else:
        out_specs = [_slab_spec(D_MODEL, tm)]
        out_shape = [jax.ShapeDtypeStruct((N_TOK, D_MODEL), F32)]
    n_x_out = len(out_specs)
    cast_in_specs, cast_args = [], ()
    if cast_next:
        cast_args, lead = next_f32
        cast_in_specs = [_cast_block_spec(s, lead) for s in FFN_WEIGHT_SHAPES]
        out_specs = out_specs + [_cast_block_spec(s, ()) for s in FFN_WEIGHT_SHAPES]
        out_shape = out_shape + [jax.ShapeDtypeStruct(s, BF16) for s in FFN_WEIGHT_SHAPES]
    scratch = [pltpu.VMEM((tm, D_FF), BF16)]
    if own_lead is None:
        weight_specs = [_resident(s) for s in FFN_WEIGHT_SHAPES]
    else:
        weights = next_f32[0]
        weight_specs = [pl.BlockSpec(memory_space=pl.ANY)] * 3
        stage_shapes = sorted({(STAGE_SLOTS, s[0] // OWN_CHUNKS, s[1]) for s in FFN_WEIGHT_SHAPES})
        scratch += [pltpu.VMEM(s, BF16) for s in FFN_WEIGHT_SHAPES]
        scratch += [pltpu.VMEM(s, F32) for s in stage_shapes]
        scratch += [pltpu.SemaphoreType.DMA((STAGE_SLOTS,)) for _ in stage_shapes]
    outs = pl.pallas_call(
        functools.partial(_ffn_kernel, row0=row0, n_prompt_tiles=N_PROMPT_TOK // tm,
                          split_in=split_in, split_out=split_out, cast_next=cast_next,
                          own_lead=own_lead),
        grid=(N_TOK // tm,),
        in_specs=x_specs + [_mod_spec(tm, layer)] + _ln_specs(layer, ln_row)
        + weight_specs + cast_in_specs,
        out_specs=out_specs,
        out_shape=out_shape,
        scratch_shapes=scratch,
        compiler_params=_params(vmem_limit=FFN_VMEM_LIMIT),
        name="ffn",
    )(*xs, mods, ln_g, ln_b, *weights, *cast_args)
    x_out = tuple(outs[:n_x_out]) if split_out else outs[0]
    return x_out, (tuple(outs[n_x_out:]) if cast_next else None)


ROPE_TILES = DEC_SEQ // TM


def _write_ab(dst_ref, rows, pair, src):
    lo = lax.broadcasted_iota(jnp.int32, src.shape, 1) < HALF_LANES
    swapped = pltpu.roll(src, HALF_LANES, 1)
    blocks = (jnp.where(lo, src, 0.0), jnp.where(lo, 0.0, swapped),
              jnp.where(lo, swapped, 0.0), jnp.where(lo, 0.0, src))
    for n, blk in enumerate(blocks):
        c0 = (4 * pair + n) * LANES
        dst_ref[rows, c0:c0 + LANES] = blk.astype(BF16)


def _qkv_kernel(x_ref, mod_ref, w_ref, cos_ref, sin_ref, q_ref, kab_ref, vab_ref, ks_ref, vs_ref):
    kv_raw = []
    is_prompt = pl.program_id(0) < N_PROMPT_TILES
    first_half = (lax.broadcasted_iota(jnp.int32, (EPILOGUE_ROWS, LANES), 1) & 16) == 0
    for r in range(TM // EPILOGUE_ROWS):
        rows = slice(r * EPILOGUE_ROWS, (r + 1) * EPILOGUE_ROWS)
        h = _modulate(x_ref[rows, :], mod_ref, 3).astype(BF16)
        qkv = jnp.dot(h, w_ref[...], preferred_element_type=F32)
        cos = jnp.where(is_prompt, 1.0, cos_ref[rows, :])
        sin = jnp.where(is_prompt, 0.0, sin_ref[rows, :])

        def rope(blk, c, s):
            up = pltpu.roll(blk, LANES - 16, 1)
            dn = pltpu.roll(blk, 16, 1)
            return blk * c + jnp.where(first_half, up, dn) * s

        cos_q, sin_q = cos * Q_SCALE, sin * Q_SCALE
        for j in range(D_Q // LANES):
            cols = slice(j * LANES, (j + 1) * LANES)
            q_ref[rows, cols] = rope(qkv[:, cols], cos_q, sin_q).astype(BF16)
        for pair in range(D_KV // LANES):
            k_cols = slice(D_Q + pair * LANES, D_Q + (pair + 1) * LANES)
            v_cols = slice(D_Q + D_KV + pair * LANES, D_Q + D_KV + (pair + 1) * LANES)
            _write_ab(kab_ref, rows, pair, rope(qkv[:, k_cols], cos, sin))
            _write_ab(vab_ref, rows, pair, qkv[:, v_cols])
        kv_raw.append(qkv[:, D_Q:])

    @pl.when(pl.program_id(0) < N_PROMPT_TILES)
    def _():
        kv_t = jnp.concatenate(kv_raw, axis=0).T
        for s in range(TM // SEQ):
            ks_ref[s] = kv_t[:D_KV, s * SEQ:(s + 1) * SEQ]
            vs_ref[s] = kv_t[D_KV:, s * SEQ:(s + 1) * SEQ]


def _rope_index(i):
    return (jnp.maximum(i - N_PROMPT_TILES, 0) % ROPE_TILES, 0)


def _qkv(x, mods, w, cos_t, sin_t):
    seqs_per_tile = TM // SEQ
    state_spec = pl.BlockSpec((seqs_per_tile, D_KV, SEQ),
                              lambda i: (jnp.minimum(i, N_PROMPT_TILES - 1), 0, 0))
    return pl.pallas_call(
        _qkv_kernel,
        grid=(N_TOK // TM,),
        in_specs=[
            _slab_spec(D_MODEL),
            _mod_spec(TM, 0),
            _resident((D_MODEL, D_Q + 2 * D_KV)),
            pl.BlockSpec((TM, LANES), _rope_index),
            pl.BlockSpec((TM, LANES), _rope_index),
        ],
        out_specs=[
            _slab_spec(D_Q), _slab_spec(D_KV_AB), _slab_spec(D_KV_AB), state_spec, state_spec,
        ],
        out_shape=[
            jax.ShapeDtypeStruct((N_TOK, D_Q), BF16),
            jax.ShapeDtypeStruct((N_TOK, D_KV_AB), BF16),
            jax.ShapeDtypeStruct((N_TOK, D_KV_AB), BF16),
            jax.ShapeDtypeStruct((BATCH, D_KV, SEQ), F32),
            jax.ShapeDtypeStruct((BATCH, D_KV, SEQ), F32),
        ],
        compiler_params=_params(),
        name="qkv",
    )(x, mods, w, cos_t, sin_t)


def _attend(q_ref, o_ref, sink_ref, k_slabs, v_slabs, masks, tq, row0=0):
    nt = (((1,), (1,)), ((), ()))
    rows = slice(row0, row0 + tq)
    row_hi = lax.broadcasted_iota(jnp.int32, (2 * tq, 1), 0) >= tq
    lane_lo = lax.broadcasted_iota(jnp.int32, (2 * tq, LANES), 1) < HALF_LANES
    for kh in range(N_KV_HEADS):
        ks = k_slabs(kh)
        vs = v_slabs(kh)
        k_cat = jnp.concatenate([s[:, :LANES] for s in ks] + [s[:, LANES:] for s in ks], axis=0)
        v_cat = jnp.concatenate([s[:, :LANES] for s in vs] + [s[:, LANES:] for s in vs], axis=0)
        n_keys = k_cat.shape[0] // 2
        j0 = 2 * kh
        q2 = jnp.concatenate([q_ref[rows, j0 * LANES:(j0 + 1) * LANES],
                              q_ref[rows, (j0 + 1) * LANES:(j0 + 2) * LANES]], axis=0)
        s = lax.dot_general(q2, k_cat, nt, preferred_element_type=F32)
        es, inv_dens = [], []
        for half in range(2):
            segs, off = [], half * n_keys
            for slab, m in zip(ks, masks):
                seg = s[:, off:off + slab.shape[0]]
                segs.append(seg if m is None else jnp.where(m, seg, NEG_INF))
                off += slab.shape[0]
            logits = jnp.concatenate(segs, axis=1)
            sink = LOG2_E * jnp.where(row_hi, sink_ref[4 * kh + 2 + half], sink_ref[4 * kh + half])
            m_row = jnp.maximum(jnp.max(logits, axis=-1, keepdims=True), sink)
            e = jnp.exp2(logits - m_row)
            den = jnp.sum(e, axis=-1, keepdims=True) + jnp.exp2(sink - m_row)
            es.append(e.astype(BF16))
            inv_dens.append(1.0 / den)
        p = jnp.concatenate(es, axis=1)
        o2 = jnp.dot(p, v_cat, preferred_element_type=F32)
        o2 = o2 * jnp.where(lane_lo, inv_dens[0], inv_dens[1])
        o_ref[rows, j0 * LANES:(j0 + 1) * LANES] = o2[:tq].astype(BF16)
        o_ref[rows, (j0 + 1) * LANES:(j0 + 2) * LANES] = o2[tq:].astype(BF16)


def _kv_cols(kh):
    return slice(kh * 2 * LANES, (kh + 1) * 2 * LANES)


CTX_SUB = 4


def _ctx_attn_kernel(sink_ref, q_ref, k_ref, v_ref, o_ref):
    for sub in range(CTX_SUB):
        rows = slice(sub * SEQ, (sub + 1) * SEQ)
        _attend(q_ref, o_ref, sink_ref,
                lambda kh, rows=rows: [k_ref[rows, _kv_cols(kh)]],
                lambda kh, rows=rows: [v_ref[rows, _kv_cols(kh)]],
                [None], SEQ, row0=sub * SEQ)


def _ctx_attn(sink, q, kab, vab):
    step_rows = CTX_SUB * SEQ
    return pl.pallas_call(
        _ctx_attn_kernel,
        grid=(BATCH // CTX_SUB,),
        in_specs=[
            pl.BlockSpec(memory_space=pltpu.SMEM),
            pl.BlockSpec((step_rows, D_Q), lambda b: (b, 0)),
            pl.BlockSpec((step_rows, D_KV_AB), lambda b: (b, 0)),
            pl.BlockSpec((step_rows, D_KV_AB), lambda b: (b, 0)),
        ],
        out_specs=pl.BlockSpec((step_rows, D_Q), lambda b: (b, 0)),
        out_shape=jax.ShapeDtypeStruct((N_PROMPT_TOK, D_Q), BF16),
        compiler_params=_params(),
        name="ctx_attn",
    )(sink, q, kab, vab)


N_QBLK = DEC_SEQ // BLOCK


LAT_SUB = 8
LAT_STEPS = N_QBLK // LAT_SUB


def _lat_attn_kernel(sink_ref, q_ref, kp_ref, kc_ref, kn_ref, vp_ref, vc_ref, vn_ref,
                     kx_ref, vx_ref, o_ref):
    step = pl.program_id(1)
    r = lax.broadcasted_iota(jnp.int32, (2 * BLOCK, BLOCK), 0) & (BLOCK - 1)
    c = lax.broadcasted_iota(jnp.int32, (2 * BLOCK, BLOCK), 1)

    def blocks(p_ref, c_ref, n_ref, kh):
        cols = _kv_cols(kh)
        return ([p_ref[:, cols]]
                + [c_ref[j * BLOCK:(j + 1) * BLOCK, cols] for j in range(LAT_SUB)]
                + [n_ref[:, cols]])

    for sub in range(LAT_SUB):
        no_prev = (step == 0) if sub == 0 else False
        no_next = (step == LAT_STEPS - 1) if sub == LAT_SUB - 1 else False
        m_prev = c >= r + jnp.where(no_prev, BLOCK, 0)
        m_next = c <= r - jnp.where(no_next, BLOCK, 0)

        def slabs(p_ref, c_ref, n_ref, x_ref, sub=sub):
            return lambda kh: blocks(p_ref, c_ref, n_ref, kh)[sub:sub + 3] + [x_ref[0, :, _kv_cols(kh)]]

        _attend(q_ref, o_ref, sink_ref,
                slabs(kp_ref, kc_ref, kn_ref, kx_ref), slabs(vp_ref, vc_ref, vn_ref, vx_ref),
                [m_prev, None, m_next, None], BLOCK, row0=sub * BLOCK)


def _lat_attn(sink, q, kab, vab, kab_ctx, vab_ctx):
    tq = LAT_SUB * BLOCK
    first = N_PROMPT_TOK // BLOCK

    def cur(b, i):
        return (N_PROMPT_TOK // tq + b * LAT_STEPS + i, 0)

    def prev(b, i):
        return (first + b * N_QBLK + jnp.maximum(i * LAT_SUB - 1, 0), 0)

    def nxt(b, i):
        return (first + b * N_QBLK + jnp.minimum((i + 1) * LAT_SUB, N_QBLK - 1), 0)

    edge_spec = lambda f: pl.BlockSpec((BLOCK, D_KV_AB), f)
    cur_spec = pl.BlockSpec((tq, D_KV_AB), cur)
    ctx_spec = pl.BlockSpec((1, PAST_LEN, D_KV_AB), lambda b, i: (b, 0, 0))
    return pl.pallas_call(
        _lat_attn_kernel,
        grid=(DEC_BATCH, LAT_STEPS),
        in_specs=[
            pl.BlockSpec(memory_space=pltpu.SMEM),
            pl.BlockSpec((tq, D_Q), cur),
            edge_spec(prev), cur_spec, edge_spec(nxt),
            edge_spec(prev), cur_spec, edge_spec(nxt),
            ctx_spec, ctx_spec,
        ],
        out_specs=pl.BlockSpec((tq, D_Q), lambda b, i: (b * LAT_STEPS + i, 0)),
        out_shape=jax.ShapeDtypeStruct((N_LATENT_TOK, D_Q), BF16),
        compiler_params=_params(2, vmem_limit=FFN_VMEM_LIMIT),
        name="lat_attn",
    )(sink, q, kab, kab, kab, vab, vab, vab, kab_ctx, vab_ctx)


def _proj_kernel(x_ref, ap_ref, al_ref, mod_ref, lng_ref, lnb_ref, wo_ref, o_ref):
    a = _read_split(ap_ref, al_ref)
    gate = mod_ref[5:6, :]
    for r in range(TM // EPILOGUE_ROWS):
        rows = slice(r * EPILOGUE_ROWS, (r + 1) * EPILOGUE_ROWS)
        f = jnp.dot(a[rows], wo_ref[...], preferred_element_type=F32)
        y = DEEPNORM_ALPHA * x_ref[rows, :] + gate * f
        o_ref[rows, :] = _layer_norm(y, lng_ref[...], lnb_ref[...])


def _proj(x, attn_prompt, attn_latent, mods, ln_g, ln_b, w_o):
    return pl.pallas_call(
        _proj_kernel,
        grid=(N_TOK // TM,),
        in_specs=[
            _slab_spec(D_MODEL),
            _prompt_spec(D_Q),
            _latent_spec(D_Q),
            _mod_spec(TM, 0)] + _ln_specs(0, 1) + [
            _resident((D_Q, D_MODEL)),
        ],
        out_specs=_slab_spec(D_MODEL),
        out_shape=jax.ShapeDtypeStruct((N_TOK, D_MODEL), F32),
        compiler_params=_params(),
        name="attn_proj",
    )(x, attn_prompt, attn_latent, mods, ln_g, ln_b, w_o)


POOL_SUB = 4
POOL_STEP_ROWS = POOL_SUB * POOL_TM
PROMPT_POOL_STEPS = N_PROMPT_TOK // POOL_STEP_ROWS
LATENT_POOL_STEPS = DEC_SEQ // POOL_STEP_ROWS


def _pool_kernel(x_ref, xp_ref, xn_ref, mod_ref, lng_ref, lnb_ref, w_ref, sc_ref, o_ref):
    i = pl.program_id(0)
    is_prompt = i < PROMPT_POOL_STEPS
    in_seq = (i - PROMPT_POOL_STEPS) % LATENT_POOL_STEPS
    h_all = _modulate(x_ref[...], mod_ref, 3)
    h_before = _modulate(xp_ref[...], mod_ref, 3)
    h_after = _modulate(xn_ref[...], mod_ref, 3)
    gate_scale = mod_ref[5:6, :] * sc_ref[...]
    for sub in range(POOL_SUB):
        rows = slice(sub * POOL_TM, (sub + 1) * POOL_TM)
        is_start = is_prompt | (in_seq == 0) if sub == 0 else is_prompt
        is_end = is_prompt | (in_seq == LATENT_POOL_STEPS - 1) if sub == POOL_SUB - 1 else is_prompt
        before = h_before if sub == 0 else h_all[rows.start - POOL_HALO:rows.start]
        after = h_after if sub == POOL_SUB - 1 else h_all[rows.stop:rows.stop + POOL_HALO]
        _pool_sub_tile(x_ref[rows, :], h_all[rows], before, after, is_start, is_end, gate_scale,
                       lng_ref, lnb_ref, w_ref, o_ref, rows)


def _pool_sub_tile(x, h, before, after, is_start, is_end, gate_scale, lng_ref, lnb_ref, w_ref,
                   o_ref, rows):
    h_ext = jnp.concatenate([jnp.where(is_start, 0.0, before), h, jnp.where(is_end, 0.0, after)],
                            axis=0)
    n_ext = POOL_TM + 2 * POOL_HALO

    r8 = lax.broadcasted_iota(jnp.int32, (POOL_HALO, POOL_GROUP_DIM), 0)
    outs = []
    for gi, w in enumerate(POOL_WINDOWS):
        half = w // 2
        cols = slice(gi * POOL_GROUP_DIM, (gi + 1) * POOL_GROUP_DIM)
        acc = h_ext[:, cols]
        s = 1
        while s < w:
            acc = acc + pltpu.roll(acc, s, 0)
            s *= 2
        if half > 1:
            acc = pltpu.roll(acc, n_ext - (half - 1), 0)
        total = acc[POOL_HALO:POOL_HALO + POOL_TM]
        cnt_top = w - jnp.where(is_start, jnp.maximum(half - r8, 0), 0)
        cnt_bot = w - jnp.where(is_end, jnp.maximum(r8 + half - POOL_HALO, 0), 0)
        inv_cnt = jnp.concatenate([
            1.0 / cnt_top.astype(F32),
            jnp.full((POOL_TM - 2 * POOL_HALO, POOL_GROUP_DIM), 1.0 / w, F32),
            1.0 / cnt_bot.astype(F32)], axis=0)
        pooled = (total * inv_cnt - h[:, cols]).astype(BF16)
        outs.append(jnp.dot(pooled, w_ref[gi], preferred_element_type=F32))
    y = DEEPNORM_ALPHA * x + gate_scale * jnp.concatenate(outs, axis=-1)
    o_ref[rows, :] = _layer_norm(y, lng_ref[...], lnb_ref[...])


def _pool(x, mods, ln_g, ln_b, w_pool, scale):
    halo_per_tile = POOL_STEP_ROWS // POOL_HALO
    last_halo = N_TOK // POOL_HALO - 1
    return pl.pallas_call(
        _pool_kernel,
        grid=(N_TOK // POOL_STEP_ROWS,),
        in_specs=[
            pl.BlockSpec((POOL_STEP_ROWS, D_MODEL), lambda i: (i, 0)),
            pl.BlockSpec((POOL_HALO, D_MODEL), lambda i: (jnp.maximum(i * halo_per_tile - 1, 0), 0)),
            pl.BlockSpec((POOL_HALO, D_MODEL),
                         lambda i: (jnp.minimum((i + 1) * halo_per_tile, last_halo), 0)),
            _mod_spec(POOL_STEP_ROWS, 1)] + _ln_specs(1, 1) + [
            _resident((len(POOL_WINDOWS), POOL_GROUP_DIM, POOL_GROUP_DIM)),
            _resident((1, D_MODEL)),
        ],
        out_specs=pl.BlockSpec((POOL_STEP_ROWS, D_MODEL), lambda i: (i, 0)),
        out_shape=jax.ShapeDtypeStruct((N_TOK, D_MODEL), F32),
        compiler_params=_params(),
        name="pool",
    )(x, x, x, mods, ln_g, ln_b, w_pool, scale.reshape(1, D_MODEL))


def _ab_layout(a):
    lead = a.shape[:-1]
    a = a.reshape(lead + (N_KV_HEADS, 1, HEAD_DIM)).astype(BF16)
    z = jnp.zeros_like(a)
    return jnp.concatenate([a, z, z, a], axis=-2).reshape(lead + (D_KV_AB,))


def _rope_tables():
    n_rows = DEC_SEQ // GRID_W
    inv = jnp.power(ROPE_BASE, -jnp.arange(N_FREQ, dtype=F32) / N_FREQ)

    def expand(fn):
        by_row = fn(jnp.arange(n_rows, dtype=F32)[:, None] * inv)
        by_col = fn(jnp.arange(GRID_W, dtype=F32)[:, None] * inv)
        r = jnp.broadcast_to(by_row[:, None, :], (n_rows, GRID_W, N_FREQ))
        c = jnp.broadcast_to(by_col[None, :, :], (n_rows, GRID_W, N_FREQ))
        per_head = jnp.concatenate([r, r, c, c], axis=-1).reshape(DEC_SEQ, HEAD_DIM)
        return jnp.concatenate([per_head] * (LANES // HEAD_DIM), axis=-1)

    cos, sin = expand(jnp.cos), expand(jnp.sin)
    first_half = (jnp.arange(LANES) % 32) < 16
    return cos, jnp.where(first_half, -sin, sin)


def kernel(x_prompt, x_sample, cache_k, cache_v, c, c_ctx, w_mod, b_mod, ln_g, ln_b,
           ffn_w_gate, ffn_w_up, ffn_w_down, attn_w_qkv, attn_w_o, attn_sink,
           pool_w, pool_scale):
    cond = jnp.concatenate(
        [c_ctx[None, :], c, jnp.zeros((COND_ROWS - N_COND, D_MODEL), F32)], axis=0)
    mods = _adaln(cond, w_mod, b_mod)

    ffn_w = (ffn_w_gate, ffn_w_up, ffn_w_down)
    ln_g = ln_g.reshape(DEPTH * 3, 1, D_MODEL)
    ln_b = ln_b.reshape(DEPTH * 3, 1, D_MODEL)
    ffn = functools.partial(_ffn, mods=mods, ln_g=ln_g, ln_b=ln_b)

    x, w_next = ffn((x_prompt.reshape(N_PROMPT_TOK, D_MODEL), x_sample.reshape(N_LATENT_TOK, D_MODEL)),
                    weights=None, layer=0, which=0, next_f32=(ffn_w, (0, 1)))
    cos_t, sin_t = _rope_tables()
    q, kab, vab, k_state, v_state = _qkv(x, mods, attn_w_qkv[0].astype(BF16), cos_t, sin_t)
    sink = attn_sink[0]
    o_ctx = _ctx_attn(sink, q, kab, vab)
    kab_ctx = _ab_layout(cache_k[:, 0].reshape(DEC_BATCH, PAST_LEN, D_KV))
    vab_ctx = _ab_layout(cache_v[:, 0].reshape(DEC_BATCH, PAST_LEN, D_KV))
    o_lat = _lat_attn(sink, q, kab, vab, kab_ctx, vab_ctx)
    x = _proj(x, o_ctx, o_lat, mods, ln_g, ln_b, attn_w_o[0].astype(BF16))
    x, w_next = ffn((x,), weights=w_next, layer=0, which=1, next_f32=(ffn_w, (1, 0)))

    x, w_next = ffn((x,), weights=w_next, layer=1, which=0, next_f32=(ffn_w, (1, 1)))
    x = _pool(x, mods, ln_g, ln_b, pool_w[0].astype(BF16), pool_scale[0])
    (y_prompt, y_sample), _ = ffn((x,), weights=w_next, layer=1, which=1, split_out=True)

    def state(s_t):
        s_t = s_t.reshape(BATCH, N_KV_HEADS, HEAD_DIM, SEQ)
        return jnp.transpose(s_t, (0, 3, 1, 2))[:, None]

    return (y_prompt.reshape(BATCH, SEQ, D_MODEL),
            y_sample.reshape(DEC_BATCH, DEC_SEQ, D_MODEL),
            state(k_state), state(v_state))
```

```python
import functools

import jax
import jax.numpy as jnp
from jax import lax
from jax.experimental import pallas as pl
from jax.experimental.pallas import tpu as pltpu

D_MODEL = 1024
BATCH = 16
SEQ = 256
DEPTH = 2
DEC_BATCH = 2
DEC_SEQ = 4096
PAST_LEN = 256
GRID_W = 64
N_HEADS = 16
N_KV_HEADS = 4
HEAD_DIM = 64
WINDOW = 128
BLOCK = 128
ROPE_BASE = 10000.0
N_FREQ = HEAD_DIM // 4
POOL_WINDOWS = (2, 4, 8, 16)
POOL_GROUP_DIM = D_MODEL // 4
D_FF = 2816
N_MOD = 9
LN_EPS = 1e-5
DEEPNORM_ALPHA = (2.0 * DEPTH) ** 0.25
ATTN_SCALE = HEAD_DIM ** -0.5
LOG2_E = 1.4426950408889634
Q_SCALE = ATTN_SCALE * LOG2_E
NEG_INF = -1e30

N_PROMPT_TOK = BATCH * SEQ
N_LATENT_TOK = DEC_BATCH * DEC_SEQ
N_TOK = N_PROMPT_TOK + N_LATENT_TOK
MOD_GROUP_ROWS = 4096
N_COND = 1 + DEC_BATCH
COND_ROWS = 8

LANES = 128
HALF_LANES = LANES // 2
TM = 1024
N_PROMPT_TILES = N_PROMPT_TOK // TM
FF_CHUNK = 256
EPILOGUE_ROWS = 256
POOL_TM = 256
POOL_HALO = 8
D_Q = N_HEADS * HEAD_DIM
D_KV = N_KV_HEADS * HEAD_DIM
D_KV_AB = N_KV_HEADS * 2 * LANES
VMEM_LIMIT = 40 * 1024 * 1024
FFN_VMEM_LIMIT = 55 * 1024 * 1024

F32 = jnp.float32
BF16 = jnp.bfloat16


def _layer_norm(y, g, b):
    mu = jnp.mean(y, axis=-1, keepdims=True)
    yc = y - mu
    var = jnp.mean(yc * yc, axis=-1, keepdims=True)
    return yc * lax.rsqrt(var + LN_EPS) * g + b


def _modulate(x, mod_ref, row0):
    shift = mod_ref[row0:row0 + 1, :]
    scale = mod_ref[row0 + 1:row0 + 2, :]
    return x * (1.0 + scale) + shift


def _params(n_axes=1, vmem_limit=VMEM_LIMIT):
    return pltpu.CompilerParams(
        dimension_semantics=("arbitrary",) * n_axes,
        vmem_limit_bytes=vmem_limit)


def _resident(shape):
    nd = len(shape)
    return pl.BlockSpec(shape, lambda *_: (0,) * nd, pipeline_mode=pl.Buffered(1))


def _resident_at(index, tail):
    return pl.BlockSpec((None,) * len(index) + tuple(tail),
                        lambda *_: tuple(index) + (0,) * len(tail), pipeline_mode=pl.Buffered(1))


def _mod_spec(tile_rows, layer):
    tiles_per_group = MOD_GROUP_ROWS // tile_rows
    return pl.BlockSpec((None, None, N_MOD, D_MODEL),
                        lambda i: (layer, i // tiles_per_group, 0, 0))


def _ln_specs(layer, which):
    return [_resident_at((layer * 3 + which,), (1, D_MODEL))] * 2


def _slab_spec(cols, tm=TM):
    return pl.BlockSpec((tm, cols), lambda i: (i, 0))


def _prompt_spec(cols, tm=TM):
    return pl.BlockSpec((tm, cols), lambda i: (jnp.minimum(i, N_PROMPT_TOK // tm - 1), 0))


def _latent_spec(cols, tm=TM):
    return pl.BlockSpec((tm, cols), lambda i: (jnp.maximum(i - N_PROMPT_TOK // tm, 0), 0))


def _read_split(p_ref, l_ref, n_prompt_tiles=N_PROMPT_TILES):
    return jnp.where(pl.program_id(0) < n_prompt_tiles, p_ref[...], l_ref[...])


ADALN_NC = 2304


def _adaln_kernel(cond_ref, w_ref, b_ref, o_ref):
    c = cond_ref[...]
    s = (c * (1.0 / (1.0 + jnp.exp(-c)))).astype(BF16)
    o_ref[0] = jnp.dot(s, w_ref[0].astype(BF16), preferred_element_type=F32) + b_ref[0]


def _adaln(cond, w_mod, b_mod):
    n_out = N_MOD * D_MODEL
    out = pl.pallas_call(
        _adaln_kernel,
        grid=(DEPTH, n_out // ADALN_NC),
        in_specs=[
            pl.BlockSpec((COND_ROWS, D_MODEL), lambda l, n: (0, 0)),
            pl.BlockSpec((1, D_MODEL, ADALN_NC), lambda l, n: (l, 0, n)),
            pl.BlockSpec((1, 1, ADALN_NC), lambda l, n: (l, 0, n)),
        ],
        out_specs=pl.BlockSpec((1, COND_ROWS, ADALN_NC), lambda l, n: (l, 0, n)),
        out_shape=jax.ShapeDtypeStruct((DEPTH, COND_ROWS, n_out), F32),
        compiler_params=_params(2),
        name="adaln",
    )(cond, w_mod, b_mod.reshape(DEPTH, 1, n_out))
    return out[:, :N_COND].reshape(DEPTH, N_COND, N_MOD, D_MODEL)


def _ffn_gate_up(x, mod_ref, wg_ref, wu_ref, a_ref, row0):
    h = _modulate(x, mod_ref, row0).astype(BF16)
    for c0 in range(0, D_FF, FF_CHUNK):
        sl = slice(c0, min(c0 + FF_CHUNK, D_FF))
        g = jnp.dot(h, wg_ref[:, sl], preferred_element_type=F32)
        u = jnp.dot(h, wu_ref[:, sl], preferred_element_type=F32)
        a_ref[:, sl] = (g * (1.0 / (1.0 + jnp.exp(-g))) * u).astype(BF16)


def _ffn_epilogue(x, f, mod_ref, lng_ref, lnb_ref, row0):
    gate = mod_ref[row0 + 2:row0 + 3, :]
    y = DEEPNORM_ALPHA * x + (0.5 * gate) * f
    return _layer_norm(y, lng_ref[...], lnb_ref[...])


CAST_STEPS = 8
OWN_CHUNKS = 16
STAGE_SLOTS = 8
FFN_WEIGHT_SHAPES = ((D_MODEL, D_FF), (D_MODEL, D_FF), (D_FF, D_MODEL))


def _weight_chunk_copy(src_ref, lead, stage_ref, sem_ref, chunk, slot):
    rows = stage_ref.shape[1]
    src = src_ref.at[lead[0], lead[1], pl.ds(chunk * rows, rows), :]
    return pltpu.make_async_copy(src, stage_ref.at[slot], sem_ref.at[slot])


def _load_own_weights(lead, srcs, dsts, stages, sems):
    jobs, used = [], {}
    for src, dst in zip(srcs, dsts):
        key = dst.shape[1]
        for chunk in range(OWN_CHUNKS):
            slot = used.get(key, 0) % STAGE_SLOTS
            used[key] = used.get(key, 0) + 1
            jobs.append((_weight_chunk_copy(src, lead, stages[key], sems[key], chunk, slot),
                         stages[key], slot, dst, chunk))
    ahead = STAGE_SLOTS - 1
    for job in jobs[:ahead]:
        job[0].start()
    for n, (copy, stage, slot, dst, chunk) in enumerate(jobs):
        if n + ahead < len(jobs):
            jobs[n + ahead][0].start()
        copy.wait()
        rows = stage.shape[1]
        dst[chunk * rows:(chunk + 1) * rows, :] = stage[slot].astype(BF16)


def _ffn_kernel(*refs, row0, n_prompt_tiles, split_in, split_out, cast_next, cast_steps, own_lead):
    n_x = 2 if split_in else 1
    n_cast = 3 if cast_next else 0
    n_own = 7 if own_lead is not None else 0
    own = refs[len(refs) - n_own:]
    refs = refs[:len(refs) - n_own]
    x_refs, (mod_ref, lng_ref, lnb_ref, wg_ref, wu_ref, wd_ref) = refs[:n_x], refs[n_x:n_x + 6]
    next_f32 = refs[n_x + 6:n_x + 6 + n_cast]
    o_refs, next_bf16, a_ref = refs[n_x + 6 + n_cast:-1 - n_cast], refs[-1 - n_cast:-1], refs[-1]
    step = pl.program_id(0)
    if own_lead is not None:
        own_w, stage_a, stage_b, sem_a, sem_b = own[:3], own[3], own[4], own[5], own[6]
        stages = {stage_a.shape[2]: stage_a, stage_b.shape[2]: stage_b}
        sems = {stage_a.shape[2]: sem_a, stage_b.shape[2]: sem_b}

        @pl.when(step == 0)
        def _():
            _load_own_weights(own_lead, (wg_ref, wu_ref, wd_ref), own_w, stages, sems)

        wg_ref, wu_ref, wd_ref = own_w
    x = _read_split(*x_refs, n_prompt_tiles) if split_in else x_refs[0][...]
    _ffn_gate_up(x, mod_ref, wg_ref, wu_ref, a_ref, row0)
    chunks = []
    for r0 in range(0, x.shape[0], EPILOGUE_ROWS):
        rows = slice(r0, r0 + EPILOGUE_ROWS)
        f = jnp.dot(a_ref[rows, :], wd_ref[...], preferred_element_type=F32)
        chunk = _ffn_epilogue(x[rows], f, mod_ref, lng_ref, lnb_ref, row0)
        if split_out:
            chunks.append(chunk)
        else:
            o_refs[0][rows, :] = chunk
    if split_out:
        out = jnp.concatenate(chunks, axis=0)

        @pl.when(step < n_prompt_tiles)
        def _():
            o_refs[0][...] = out

        @pl.when(step >= n_prompt_tiles)
        def _():
            o_refs[1][...] = out
    if cast_next:
        @pl.when(step < cast_steps)
        def _():
            for src, dst in zip(next_f32, next_bf16):
                dst[...] = src[...].astype(BF16)


def _cast_block_spec(shape, lead, steps):
    block = (None,) * len(lead) + (shape[0] // steps, shape[1])
    return pl.BlockSpec(block, lambda i: tuple(lead) + (jnp.minimum(i, steps - 1), 0))


def _ffn(xs, mods, ln_g, ln_b, weights, layer, which, next_f32=None, split_out=False):
    split_in = len(xs) == 2
    cast_next = next_f32 is not None
    own_lead = (layer, which) if weights is None else None
    row0 = 6 * which
    ln_row = 2 * which
    tm = TM // 2 if split_in and cast_next else TM
    x_specs = ([_prompt_spec(D_MODEL, tm), _latent_spec(D_MODEL, tm)] if split_in
               else [_slab_spec(D_MODEL, tm)])
    if split_out:
        out_specs = [_prompt_spec(D_MODEL, tm), _latent_spec(D_MODEL, tm)]
        out_shape = [jax.ShapeDtypeStruct((N_PROMPT_TOK, D_MODEL), F32),
                     jax.ShapeDtypeStruct((N_LATENT_TOK, D_MODEL), F32)]
    else:
        out_specs = [_slab_spec(D_MODEL, tm)]
        out_shape = [jax.ShapeDtypeStruct((N_TOK, D_MODEL), F32)]
    n_x_out = len(out_specs)
    cast_steps = CAST_STEPS * (TM // tm)
    cast_in_specs, cast_args = [], ()
    if cast_next:
        cast_args, lead = next_f32
        cast_in_specs = [_cast_block_spec(s, lead, cast_steps) for s in FFN_WEIGHT_SHAPES]
        out_specs = out_specs + [_cast_block_spec(s, (), cast_steps) for s in FFN_WEIGHT_SHAPES]
        out_shape = out_shape + [jax.ShapeDtypeStruct(s, BF16) for s in FFN_WEIGHT_SHAPES]
    scratch = [pltpu.VMEM((tm, D_FF), BF16)]
    if own_lead is None:
        weight_specs = [_resident(s) for s in FFN_WEIGHT_SHAPES]
    else:
        weights = next_f32[0]
        weight_specs = [pl.BlockSpec(memory_space=pl.ANY)] * 3
        stage_shapes = sorted({(STAGE_SLOTS, s[0] // OWN_CHUNKS, s[1]) for s in FFN_WEIGHT_SHAPES})
        scratch += [pltpu.VMEM(s, BF16) for s in FFN_WEIGHT_SHAPES]
        scratch += [pltpu.VMEM(s, F32) for s in stage_shapes]
        scratch += [pltpu.SemaphoreType.DMA((STAGE_SLOTS,)) for _ in stage_shapes]
    outs = pl.pallas_call(
        functools.partial(_ffn_kernel, row0=row0, n_prompt_tiles=N_PROMPT_TOK // tm,
                          split_in=split_in, split_out=split_out, cast_next=cast_next,
                          cast_steps=cast_steps,
                          own_lead=own_lead),
        grid=(N_TOK // tm,),
        in_specs=x_specs + [_mod_spec(tm, layer)] + _ln_specs(layer, ln_row)
        + weight_specs + cast_in_specs,
        out_specs=out_specs,
        out_shape=out_shape,
        scratch_shapes=scratch,
        compiler_params=_params(vmem_limit=FFN_VMEM_LIMIT),
        name="ffn",
    )(*xs, mods, ln_g, ln_b, *weights, *cast_args)
    x_out = tuple(outs[:n_x_out]) if split_out else outs[0]
    return x_out, (tuple(outs[n_x_out:]) if cast_next else None)


ROPE_TILES = DEC_SEQ // TM


def _write_ab(dst_ref, rows, pair, src):
    lo = lax.broadcasted_iota(jnp.int32, src.shape, 1) < HALF_LANES
    swapped = pltpu.roll(src, HALF_LANES, 1)
    blocks = (jnp.where(lo, src, 0.0), jnp.where(lo, 0.0, swapped),
              jnp.where(lo, swapped, 0.0), jnp.where(lo, 0.0, src))
    for n, blk in enumerate(blocks):
        c0 = (4 * pair + n) * LANES
        dst_ref[rows, c0:c0 + LANES] = blk.astype(BF16)


def _qkv_kernel(x_ref, mod_ref, w_ref, cos_ref, sin_ref, q_ref, kab_ref, vab_ref, ks_ref, vs_ref):
    kv_raw = []
    is_prompt = pl.program_id(0) < N_PROMPT_TILES
    first_half = (lax.broadcasted_iota(jnp.int32, (EPILOGUE_ROWS, LANES), 1) & 16) == 0
    for r in range(TM // EPILOGUE_ROWS):
        rows = slice(r * EPILOGUE_ROWS, (r + 1) * EPILOGUE_ROWS)
        h = _modulate(x_ref[rows, :], mod_ref, 3).astype(BF16)
        qkv = jnp.dot(h, w_ref[...], preferred_element_type=F32)
        cos = jnp.where(is_prompt, 1.0, cos_ref[rows, :])
        sin = jnp.where(is_prompt, 0.0, sin_ref[rows, :])

        def rope(blk, c, s):
            up = pltpu.roll(blk, LANES - 16, 1)
            dn = pltpu.roll(blk, 16, 1)
            return blk * c + jnp.where(first_half, up, dn) * s

        cos_q, sin_q = cos * Q_SCALE, sin * Q_SCALE
        for j in range(D_Q // LANES):
            cols = slice(j * LANES, (j + 1) * LANES)
            q_ref[rows, cols] = rope(qkv[:, cols], cos_q, sin_q).astype(BF16)
        for pair in range(D_KV // LANES):
            k_cols = slice(D_Q + pair * LANES, D_Q + (pair + 1) * LANES)
            v_cols = slice(D_Q + D_KV + pair * LANES, D_Q + D_KV + (pair + 1) * LANES)
            _write_ab(kab_ref, rows, pair, rope(qkv[:, k_cols], cos, sin))
            _write_ab(vab_ref, rows, pair, qkv[:, v_cols])
        kv_raw.append(qkv[:, D_Q:])

    @pl.when(pl.program_id(0) < N_PROMPT_TILES)
    def _():
        kv_t = jnp.concatenate(kv_raw, axis=0).T
        for s in range(TM // SEQ):
            ks_ref[s] = kv_t[:D_KV, s * SEQ:(s + 1) * SEQ]
            vs_ref[s] = kv_t[D_KV:, s * SEQ:(s + 1) * SEQ]


def _rope_index(i):
    return (jnp.maximum(i - N_PROMPT_TILES, 0) % ROPE_TILES, 0)


def _qkv(x, mods, w, cos_t, sin_t):
    seqs_per_tile = TM // SEQ
    state_spec = pl.BlockSpec((seqs_per_tile, D_KV, SEQ),
                              lambda i: (jnp.minimum(i, N_PROMPT_TILES - 1), 0, 0))
    return pl.pallas_call(
        _qkv_kernel,
        grid=(N_TOK // TM,),
        in_specs=[
            _slab_spec(D_MODEL),
            _mod_spec(TM, 0),
            _resident((D_MODEL, D_Q + 2 * D_KV)),
            pl.BlockSpec((TM, LANES), _rope_index),
            pl.BlockSpec((TM, LANES), _rope_index),
        ],
        out_specs=[
            _slab_spec(D_Q), _slab_spec(D_KV_AB), _slab_spec(D_KV_AB), state_spec, state_spec,
        ],
        out_shape=[
            jax.ShapeDtypeStruct((N_TOK, D_Q), BF16),
            jax.ShapeDtypeStruct((N_TOK, D_KV_AB), BF16),
            jax.ShapeDtypeStruct((N_TOK, D_KV_AB), BF16),
            jax.ShapeDtypeStruct((BATCH, D_KV, SEQ), F32),
            jax.ShapeDtypeStruct((BATCH, D_KV, SEQ), F32),
        ],
        compiler_params=_params(),
        name="qkv",
    )(x, mods, w, cos_t, sin_t)


def _attend(q_ref, o_ref, sink_ref, k_slabs, v_slabs, masks, tq, row0=0):
    nt = (((1,), (1,)), ((), ()))
    rows = slice(row0, row0 + tq)
    row_hi = lax.broadcasted_iota(jnp.int32, (2 * tq, 1), 0) >= tq
    lane_lo = lax.broadcasted_iota(jnp.int32, (2 * tq, LANES), 1) < HALF_LANES
    for kh in range(N_KV_HEADS):
        ks = k_slabs(kh)
        vs = v_slabs(kh)
        k_cat = jnp.concatenate([s[:, :LANES] for s in ks] + [s[:, LANES:] for s in ks], axis=0)
        v_cat = jnp.concatenate([s[:, :LANES] for s in vs] + [s[:, LANES:] for s in vs], axis=0)
        n_keys = k_cat.shape[0] // 2
        j0 = 2 * kh
        q2 = jnp.concatenate([q_ref[rows, j0 * LANES:(j0 + 1) * LANES],
                              q_ref[rows, (j0 + 1) * LANES:(j0 + 2) * LANES]], axis=0)
        s = lax.dot_general(q2, k_cat, nt, preferred_element_type=F32)
        es, inv_dens = [], []
        for half in range(2):
            segs, off = [], half * n_keys
            for slab, m in zip(ks, masks):
                seg = s[:, off:off + slab.shape[0]]
                segs.append(seg if m is None else jnp.where(m, seg, NEG_INF))
                off += slab.shape[0]
            logits = jnp.concatenate(segs, axis=1)
            sink = LOG2_E * jnp.where(row_hi, sink_ref[4 * kh + 2 + half], sink_ref[4 * kh + half])
            m_row = jnp.maximum(jnp.max(logits, axis=-1, keepdims=True), sink)
            e = jnp.exp2(logits - m_row)
            den = jnp.sum(e, axis=-1, keepdims=True) + jnp.exp2(sink - m_row)
            es.append(e.astype(BF16))
            inv_dens.append(1.0 / den)
        p = jnp.concatenate(es, axis=1)
        o2 = jnp.dot(p, v_cat, preferred_element_type=F32)
        o2 = o2 * jnp.where(lane_lo, inv_dens[0], inv_dens[1])
        o_ref[rows, j0 * LANES:(j0 + 1) * LANES] = o2[:tq].astype(BF16)
        o_ref[rows, (j0 + 1) * LANES:(j0 + 2) * LANES] = o2[tq:].astype(BF16)


def _kv_cols(kh):
    return slice(kh * 2 * LANES, (kh + 1) * 2 * LANES)


CTX_SUB = 4


def _ctx_attn_kernel(sink_ref, q_ref, k_ref, v_ref, o_ref):
    for sub in range(CTX_SUB):
        rows = slice(sub * SEQ, (sub + 1) * SEQ)
        _attend(q_ref, o_ref, sink_ref,
                lambda kh, rows=rows: [k_ref[rows, _kv_cols(kh)]],
                lambda kh, rows=rows: [v_ref[rows, _kv_cols(kh)]],
                [None], SEQ, row0=sub * SEQ)


def _ctx_attn(sink, q, kab, vab):
    step_rows = CTX_SUB * SEQ
    return pl.pallas_call(
        _ctx_attn_kernel,
        grid=(BATCH // CTX_SUB,),
        in_specs=[
            pl.BlockSpec(memory_space=pltpu.SMEM),
            pl.BlockSpec((step_rows, D_Q), lambda b: (b, 0)),
            pl.BlockSpec((step_rows, D_KV_AB), lambda b: (b, 0)),
            pl.BlockSpec((step_rows, D_KV_AB), lambda b: (b, 0)),
        ],
        out_specs=pl.BlockSpec((step_rows, D_Q), lambda b: (b, 0)),
        out_shape=jax.ShapeDtypeStruct((N_PROMPT_TOK, D_Q), BF16),
        compiler_params=_params(),
        name="ctx_attn",
    )(sink, q, kab, vab)


N_QBLK = DEC_SEQ // BLOCK


LAT_SUB = 4
LAT_STEPS = N_QBLK // LAT_SUB


def _lat_attn_kernel(sink_ref, q_ref, kp_ref, kc_ref, kn_ref, vp_ref, vc_ref, vn_ref,
                     kx_ref, vx_ref, o_ref):
    step = pl.program_id(1)
    r = lax.broadcasted_iota(jnp.int32, (2 * BLOCK, BLOCK), 0) & (BLOCK - 1)
    c = lax.broadcasted_iota(jnp.int32, (2 * BLOCK, BLOCK), 1)

    def blocks(p_ref, c_ref, n_ref, kh):
        cols = _kv_cols(kh)
        return ([p_ref[:, cols]]
                + [c_ref[j * BLOCK:(j + 1) * BLOCK, cols] for j in range(LAT_SUB)]
                + [n_ref[:, cols]])

    for sub in range(LAT_SUB):
        no_prev = (step == 0) if sub == 0 else False
        no_next = (step == LAT_STEPS - 1) if sub == LAT_SUB - 1 else False
        m_prev = c >= r + jnp.where(no_prev, BLOCK, 0)
        m_next = c <= r - jnp.where(no_next, BLOCK, 0)

        def slabs(p_ref, c_ref, n_ref, x_ref, sub=sub):
            return lambda kh: blocks(p_ref, c_ref, n_ref, kh)[sub:sub + 3] + [x_ref[0, :, _kv_cols(kh)]]

        _attend(q_ref, o_ref, sink_ref,
                slabs(kp_ref, kc_ref, kn_ref, kx_ref), slabs(vp_ref, vc_ref, vn_ref, vx_ref),
                [m_prev, None, m_next, None], BLOCK, row0=sub * BLOCK)


def _lat_attn(sink, q, kab, vab, kab_ctx, vab_ctx):
    tq = LAT_SUB * BLOCK
    first = N_PROMPT_TOK // BLOCK

    def cur(b, i):
        return (N_PROMPT_TOK // tq + b * LAT_STEPS + i, 0)

    def prev(b, i):
        return (first + b * N_QBLK + jnp.maximum(i * LAT_SUB - 1, 0), 0)

    def nxt(b, i):
        return (first + b * N_QBLK + jnp.minimum((i + 1) * LAT_SUB, N_QBLK - 1), 0)

    edge_spec = lambda f: pl.BlockSpec((BLOCK, D_KV_AB), f)
    cur_spec = pl.BlockSpec((tq, D_KV_AB), cur)
    ctx_spec = pl.BlockSpec((1, PAST_LEN, D_KV_AB), lambda b, i: (b, 0, 0))
    return pl.pallas_call(
        _lat_attn_kernel,
        grid=(DEC_BATCH, LAT_STEPS),
        in_specs=[
            pl.BlockSpec(memory_space=pltpu.SMEM),
            pl.BlockSpec((tq, D_Q), cur),
            edge_spec(prev), cur_spec, edge_spec(nxt),
            edge_spec(prev), cur_spec, edge_spec(nxt),
            ctx_spec, ctx_spec,
        ],
        out_specs=pl.BlockSpec((tq, D_Q), lambda b, i: (b * LAT_STEPS + i, 0)),
        out_shape=jax.ShapeDtypeStruct((N_LATENT_TOK, D_Q), BF16),
        compiler_params=_params(2),
        name="lat_attn",
    )(sink, q, kab, kab, kab, vab, vab, vab, kab_ctx, vab_ctx)


def _proj_kernel(x_ref, ap_ref, al_ref, mod_ref, lng_ref, lnb_ref, wo_ref, o_ref):
    a = _read_split(ap_ref, al_ref)
    gate = mod_ref[5:6, :]
    for r in range(TM // EPILOGUE_ROWS):
        rows = slice(r * EPILOGUE_ROWS, (r + 1) * EPILOGUE_ROWS)
        f = jnp.dot(a[rows], wo_ref[...], preferred_element_type=F32)
        y = DEEPNORM_ALPHA * x_ref[rows, :] + gate * f
        o_ref[rows, :] = _layer_norm(y, lng_ref[...], lnb_ref[...])


def _proj(x, attn_prompt, attn_latent, mods, ln_g, ln_b, w_o):
    return pl.pallas_call(
        _proj_kernel,
        grid=(N_TOK // TM,),
        in_specs=[
            _slab_spec(D_MODEL),
            _prompt_spec(D_Q),
            _latent_spec(D_Q),
            _mod_spec(TM, 0)] + _ln_specs(0, 1) + [
            _resident((D_Q, D_MODEL)),
        ],
        out_specs=_slab_spec(D_MODEL),
        out_shape=jax.ShapeDtypeStruct((N_TOK, D_MODEL), F32),
        compiler_params=_params(),
        name="attn_proj",
    )(x, attn_prompt, attn_latent, mods, ln_g, ln_b, w_o)


POOL_SUB = 4
POOL_STEP_ROWS = POOL_SUB * POOL_TM
PROMPT_POOL_STEPS = N_PROMPT_TOK // POOL_STEP_ROWS
LATENT_POOL_STEPS = DEC_SEQ // POOL_STEP_ROWS


def _pool_kernel(x_ref, xp_ref, xn_ref, mod_ref, lng_ref, lnb_ref, w_ref, sc_ref, o_ref):
    i = pl.program_id(0)
    is_prompt = i < PROMPT_POOL_STEPS
    in_seq = (i - PROMPT_POOL_STEPS) % LATENT_POOL_STEPS
    h_all = _modulate(x_ref[...], mod_ref, 3)
    h_before = _modulate(xp_ref[...], mod_ref, 3)
    h_after = _modulate(xn_ref[...], mod_ref, 3)
    gate_scale = mod_ref[5:6, :] * sc_ref[...]
    for sub in range(POOL_SUB):
        rows = slice(sub * POOL_TM, (sub + 1) * POOL_TM)
        is_start = is_prompt | (in_seq == 0) if sub == 0 else is_prompt
        is_end = is_prompt | (in_seq == LATENT_POOL_STEPS - 1) if sub == POOL_SUB - 1 else is_prompt
        before = h_before if sub == 0 else h_all[rows.start - POOL_HALO:rows.start]
        after = h_after if sub == POOL_SUB - 1 else h_all[rows.stop:rows.stop + POOL_HALO]
        _pool_sub_tile(x_ref[rows, :], h_all[rows], before, after, is_start, is_end, gate_scale,
                       lng_ref, lnb_ref, w_ref, o_ref, rows)


def _pool_sub_tile(x, h, before, after, is_start, is_end, gate_scale, lng_ref, lnb_ref, w_ref,
                   o_ref, rows):
    h_ext = jnp.concatenate([jnp.where(is_start, 0.0, before), h, jnp.where(is_end, 0.0, after)],
                            axis=0)
    n_ext = POOL_TM + 2 * POOL_HALO

    r8 = lax.broadcasted_iota(jnp.int32, (POOL_HALO, POOL_GROUP_DIM), 0)
    outs = []
    for gi, w in enumerate(POOL_WINDOWS):
        half = w // 2
        cols = slice(gi * POOL_GROUP_DIM, (gi + 1) * POOL_GROUP_DIM)
        acc = h_ext[:, cols]
        s = 1
        while s < w:
            acc = acc + pltpu.roll(acc, s, 0)
            s *= 2
        if half > 1:
            acc = pltpu.roll(acc, n_ext - (half - 1), 0)
        total = acc[POOL_HALO:POOL_HALO + POOL_TM]
        cnt_top = w - jnp.where(is_start, jnp.maximum(half - r8, 0), 0)
        cnt_bot = w - jnp.where(is_end, jnp.maximum(r8 + half - POOL_HALO, 0), 0)
        inv_cnt = jnp.concatenate([
            1.0 / cnt_top.astype(F32),
            jnp.full((POOL_TM - 2 * POOL_HALO, POOL_GROUP_DIM), 1.0 / w, F32),
            1.0 / cnt_bot.astype(F32)], axis=0)
        pooled = (total * inv_cnt - h[:, cols]).astype(BF16)
        outs.append(jnp.dot(pooled, w_ref[gi], preferred_element_type=F32))
    y = DEEPNORM_ALPHA * x + gate_scale * jnp.concatenate(outs, axis=-1)
    o_ref[rows, :] = _layer_norm(y, lng_ref[...], lnb_ref[...])


def _pool(x, mods, ln_g, ln_b, w_pool, scale):
    halo_per_tile = POOL_STEP_ROWS // POOL_HALO
    last_halo = N_TOK // POOL_HALO - 1
    return pl.pallas_call(
        _pool_kernel,
        grid=(N_TOK // POOL_STEP_ROWS,),
        in_specs=[
            pl.BlockSpec((POOL_STEP_ROWS, D_MODEL), lambda i: (i, 0)),
            pl.BlockSpec((POOL_HALO, D_MODEL), lambda i: (jnp.maximum(i * halo_per_tile - 1, 0), 0)),
            pl.BlockSpec((POOL_HALO, D_MODEL),
                         lambda i: (jnp.minimum((i + 1) * halo_per_tile, last_halo), 0)),
            _mod_spec(POOL_STEP_ROWS, 1)] + _ln_specs(1, 1) + [
            _resident((len(POOL_WINDOWS), POOL_GROUP_DIM, POOL_GROUP_DIM)),
            _resident((1, D_MODEL)),
        ],
        out_specs=pl.BlockSpec((POOL_STEP_ROWS, D_MODEL), lambda i: (i, 0)),
        out_shape=jax.ShapeDtypeStruct((N_TOK, D_MODEL), F32),
        compiler_params=_params(),
        name="pool",
    )(x, x, x, mods, ln_g, ln_b, w_pool, scale.reshape(1, D_MODEL))


def _ab_layout(a):
    lead = a.shape[:-1]
    a = a.reshape(lead + (N_KV_HEADS, 1, HEAD_DIM)).astype(BF16)
    z = jnp.zeros_like(a)
    return jnp.concatenate([a, z, z, a], axis=-2).reshape(lead + (D_KV_AB,))


def _rope_tables():
    n_rows = DEC_SEQ // GRID_W
    inv = jnp.power(ROPE_BASE, -jnp.arange(N_FREQ, dtype=F32) / N_FREQ)

    def expand(fn):
        by_row = fn(jnp.arange(n_rows, dtype=F32)[:, None] * inv)
        by_col = fn(jnp.arange(GRID_W, dtype=F32)[:, None] * inv)
        r = jnp.broadcast_to(by_row[:, None, :], (n_rows, GRID_W, N_FREQ))
        c = jnp.broadcast_to(by_col[None, :, :], (n_rows, GRID_W, N_FREQ))
        per_head = jnp.concatenate([r, r, c, c], axis=-1).reshape(DEC_SEQ, HEAD_DIM)
        return jnp.concatenate([per_head] * (LANES // HEAD_DIM), axis=-1)

    cos, sin = expand(jnp.cos), expand(jnp.sin)
    first_half = (jnp.arange(LANES) % 32) < 16
    return cos, jnp.where(first_half, -sin, sin)


def kernel(x_prompt, x_sample, cache_k, cache_v, c, c_ctx, w_mod, b_mod, ln_g, ln_b,
           ffn_w_gate, ffn_w_up, ffn_w_down, attn_w_qkv, attn_w_o, attn_sink,
           pool_w, pool_scale):
    cond = jnp.concatenate(
        [c_ctx[None, :], c, jnp.zeros((COND_ROWS - N_COND, D_MODEL), F32)], axis=0)
    mods = _adaln(cond, w_mod, b_mod)

    ffn_w = (ffn_w_gate, ffn_w_up, ffn_w_down)
    ln_g = ln_g.reshape(DEPTH * 3, 1, D_MODEL)
    ln_b = ln_b.reshape(DEPTH * 3, 1, D_MODEL)
    ffn = functools.partial(_ffn, mods=mods, ln_g=ln_g, ln_b=ln_b)

    x, w_next = ffn((x_prompt.reshape(N_PROMPT_TOK, D_MODEL), x_sample.reshape(N_LATENT_TOK, D_MODEL)),
                    weights=None, layer=0, which=0, next_f32=(ffn_w, (0, 1)))
    cos_t, sin_t = _rope_tables()
    q, kab, vab, k_state, v_state = _qkv(x, mods, attn_w_qkv[0].astype(BF16), cos_t, sin_t)
    sink = attn_sink[0]
    o_ctx = _ctx_attn(sink, q, kab, vab)
    kab_ctx = _ab_layout(cache_k[:, 0].reshape(DEC_BATCH, PAST_LEN, D_KV))
    vab_ctx = _ab_layout(cache_v[:, 0].reshape(DEC_BATCH, PAST_LEN, D_KV))
    o_lat = _lat_attn(sink, q, kab, vab, kab_ctx, vab_ctx)
    x = _proj(x, o_ctx, o_lat, mods, ln_g, ln_b, attn_w_o[0].astype(BF16))
    x, w_next = ffn((x,), weights=w_next, layer=0, which=1, next_f32=(ffn_w, (1, 0)))

    x, w_next = ffn((x,), weights=w_next, layer=1, which=0, next_f32=(ffn_w, (1, 1)))
    x = _pool(x, mods, ln_g, ln_b, pool_w[0].astype(BF16), pool_scale[0])
    (y_prompt, y_sample), _ = ffn((x,), weights=w_next, layer=1, which=1, split_out=True)

    def state(s_t):
        s_t = s_t.reshape(BATCH, N_KV_HEADS, HEAD_DIM, SEQ)
        return jnp.transpose(s_t, (0, 3, 1, 2))[:, None]

    return (y_prompt.reshape(BATCH, SEQ, D_MODEL),
            y_sample.reshape(DEC_BATCH, DEC_SEQ, D_MODEL),
            state(k_state), state(v_state))
```
